```python
import jax, jax.numpy as jnp
from jax import lax
import numpy as np


D_MODEL = 2048
BATCH = 8
SEQ = 2048
DEPTH = 2

CTX_LEN = 256
GRID_W = 64
NORM_EPS = 1e-6
RWKV_DIM = D_MODEL // 2
RWKV_HEAD = 64
RWKV_HEADS = RWKV_DIM // RWKV_HEAD
DECAY_LORA = 96
ICLR_LORA = 96
GATE_LORA = 256
RWKV_GN_EPS = 64e-5
GLA_HEADS = 4
GLA_KDIM = D_MODEL // 4
GLA_VDIM = D_MODEL // 2
GLA_DK = GLA_KDIM // GLA_HEADS
GLA_DV = GLA_VDIM // GLA_HEADS
GLA_GATE_LORA = 16
GLA_TAU = 16.0
GLA_CHUNK = 64
GLA_CONV = 3
N_EXPERTS = 16
N_GROUPS = 4
EXPERTS_PER_GROUP = N_EXPERTS // N_GROUPS
TOP_K = 2
D_EXPERT = 1408
RWKV_IN = 3 * RWKV_DIM + DECAY_LORA + ICLR_LORA + GATE_LORA
RWKV_SPLITS = (RWKV_DIM, 2 * RWKV_DIM, 3 * RWKV_DIM, 3 * RWKV_DIM + DECAY_LORA, 3 * RWKV_DIM + DECAY_LORA + ICLR_LORA)
GLA_QKV = 2 * GLA_KDIM + GLA_VDIM
GLA_SEQ_IN = GLA_QKV + GLA_GATE_LORA
GLA_GATE_OFF = RWKV_IN + GLA_SEQ_IN
BR_GATE_OFF = GLA_GATE_OFF + GLA_VDIM
IN_DIM = BR_GATE_OFF + 2 * D_MODEL

kernel_name = 'hybrid_rwkv7_gla_moe_prefix_dit'


def rms_norm(x, g):
    x32 = x.astype(jnp.float32)
    y = x32 * lax.rsqrt(jnp.mean(x32 * x32, -1, keepdims=True) + NORM_EPS)
    return (y * g).astype(x.dtype)


def modulate(h, shift, scale):
    return h * (1.0 + scale) + shift


def centred_shift(p, mu):
    prev = jnp.pad(p, ((0, 0), (1, 0), (0, 0)))[:, :-1]
    nxt = jnp.pad(p, ((0, 0), (0, 1), (0, 0)))[:, 1:]
    return p + mu[0] * (prev - p) + mu[1] * (nxt - p)


def depthwise_conv(u, w):
    return lax.conv_general_dilated(u, w[:, None, :].astype(u.dtype), window_strides=(1,), padding='SAME',
                                    dimension_numbers=('NWC', 'WIO', 'NWC'), feature_group_count=u.shape[-1])


def to_column_major(t, rows):
    b, l, ch = t.shape
    return t.reshape(b, rows, GRID_W, ch).transpose(0, 2, 1, 3).reshape(b, l, ch)


def to_row_major(t, rows):
    b, l, ch = t.shape
    return t.reshape(b, GRID_W, rows, ch).transpose(0, 2, 1, 3).reshape(b, l, ch)


def rwkv_scan(s0, r, w, k, v, kk, a, reverse):
    def step(s, inp):
        r_t, w_t, k_t, v_t, kk_t, a_t = inp
        sa = jnp.einsum('bhvk,bhk->bhv', s, kk_t)
        s = s * w_t[:, :, None, :] - sa[..., None] * (kk_t * a_t)[:, :, None, :] + v_t[..., None] * k_t[:, :, None, :]
        return s, jnp.einsum('bhvk,bhk->bhv', s, r_t)
    xs = tuple(jnp.moveaxis(t, 1, 0) for t in (r, w, k, v, kk, a))
    s_fin, ys = lax.scan(step, s0, xs, reverse=reverse)
    return jnp.moveaxis(ys, 0, 1), s_fin


def rwkv_stream(p, init, mu, w0, w2, a0, a2, g2, k_k, k_a, r_k, ln_g, ln_b):
    b, l, _ = p.shape
    f32 = jnp.float32
    hd = lambda t: t.astype(f32).reshape(t.shape[:-1] + (RWKV_HEADS, RWKV_HEAD))
    s = centred_shift(p, mu)
    r, k, v, xw, xa, xg = jnp.split(s, RWKV_SPLITS, axis=-1)
    r_h, k_h, v_h = hd(r), hd(k), hd(v)
    kk = hd(k * k_k)
    kk = kk / jnp.maximum(jnp.sqrt(jnp.sum(kk * kk, -1, keepdims=True)), 1e-12)
    y = jnp.zeros_like(v_h)
    finals = []
    for d in range(2):
        w_log = -jax.nn.softplus(-(w0[d] + jnp.tanh(xw) @ w2[d]).astype(f32)) - 0.5
        decay = hd(jnp.exp(-jnp.exp(w_log)))
        a = hd(jax.nn.sigmoid((a0[d] + xa @ a2[d]).astype(f32)))
        k_d = k_h * (1.0 + (a - 1.0) * hd(k_a))
        y_d, s_d = rwkv_scan(init[d], r_h, decay, k_d, v_h, kk, a, reverse=(d == 1))
        y = y + y_d
        finals.append(s_d)
    mean = jnp.mean(y, -1, keepdims=True)
    var = jnp.mean(jnp.square(y - mean), -1, keepdims=True)
    yn = ((y - mean) * lax.rsqrt(var + RWKV_GN_EPS)).reshape(b, l, RWKV_DIM) * ln_g + ln_b
    bonus = (jnp.sum(r_h * k_h * r_k, -1, keepdims=True) * v_h).reshape(b, l, RWKV_DIM)
    gate = jax.nn.sigmoid(xg) @ g2
    return (yn + bonus).astype(p.dtype) * gate, finals


def gla_chunked(q, k, v, g, s0):
    b, l, h, _ = q.shape
    dv = v.shape[-1]
    n = l // GLA_CHUNK
    blk = lambda t: t.astype(jnp.float32).reshape(b, n, GLA_CHUNK, h, t.shape[-1])
    q, k, v, g = blk(q), blk(k), blk(v), blk(g)
    gc = jnp.cumsum(g, axis=2)
    g_last = gc[:, :, -1]
    q_g = q * jnp.exp(gc)
    k_g = k * jnp.exp(-gc)
    k_end = k * jnp.exp(g_last[:, :, None] - gc)
    mask = jnp.tril(jnp.ones((GLA_CHUNK, GLA_CHUNK), jnp.float32))
    att = jnp.einsum('bnihd,bnjhd->bnhij', q_g, k_g) * mask
    o_intra = jnp.einsum('bnhij,bnjhv->bnihv', att, v)
    ds = jnp.einsum('bnjhd,bnjhv->bnhdv', k_end, v)

    def step(s, inp):
        ds_c, dec_c = inp
        return s * dec_c[..., None] + ds_c, s
    s_fin, s_prev = lax.scan(step, s0, (jnp.moveaxis(ds, 1, 0), jnp.moveaxis(jnp.exp(g_last), 1, 0)))
    o_inter = jnp.einsum('bnihd,nbhdv->bnihv', q_g, s_prev)
    return (o_intra + o_inter).reshape(b, l, h, dv), s_fin


def gla_stream(u, init, conv_w, alpha_w2, alpha_b, norm_g):
    b, l, _ = u.shape
    f32 = jnp.float32
    qkv = jax.nn.silu(depthwise_conv(u[..., :GLA_QKV], conv_w))
    q = qkv[..., :GLA_KDIM].reshape(b, l, GLA_HEADS, GLA_DK) * (GLA_DK ** -0.5)
    k = qkv[..., GLA_KDIM:2 * GLA_KDIM].reshape(b, l, GLA_HEADS, GLA_DK)
    v = qkv[..., 2 * GLA_KDIM:].reshape(b, l, GLA_HEADS, GLA_DV)
    ad = u[..., GLA_QKV:]
    o = jnp.zeros((b, l, GLA_HEADS, GLA_DV), f32)
    finals = []
    for d in range(2):
        g = (jax.nn.log_sigmoid((ad @ alpha_w2[d] + alpha_b[d]).astype(f32)) / GLA_TAU).reshape(b, l, GLA_HEADS, GLA_DK)
        if d == 0:
            o_d, s_d = gla_chunked(q, k, v, g, init[d])
        else:
            o_d, s_d = gla_chunked(*(jnp.flip(t, 1) for t in (q, k, v, g)), init[d])
            o_d = jnp.flip(o_d, 1)
        o = o + o_d
        finals.append(s_d)
    o = o * lax.rsqrt(jnp.mean(o * o, -1, keepdims=True) + NORM_EPS) * norm_g
    return o.reshape(b, l, GLA_VDIM).astype(u.dtype), finals


def hybrid_mixer(h_c, h_l, rows, need_ctx, w_in, rwkv_mu, rwkv_w0, rwkv_w2, rwkv_a0, rwkv_a2, rwkv_g2,
                 rwkv_k_k, rwkv_k_a, rwkv_r_k, rwkv_ln_g, rwkv_ln_b, gla_conv, gla_alpha_w2, gla_alpha_b,
                 gla_norm_g, w_branch_a, w_branch_b, w_out):
    b = h_l.shape[0]
    f32 = jnp.float32
    rwkv_args = (rwkv_mu, rwkv_w0, rwkv_w2, rwkv_a0, rwkv_a2, rwkv_g2, rwkv_k_k, rwkv_k_a, rwkv_r_k, rwkv_ln_g, rwkv_ln_b)
    gla_args = (gla_conv, gla_alpha_w2, gla_alpha_b, gla_norm_g)
    p_c = h_c @ w_in
    p_l = h_l @ w_in
    zr = jnp.zeros((b, RWKV_HEADS, RWKV_HEAD, RWKV_HEAD), f32)
    zg = jnp.zeros((b, GLA_HEADS, GLA_DK, GLA_DV), f32)
    ya_c, st_r = rwkv_stream(p_c[..., :RWKV_IN], (zr, zr), *rwkv_args)
    ya_l, _ = rwkv_stream(p_l[..., :RWKV_IN], st_r, *rwkv_args)
    yb_c, st_g = gla_stream(p_c[..., RWKV_IN:GLA_GATE_OFF], (zg, zg), *gla_args)
    yb_l_cm, _ = gla_stream(to_column_major(p_l[..., RWKV_IN:GLA_GATE_OFF], rows), st_g, *gla_args)
    yb_l = to_row_major(yb_l_cm, rows)

    def merge(p, ya, yb):
        yb = yb * jax.nn.silu(p[..., GLA_GATE_OFF:BR_GATE_OFF])
        ga = jax.nn.sigmoid(p[..., BR_GATE_OFF:BR_GATE_OFF + D_MODEL])
        gb = jax.nn.sigmoid(p[..., BR_GATE_OFF + D_MODEL:])
        return (ga * (ya @ w_branch_a) + gb * (yb @ w_branch_b)) @ w_out
    y_l = merge(p_l, ya_l, yb_l)
    y_c = merge(p_c, ya_c, yb_c) if need_ctx else None
    return y_c, y_l


def moe_ffn(h, router_w, router_b, w_gate, w_up, w_down):
    shp = h.shape
    t = h.reshape(-1, shp[-1])
    scores = jax.nn.sigmoid((t @ router_w).astype(jnp.float32))
    sel = scores + router_b.astype(jnp.float32)
    grp_score = jnp.sum(lax.top_k(sel.reshape(-1, N_GROUPS, EXPERTS_PER_GROUP), 2)[0], -1)
    best = jnp.argmax(grp_score, -1)
    in_grp = (jnp.arange(N_EXPERTS) // EXPERTS_PER_GROUP)[None, :] == best[:, None]
    _, idx = lax.top_k(jnp.where(in_grp, sel, -jnp.inf), TOP_K)
    wts = jnp.take_along_axis(scores, idx, -1)
    wts = wts / jnp.sum(wts, -1, keepdims=True)
    gates = jnp.sum(jax.nn.one_hot(idx, N_EXPERTS, dtype=jnp.float32) * wts[..., None], 1).astype(t.dtype)
    out = jnp.zeros_like(t)
    for e in range(N_EXPERTS):
        out = out + gates[:, e:e + 1] * ((jax.nn.silu(t @ w_gate[e]) * (t @ w_up[e])) @ w_down[e])
    return out.reshape(shp)


def ffn_sublayer(xs, norm_g, shift, scale, gate, router_w, router_b, w_gate, w_up, w_down):
    return xs + gate * moe_ffn(modulate(rms_norm(xs, norm_g), shift, scale), router_w, router_b, w_gate, w_up, w_down)


def setup_inputs(seed: int = 0) -> dict:
    key = jax.random.key(seed)
    ks = iter(jax.random.split(key, 40))
    f32 = jnp.float32
    normal = lambda shape, s: jax.random.normal(next(ks), shape, f32) * s
    unif = lambda shape, lo, hi: jax.random.uniform(next(ks), shape, f32, lo, hi)
    L = DEPTH
    return {
        'x': normal((BATCH, SEQ, D_MODEL), 1.0),
        'c': normal((BATCH, D_MODEL), 1.0),
        'ctx': normal((BATCH, CTX_LEN, D_MODEL), 1.0),
        'c_ctx': normal((D_MODEL,), 1.0),
        'w_ada': normal((L, D_MODEL, 6 * D_MODEL), 0.5 * D_MODEL ** -0.5),
        'b_ada': normal((L, 6 * D_MODEL), 0.01),
        'norm_mix_g': 1.0 + normal((L, D_MODEL), 0.05),
        'norm_ffn_g': 1.0 + normal((L, D_MODEL), 0.05),
        'w_in': normal((L, D_MODEL, IN_DIM), D_MODEL ** -0.5),
        'rwkv_mu': unif((L, 2, RWKV_IN), 0.0, 0.5),
        'rwkv_w0': unif((L, 2, RWKV_DIM), -5.0, 0.5),
        'rwkv_w2': normal((L, 2, DECAY_LORA, RWKV_DIM), 0.1 * DECAY_LORA ** -0.5),
        'rwkv_a0': normal((L, 2, RWKV_DIM), 0.5),
        'rwkv_a2': normal((L, 2, ICLR_LORA, RWKV_DIM), 0.1 * ICLR_LORA ** -0.5),
        'rwkv_g2': normal((L, GATE_LORA, RWKV_DIM), GATE_LORA ** -0.5),
        'rwkv_k_k': 0.85 + normal((L, RWKV_DIM), 0.05),
        'rwkv_k_a': 1.0 + normal((L, RWKV_DIM), 0.05),
        'rwkv_r_k': normal((L, RWKV_HEADS, RWKV_HEAD), 0.1),
        'rwkv_ln_g': 1.0 + normal((L, RWKV_DIM), 0.05),
        'rwkv_ln_b': normal((L, RWKV_DIM), 0.01),
        'gla_conv': normal((L, GLA_CONV, GLA_QKV), GLA_CONV ** -0.5),
        'gla_alpha_w2': normal((L, 2, GLA_GATE_LORA, GLA_KDIM), GLA_GATE_LORA ** -0.5),
        'gla_alpha_b': unif((L, 2, GLA_KDIM), 0.0, 4.0),
        'gla_norm_g': 1.0 + normal((L, GLA_DV), 0.05),
        'w_branch_a': normal((L, RWKV_DIM, D_MODEL), RWKV_DIM ** -0.5),
        'w_branch_b': normal((L, GLA_VDIM, D_MODEL), GLA_VDIM ** -0.5),
        'w_out': normal((L, D_MODEL, D_MODEL), D_MODEL ** -0.5),
        'router_w': normal((D_MODEL, N_EXPERTS), D_MODEL ** -0.5),
        'router_b': normal((N_EXPERTS,), 0.01),
        'exp_w_gate': normal((L, N_EXPERTS, D_MODEL, D_EXPERT), D_MODEL ** -0.5),
        'exp_w_up': normal((L, N_EXPERTS, D_MODEL, D_EXPERT), D_MODEL ** -0.5),
        'exp_w_down': normal((L, N_EXPERTS, D_EXPERT, D_MODEL), D_EXPERT ** -0.5),
        'final_norm_g': 1.0 + normal((D_MODEL,), 0.05),
    }


def reference(x, c, ctx, c_ctx, w_ada, b_ada, norm_mix_g, norm_ffn_g, w_in, rwkv_mu, rwkv_w0, rwkv_w2,
              rwkv_a0, rwkv_a2, rwkv_g2, rwkv_k_k, rwkv_k_a, rwkv_r_k, rwkv_ln_g, rwkv_ln_b, gla_conv,
              gla_alpha_w2, gla_alpha_b, gla_norm_g, w_branch_a, w_branch_b, w_out, router_w, router_b,
              exp_w_gate, exp_w_up, exp_w_down, final_norm_g):
    rows = x.shape[1] // GRID_W
    silu_c = jax.nn.silu(c)[:, None, :]
    silu_cc = jax.nn.silu(c_ctx)
    x_lat, x_ctx = x, ctx
    for l in range(DEPTH):
        last = l == DEPTH - 1
        mod_l = jnp.split(silu_c @ w_ada[l] + b_ada[l], 6, axis=-1)
        mod_c = jnp.split(silu_cc @ w_ada[l] + b_ada[l], 6, axis=-1)
        h_l = modulate(rms_norm(x_lat, norm_mix_g[l]), mod_l[0], mod_l[1])
        h_c = modulate(rms_norm(x_ctx, norm_mix_g[l]), mod_c[0], mod_c[1])
        y_c, y_l = hybrid_mixer(h_c, h_l, rows, not last, w_in[l], rwkv_mu[l], rwkv_w0[l], rwkv_w2[l],
                                rwkv_a0[l], rwkv_a2[l], rwkv_g2[l], rwkv_k_k[l], rwkv_k_a[l], rwkv_r_k[l],
                                rwkv_ln_g[l], rwkv_ln_b[l], gla_conv[l], gla_alpha_w2[l], gla_alpha_b[l],
                                gla_norm_g[l], w_branch_a[l], w_branch_b[l], w_out[l])
        x_lat = x_lat + mod_l[2] * y_l
        x_lat = ffn_sublayer(x_lat, norm_ffn_g[l], mod_l[3], mod_l[4], mod_l[5], router_w, router_b,
                             exp_w_gate[l], exp_w_up[l], exp_w_down[l])
        if not last:
            x_ctx = x_ctx + mod_c[2] * y_c
            x_ctx = ffn_sublayer(x_ctx, norm_ffn_g[l], mod_c[3], mod_c[4], mod_c[5], router_w, router_b,
                                 exp_w_gate[l], exp_w_up[l], exp_w_down[l])
    return rms_norm(x_lat, final_norm_g)
```

```python
import functools

import jax
import jax.numpy as jnp
from jax import lax
from jax.experimental import pallas as pl
from jax.experimental.pallas import tpu as pltpu

F32 = jnp.float32
BF16 = jnp.bfloat16

D_MODEL = 2048
DEPTH = 2
GRID_W = 64
NORM_EPS = 1e-6
RWKV_DIM = 1024
RWKV_HEAD = 64
RWKV_HEADS = 16
DECAY_LORA = 96
ICLR_LORA = 96
GATE_LORA = 256
RWKV_GN_EPS = 64e-5
GLA_HEADS = 4
GLA_KDIM = 512
GLA_VDIM = 1024
GLA_DK = 128
GLA_DV = 256
GLA_GATE_LORA = 16
GLA_TAU = 16.0
GLA_QKV = 2048
N_EXPERTS = 16
EXPERTS_PER_GROUP = 4
N_PAIRS = 24
D_EXPERT = 1408
RWKV_IN = 3520
GLA_SEQ_IN = 2064
GLA_GATE_OFF = RWKV_IN + GLA_SEQ_IN
BR_GATE_OFF = GLA_GATE_OFF + GLA_VDIM

LANES = 128
CHUNK = 64
MOD_ROWS = 16
LORA_PAD = 128
RW_XW = 3 * RWKV_DIM
RW_XA = RW_XW + LORA_PAD
RW_XG = RW_XA + LORA_PAD
RW_COLS = RW_XG + GATE_LORA
GLA_COLS = GLA_QKV + LANES
GATE_COLS = GLA_VDIM + 2 * D_MODEL
VMEM_LIMIT = 56 * 1024 * 1024


def _cparams(sem, vmem=VMEM_LIMIT):
    return pltpu.CompilerParams(dimension_semantics=sem, vmem_limit_bytes=vmem)


def _pick_tile(n, cands):
    for c in cands:
        if n % c == 0:
            return c
    raise ValueError(f"no tile for {n}")


def _mm(a, b):
    return jnp.dot(a.astype(BF16), b.astype(BF16), preferred_element_type=F32)


def _mm_nt(a, b):
    return lax.dot_general(a.astype(BF16), b.astype(BF16), (((1,), (1,)), ((), ())), preferred_element_type=F32)


def _mm_tn(a, b):
    return lax.dot_general(a.astype(BF16), b.astype(BF16), (((0,), (0,)), ((), ())), preferred_element_type=F32)


def _split2(a):
    hi = a.astype(BF16)
    lo = (a - hi.astype(F32)).astype(BF16)
    return hi, lo


def _split3(a):
    hi = a.astype(BF16)
    r1 = a - hi.astype(F32)
    mid = r1.astype(BF16)
    lo = (r1 - mid.astype(F32)).astype(BF16)
    return hi, mid, lo


def _mm_x3(a, b):
    ah, al = _split2(a)
    bh, bl = _split2(b)
    d = functools.partial(jnp.dot, preferred_element_type=F32)
    return d(ah, bh) + (d(al, bh) + d(ah, bl))


def _mm_mask(m01, x):
    h, m, l = _split3(x)
    d = functools.partial(jnp.dot, preferred_element_type=F32)
    return d(m01, h) + (d(m01, m) + d(m01, l))


def _sigmoid(x):
    return 1.0 / (1.0 + jnp.exp(-x))


def _silu(x):
    return x * _sigmoid(x)


def _softplus(z):
    return jnp.maximum(z, 0.0) + jnp.log1p(jnp.exp(-jnp.abs(z)))


def _head_sum64(x):
    ri = lax.broadcasted_iota(jnp.int32, (256, 256), 0) // RWKV_HEAD
    ci = lax.broadcasted_iota(jnp.int32, (256, 256), 1) // RWKV_HEAD
    ones = (ri == ci).astype(BF16)
    hi, lo = _split2(x)
    d = functools.partial(jnp.dot, preferred_element_type=F32)
    outs = [d(hi[:, g * 256:(g + 1) * 256], ones) + d(lo[:, g * 256:(g + 1) * 256], ones)
            for g in range(x.shape[1] // 256)]
    return jnp.concatenate(outs, axis=1)


def _adaln_kernel(c_ref, w_ref, b_ref, o_ref):
    c = c_ref[...]
    o_ref[...] = _mm_x3(_silu(c), w_ref[...]) + b_ref[...]


def _adaln(c_all, w_ada, b_ada):
    nl, d, n = w_ada.shape
    tn = 1024
    return pl.pallas_call(
        _adaln_kernel,
        out_shape=jax.ShapeDtypeStruct((nl, MOD_ROWS, n), F32),
        grid=(nl, n // tn),
        in_specs=[pl.BlockSpec((MOD_ROWS, d), lambda l, j: (0, 0)),
                  pl.BlockSpec((None, d, tn), lambda l, j: (l, 0, j)),
                  pl.BlockSpec((None, 1, tn), lambda l, j: (l, 0, j))],
        out_specs=pl.BlockSpec((None, MOD_ROWS, tn), lambda l, j: (l, 0, j)),
        compiler_params=_cparams(("parallel", "parallel")),
        name="adaln",
    )(c_all, w_ada, b_ada.reshape(nl, 1, n))


def _rms(x, g):
    return x * lax.rsqrt(jnp.mean(x * x, axis=-1, keepdims=True) + NORM_EPS) * g


def _prenorm_kernel(*refs, has_moe, final):
    if has_moe:
        x_ref, o0_ref, o1_ref, mprev_ref, mcur_ref, g_ref = refs[:6]
        outs = refs[6:]
        x = x_ref[...] + mprev_ref[0, 5:6, :] * (o0_ref[...] + o1_ref[...])
    else:
        x_ref, mcur_ref, g_ref = refs[:3]
        outs = refs[3:]
        x = x_ref[...]
    y = _rms(x, g_ref[...])
    if final:
        outs[0][...] = y
        return
    h = y * (1.0 + mcur_ref[0, 1:2, :]) + mcur_ref[0, 0:1, :]
    if has_moe:
        outs[0][...] = x
        outs[1][...] = h.astype(outs[1].dtype)
    else:
        outs[0][...] = h.astype(outs[0].dtype)


def _mod_row_map(n_lat_tiles, tiles_per_batch, n_batch):
    return lambda i: (jnp.where(i < n_lat_tiles, i // tiles_per_batch, n_batch), 0, 0)


def _prenorm(x, moe_out, mods_prev, mods_cur, g, *, tm, n_lat_tiles, tiles_per_batch, n_batch, n_rows, final=False):
    d = x.shape[1]
    has_moe = moe_out is not None
    mrow = _mod_row_map(n_lat_tiles, tiles_per_batch, n_batch)
    row = pl.BlockSpec((tm, d), lambda i: (i, 0))
    mspec = pl.BlockSpec((1, 6, d), mrow)
    gspec = pl.BlockSpec((1, d), lambda i: (0, 0))
    ins, specs = [x], [row]
    if has_moe:
        ins += [moe_out, moe_out, mods_prev]
        specs += [pl.BlockSpec((None, tm, d), lambda i: (0, i, 0)),
                  pl.BlockSpec((None, tm, d), lambda i: (1, i, 0)), mspec]
    ins += [mods_cur, g.reshape(1, d)]
    specs += [mspec, gspec]
    if final:
        out_shape = [jax.ShapeDtypeStruct((n_rows, d), F32)]
    elif has_moe:
        out_shape = [jax.ShapeDtypeStruct((n_rows, d), F32), jax.ShapeDtypeStruct((n_rows, d), BF16)]
    else:
        out_shape = [jax.ShapeDtypeStruct((n_rows, d), BF16)]
    return pl.pallas_call(
        functools.partial(_prenorm_kernel, has_moe=has_moe, final=final),
        out_shape=out_shape,
        grid=(n_rows // tm,),
        in_specs=specs,
        out_specs=[row] * len(out_shape),
        compiler_params=_cparams(("parallel",)),
        name="prenorm_final" if final else "prenorm",
    )(*ins)


def _matmul_kernel(a_ref, w_ref, o_ref):
    o_ref[...] = jnp.dot(a_ref[...], w_ref[...], preferred_element_type=F32).astype(o_ref.dtype)


def _matmul(a, w, out_dtype, *, tn):
    m, k = a.shape
    n = w.shape[1]
    tm = _pick_tile(m, (1024, 512, 256, 128, 64))
    return pl.pallas_call(
        _matmul_kernel,
        out_shape=jax.ShapeDtypeStruct((m, n), out_dtype),
        grid=(m // tm, n // tn),
        in_specs=[pl.BlockSpec((tm, k), lambda i, j: (i, 0)),
                  pl.BlockSpec((k, tn), lambda i, j: (0, j))],
        out_specs=pl.BlockSpec((tm, tn), lambda i, j: (i, j)),
        compiler_params=_cparams(("parallel", "parallel")),
        name="in_proj",
    )(a, w)


def _seq_flags(i, n_lat_tiles, tps_lat, tps_ctx):
    is_lat = i < n_lat_tiles
    j = jnp.where(is_lat, i % tps_lat, (i - n_lat_tiles) % tps_ctx)
    return j == 0, j == jnp.where(is_lat, tps_lat - 1, tps_ctx - 1)


def _neighbours(ref, prev_ref, next_ref, c0, c1, first, last):
    tm = ref.shape[0]
    p = ref[:, c0:c1]
    rows = lax.broadcasted_iota(jnp.int32, (tm, 1), 0)
    pr = jnp.where(first, 0.0, prev_ref[7:8, c0:c1])
    nx = jnp.where(last, 0.0, next_ref[0:1, c0:c1])
    prev = jnp.where(rows == 0, pr, pltpu.roll(p, 1, 0))
    nxt = jnp.where(rows == tm - 1, nx, pltpu.roll(p, tm - 1, 0))
    return p, prev, nxt


def _halo_specs(tm, cols, n_rows):
    r8 = tm // 8
    last8 = n_rows // 8 - 1
    return [pl.BlockSpec((tm, cols), lambda i: (i, 0)),
            pl.BlockSpec((8, cols), lambda i: (jnp.maximum(i * r8 - 1, 0), 0)),
            pl.BlockSpec((8, cols), lambda i: (jnp.minimum((i + 1) * r8, last8), 0))]


def _rwkv_prep_kernel(p_ref, pp_ref, pn_ref, mu_ref, w0_ref, w2_ref, a0_ref, a2_ref, g2_ref, kk_ref, ka_ref, rk_ref,
                      r_o, v_o, kap_o, lw_o, b_o, kd_o, bonus_o, gate_o, *, n_lat_tiles, tps_lat, tps_ctx):
    first, last = _seq_flags(pl.program_id(0), n_lat_tiles, tps_lat, tps_ctx)

    def shifted(c0, c1):
        p, prev, nxt = _neighbours(p_ref, pp_ref, pn_ref, c0, c1, first, last)
        return p + mu_ref[0:1, c0:c1] * (prev - p) + mu_ref[1:2, c0:c1] * (nxt - p)

    r = shifted(0, RWKV_DIM)
    k = shifted(RWKV_DIM, 2 * RWKV_DIM)
    v = shifted(2 * RWKV_DIM, 3 * RWKV_DIM)
    kk = k * kk_ref[...]
    kap = kk / jnp.maximum(jnp.sqrt(_head_sum64(kk * kk)), 1e-12)
    r_o[...] = r
    v_o[...] = v
    kap_o[...] = kap
    bonus_o[...] = _head_sum64(r * k * rk_ref[...]) * v
    txw = jnp.tanh(shifted(RW_XW, RW_XA))
    xa = shifted(RW_XA, RW_XG)
    for d in range(2):
        w_log = -_softplus(-(w0_ref[d:d + 1, :] + _mm_x3(txw, w2_ref[d]))) - 0.5
        lw_o[d] = -jnp.exp(w_log)
        a = _sigmoid(a0_ref[d:d + 1, :] + _mm_x3(xa, a2_ref[d]))
        kd_o[d] = k * (1.0 + (a - 1.0) * ka_ref[...])
        b_o[d] = kap * a
    gate_o[...] = _mm_x3(_sigmoid(shifted(RW_XG, RW_COLS)), g2_ref[...])


def _rwkv_prep(p_rw, mu, w0, w2, a0, a2, g2, k_k, k_a, r_k, *, tm, n_lat_tiles, tps_lat, tps_ctx):
    n = p_rw.shape[0]
    full = lambda a: pl.BlockSpec(a.shape, lambda i: (0,) * a.ndim)
    row = pl.BlockSpec((tm, RWKV_DIM), lambda i: (i, 0))
    row2 = pl.BlockSpec((2, tm, RWKV_DIM), lambda i: (0, i, 0))
    one = jax.ShapeDtypeStruct((n, RWKV_DIM), F32)
    two = jax.ShapeDtypeStruct((2, n, RWKV_DIM), F32)
    params = [mu, w0, w2, a0, a2, g2, k_k, k_a, r_k]
    return pl.pallas_call(
        functools.partial(_rwkv_prep_kernel, n_lat_tiles=n_lat_tiles, tps_lat=tps_lat, tps_ctx=tps_ctx),
        out_shape=[one, one, one, two, two, two, one, one],
        grid=(n // tm,),
        in_specs=_halo_specs(tm, RW_COLS, n) + [full(a) for a in params],
        out_specs=[row, row, row, row2, row2, row2, row, row],
        compiler_params=_cparams(("parallel",)),
        name="rwkv_prep",
    )(p_rw, p_rw, p_rw, *params)


def _order_masks(n, rev):
    ri = lax.broadcasted_iota(jnp.int32, (n, n), 0)
    ci = lax.broadcasted_iota(jnp.int32, (n, n), 1)
    if rev:
        return ci >= ri, ci > ri, ri == ci
    return ci <= ri, ci < ri, ri == ci


def _rwkv_scan_kernel(r_ref, v_ref, kap_ref, lw_ref, b_ref, kd_ref, y_ref, s_ref, *, rev):
    @pl.when(pl.program_id(1) == 0)
    def _():
        s_ref[...] = jnp.zeros_like(s_ref)

    incl, strict, eye = _order_masks(CHUNK, rev)
    lw = lw_ref[...]
    gam = _mm_mask(incl.astype(BF16), lw)
    e_tot = jnp.exp(jnp.sum(lw, axis=0, keepdims=True))
    e_neg = jnp.exp(-gam)
    rt = r_ref[...] * jnp.exp(gam)
    kt = kap_ref[...] * jnp.exp(gam - lw)
    bt = b_ref[...] * e_neg
    kdt = kd_ref[...] * e_neg
    v_all = v_ref[...]
    eye_f = eye.astype(F32)
    for h in range(RWKV_HEADS):
        sl = slice(h * RWKV_HEAD, (h + 1) * RWKV_HEAD)
        k_h, r_h, b_h, kd_h, v_h = kt[:, sl], rt[:, sl], bt[:, sl], kdt[:, sl], v_all[:, sl]
        kr = jnp.concatenate([k_h, r_h], axis=0)
        g_b = _mm_nt(kr, b_h)
        g_k = _mm_nt(kr, kd_h)
        n_ab = jnp.where(strict, g_b[:CHUNK], 0.0)
        a_kv = jnp.where(strict, g_k[:CHUNK], 0.0)
        m_b = jnp.where(incl, g_b[CHUNK:], 0.0)
        m_kd = jnp.where(incl, g_k[CHUNK:], 0.0)
        t_inv = eye_f - n_ab
        pw = _mm(n_ab, n_ab)
        for it in range(5):
            t_inv = t_inv + _mm(t_inv, pw)
            if it < 4:
                pw = _mm(pw, pw)
        w_h = _mm(t_inv, k_h)
        u_v = _mm(t_inv, _mm(a_kv, v_h))
        s_h = s_ref[h]
        u = _mm_nt(w_h, s_h) + u_v
        y_ref[:, sl] = _mm_nt(r_h, s_h) + _mm(m_kd, v_h) - _mm(m_b, u)
        s_ref[h] = (s_h + _mm_tn(v_h, kd_h) - _mm_tn(u, b_h)) * e_tot[:, sl]


def _scan_block_map(rev, n_lat_blocks, nb_lat, nb_ctx):
    def f(b, s):
        c_ctx = (nb_ctx - 1 - s) if rev else s
        c_lat = (nb_lat - 1 - (s - nb_ctx)) if rev else (s - nb_ctx)
        return jnp.where(s < nb_ctx, n_lat_blocks + b * nb_ctx + c_ctx, b * nb_lat + c_lat)
    return f


def _rwkv_scan(r, v, kap, lw, bb, kd, *, rev, n_batch, nc_lat, nc_ctx):
    n = r.shape[0]
    blk = _scan_block_map(rev, n_batch * nc_lat, nc_lat, nc_ctx)
    d = 1 if rev else 0
    row = pl.BlockSpec((CHUNK, RWKV_DIM), lambda b, s: (blk(b, s), 0))
    rowd = pl.BlockSpec((None, CHUNK, RWKV_DIM), lambda b, s: (d, blk(b, s), 0))
    return pl.pallas_call(
        functools.partial(_rwkv_scan_kernel, rev=rev),
        out_shape=jax.ShapeDtypeStruct((n, RWKV_DIM), F32),
        grid=(n_batch, nc_ctx + nc_lat),
        in_specs=[row, row, row, rowd, rowd, rowd],
        out_specs=row,
        scratch_shapes=[pltpu.VMEM((RWKV_HEADS, RWKV_HEAD, RWKV_HEAD), F32)],
        compiler_params=_cparams(("parallel", "arbitrary")),
        name="rwkv_scan_rev" if rev else "rwkv_scan_fwd",
    )(r, v, kap, lw, bb, kd)


def _gla_kernel(u_ref, up_ref, un_ref, cw_ref, aw_ref, ab_ref, o_ref, st_ref, qkv_s, g_s, *, rev, nb_lat, nb_ctx):
    s = pl.program_id(1)

    @pl.when(s == 0)
    def _():
        st_ref[...] = jnp.zeros_like(st_ref)

    c_ctx = (nb_ctx - 1 - s) if rev else s
    c_lat = (nb_lat - 1 - (s - nb_ctx)) if rev else (s - nb_ctx)
    is_ctx = s < nb_ctx
    c = jnp.where(is_ctx, c_ctx, c_lat)
    first = c == 0
    last = c == jnp.where(is_ctx, nb_ctx - 1, nb_lat - 1)
    for c0 in range(0, GLA_QKV, 512):
        p, prev, nxt = _neighbours(u_ref, up_ref, un_ref, c0, c0 + 512, first, last)
        conv = cw_ref[0:1, c0:c0 + 512] * prev + cw_ref[1:2, c0:c0 + 512] * p + cw_ref[2:3, c0:c0 + 512] * nxt
        act = _silu(conv)
        if c0 == 0:
            act = act * (GLA_DK ** -0.5)
        qkv_s[:, c0:c0 + 512] = act
    z = _mm_x3(u_ref[:, GLA_QKV:GLA_COLS], aw_ref[...]) + ab_ref[...]
    g_s[...] = -_softplus(-z) * (1.0 / GLA_TAU)

    incl, _, _ = _order_masks(CHUNK, rev)
    incl_b = incl.astype(BF16)
    n_chunks = u_ref.shape[0] // CHUNK
    order = range(n_chunks - 1, -1, -1) if rev else range(n_chunks)
    for ch in order:
        rows = slice(ch * CHUNK, (ch + 1) * CHUNK)
        g = g_s[rows, :]
        gc = _mm_mask(incl_b, g)
        gl = jnp.sum(g, axis=0, keepdims=True)
        q = qkv_s[rows, 0:GLA_KDIM]
        k = qkv_s[rows, GLA_KDIM:2 * GLA_KDIM]
        qg = q * jnp.exp(gc)
        kg = k * jnp.exp(-gc)
        ke = k * jnp.exp(gl - gc)
        e_l = jnp.exp(gl)
        for h in range(GLA_HEADS):
            ks = slice(h * GLA_DK, (h + 1) * GLA_DK)
            vs = slice(2 * GLA_KDIM + h * GLA_DV, 2 * GLA_KDIM + (h + 1) * GLA_DV)
            v_h = qkv_s[rows, vs]
            att = jnp.where(incl, _mm_nt(qg[:, ks], kg[:, ks]), 0.0)
            st = st_ref[h]
            o_ref[rows, h * GLA_DV:(h + 1) * GLA_DV] = _mm(att, v_h) + _mm_nt(qg[:, ks], st)
            st_ref[h] = st * e_l[:, ks] + _mm_tn(v_h, ke[:, ks])


def _gla_scan(u, conv_w, alpha_w, alpha_b, *, rev, tm, n_batch, nb_lat, nb_ctx):
    n = u.shape[0]
    blk = _scan_block_map(rev, n_batch * nb_lat, nb_lat, nb_ctx)
    r8 = tm // 8
    last8 = n // 8 - 1
    full = lambda a: pl.BlockSpec(a.shape, lambda b, s: (0,) * a.ndim)
    return pl.pallas_call(
        functools.partial(_gla_kernel, rev=rev, nb_lat=nb_lat, nb_ctx=nb_ctx),
        out_shape=jax.ShapeDtypeStruct((n, GLA_VDIM), F32),
        grid=(n_batch, nb_ctx + nb_lat),
        in_specs=[pl.BlockSpec((tm, GLA_COLS), lambda b, s: (blk(b, s), 0)),
                  pl.BlockSpec((8, GLA_COLS), lambda b, s: (jnp.maximum(blk(b, s) * r8 - 1, 0), 0)),
                  pl.BlockSpec((8, GLA_COLS), lambda b, s: (jnp.minimum((blk(b, s) + 1) * r8, last8), 0)),
                  full(conv_w), full(alpha_w), full(alpha_b)],
        out_specs=pl.BlockSpec((tm, GLA_VDIM), lambda b, s: (blk(b, s), 0)),
        scratch_shapes=[pltpu.VMEM((GLA_HEADS, GLA_DV, GLA_DK), F32),
                        pltpu.VMEM((tm, GLA_QKV), F32),
                        pltpu.VMEM((tm, GLA_KDIM), F32)],
        compiler_params=_cparams(("parallel", "arbitrary")),
        name="gla_scan_rev" if rev else "gla_scan_fwd",
    )(u, u, u, conv_w, alpha_w, alpha_b)


def _pair_tables():
    ri = lax.broadcasted_iota(jnp.int32, (LANES, LANES), 0)
    ci = lax.broadcasted_iota(jnp.int32, (LANES, LANES), 1)
    grp = ci // 6
    pr = ci % 6
    lo = jnp.where(pr < 3, 0, jnp.where(pr < 5, 1, 2))
    hi = jnp.where(pr < 3, pr + 1, jnp.where(pr < 5, pr - 1, 3))
    member = ((ri == grp * EXPERTS_PER_GROUP + lo) | (ri == grp * EXPERTS_PER_GROUP + hi)) & (ci < N_PAIRS)
    return member


def _route(h, rw_ref, rb_ref):
    lane = lax.broadcasted_iota(jnp.int32, (h.shape[0], LANES), 1)
    scores = _sigmoid(_mm_x3(h, rw_ref[...]))
    sel = jnp.where(lane < N_EXPERTS, scores + rb_ref[...], 0.0)
    member = _pair_tables()
    s3 = _split3(sel)
    mb = member.astype(BF16)
    d = functools.partial(jnp.dot, preferred_element_type=F32)
    ps = d(s3[0], mb) + (d(s3[1], mb) + d(s3[2], mb))
    ps = jnp.where(lane < N_PAIRS, ps, -jnp.inf)
    best = jnp.max(ps, axis=-1, keepdims=True)
    bucket = jnp.min(jnp.where(ps == best, lane, LANES), axis=-1, keepdims=True)
    onehot = (lane == bucket).astype(BF16)
    chosen = lax.dot_general(onehot, mb, (((1,), (1,)), ((), ())), preferred_element_type=F32) > 0.5
    picked = jnp.where(chosen, scores, 0.0)
    gates = picked / jnp.sum(picked, axis=-1, keepdims=True)
    e_lo = jnp.min(jnp.where(chosen, lane, LANES), axis=-1, keepdims=True)
    e_hi = jnp.max(jnp.where(chosen, lane, -1), axis=-1, keepdims=True)
    w_lo = jnp.sum(jnp.where(lane == e_lo, gates, 0.0), axis=-1, keepdims=True)
    w_hi = jnp.sum(jnp.where(lane == e_hi, gates, 0.0), axis=-1, keepdims=True)
    out = jnp.where(lane == 0, e_lo.astype(F32), 0.0)
    out = jnp.where(lane == 1, e_hi.astype(F32), out)
    out = jnp.where(lane == 2, w_lo, out)
    return jnp.where(lane == 3, w_hi, out)


def _merge_kernel(x_ref, y0_ref, y1_ref, bonus_ref, gate_ref, o0_ref, o1_ref, pg_ref, mods_ref,
                  lng_ref, lnb_ref, gng_ref, wa_ref, wb_ref, wo_ref, nfg_ref, rw_ref, rb_ref,
                  xs_o, h_o, route_o):
    y = y0_ref[...] + y1_ref[...]
    mean = _head_sum64(y) * (1.0 / RWKV_HEAD)
    yc = y - mean
    var = _head_sum64(yc * yc) * (1.0 / RWKV_HEAD)
    ya = (yc * lax.rsqrt(var + RWKV_GN_EPS) * lng_ref[...] + lnb_ref[...] + bonus_ref[...]) * gate_ref[...]
    o = o0_ref[...] + o1_ref[...]
    parts = []
    for h in range(GLA_HEADS):
        o_h = o[:, h * GLA_DV:(h + 1) * GLA_DV]
        parts.append(o_h * lax.rsqrt(jnp.mean(o_h * o_h, axis=-1, keepdims=True) + NORM_EPS) * gng_ref[...])
    yb = jnp.concatenate(parts, axis=1) * _silu(pg_ref[:, 0:GLA_VDIM].astype(F32))
    ga = _sigmoid(pg_ref[:, GLA_VDIM:GLA_VDIM + D_MODEL].astype(F32))
    gb = _sigmoid(pg_ref[:, GLA_VDIM + D_MODEL:GATE_COLS].astype(F32))
    z = ga * _mm(ya, wa_ref[...]) + gb * _mm(yb, wb_ref[...])
    xs = x_ref[...] + mods_ref[0, 2:3, :] * _mm(z, wo_ref[...])
    xs_o[...] = xs
    h = _rms(xs, nfg_ref[...]) * (1.0 + mods_ref[0, 4:5, :]) + mods_ref[0, 3:4, :]
    h_o[...] = h
    route_o[...] = _route(h, rw_ref, rb_ref)


def _merge(x, y0, y1, bonus, gate, o0, o1, pg, mods, ln_g, ln_b, gn_g, wa, wb, wo, nf_g, rw, rb,
           *, tm, n_rows, n_lat_tiles, tiles_per_batch, n_batch):
    d = D_MODEL
    mrow = _mod_row_map(n_lat_tiles, tiles_per_batch, n_batch)
    row = lambda c: pl.BlockSpec((tm, c), lambda i: (i, 0))
    full = lambda a: pl.BlockSpec(a.shape, lambda i: (0,) * a.ndim)
    consts = [ln_g.reshape(1, -1), ln_b.reshape(1, -1), gn_g.reshape(1, -1), wa, wb, wo, nf_g.reshape(1, -1), rw, rb]
    return pl.pallas_call(
        _merge_kernel,
        out_shape=[jax.ShapeDtypeStruct((n_rows, d), F32), jax.ShapeDtypeStruct((n_rows, d), F32),
                   jax.ShapeDtypeStruct((n_rows, LANES), F32)],
        grid=(n_rows // tm,),
        in_specs=[row(d), row(RWKV_DIM), row(RWKV_DIM), row(RWKV_DIM), row(RWKV_DIM), row(GLA_VDIM), row(GLA_VDIM),
                  row(GATE_COLS), pl.BlockSpec((1, 6, d), mrow)] + [full(a) for a in consts],
        out_specs=[row(d), row(d), row(LANES)],
        compiler_params=_cparams(("parallel",)),
        name="merge",
    )(x, y0, y1, bonus, gate, o0, o1, pg, mods, *consts)


def _moe_kernel(te_ref, nu_ref, src_ref, dst_ref, w_ref, h_hbm, wg_ref, wu_ref, wd_ref, o_hbm,
                xbuf, obuf, sem_in, sem_out):
    del te_ref
    tm = xbuf.shape[0]

    def row_in(r, src_row):
        return pltpu.make_async_copy(h_hbm.at[pl.ds(src_row, 1)], xbuf.at[pl.ds(r, 1)], sem_in)

    def row_out(r, dst_row):
        return pltpu.make_async_copy(obuf.at[pl.ds(r, 1)], o_hbm.at[pl.ds(dst_row, 1)], sem_out)

    @pl.when(pl.program_id(0) < nu_ref[0])
    def _():
        def start_in(r, carry):
            row_in(r, src_ref[0, 0, r]).start()
            return carry

        def wait_in(r, carry):
            row_in(r, 0).wait()
            return carry

        lax.fori_loop(0, tm, start_in, 0)
        lax.fori_loop(0, tm, wait_in, 0)
        xb = xbuf[...].astype(BF16)
        g = jnp.dot(xb, wg_ref[...], preferred_element_type=F32)
        u = jnp.dot(xb, wu_ref[...], preferred_element_type=F32)
        act = (_silu(g) * u).astype(BF16)
        obuf[...] = jnp.dot(act, wd_ref[...], preferred_element_type=F32) * w_ref[...][:, 0:1]

        def start_out(r, carry):
            dst_row = dst_ref[0, 0, r]

            @pl.when(dst_row >= 0)
            def _():
                row_out(r, dst_row).start()
            return carry

        def wait_out(r, carry):
            @pl.when(dst_ref[0, 0, r] >= 0)
            def _():
                row_out(r, 0).wait()
            return carry

        lax.fori_loop(0, tm, start_out, 0)
        lax.fori_loop(0, tm, wait_out, 0)


def _moe(h, route, wg, wu, wd, *, tm):
    t, d = h.shape
    e_lo = route[:, 0].astype(jnp.int32)
    e_hi = route[:, 1].astype(jnp.int32)
    ids = jnp.concatenate([e_lo, e_hi])
    wts = jnp.concatenate([route[:, 2], route[:, 3]])
    n_tiles = (2 * t) // tm + N_EXPERTS
    order = jnp.argsort(ids, stable=True).astype(jnp.int32)
    counts = jnp.sum((ids[:, None] == jnp.arange(N_EXPERTS, dtype=jnp.int32)[None, :]).astype(jnp.int32), axis=0)
    starts = jnp.cumsum(counts) - counts
    pcounts = ((counts + tm - 1) // tm) * tm
    pend = jnp.cumsum(pcounts)
    poff = pend - pcounts
    n_used = (pend[-1] // tm).astype(jnp.int32)
    tile_start = jnp.arange(n_tiles, dtype=jnp.int32) * tm
    te = jnp.sum((tile_start[:, None] >= pend[None, :]).astype(jnp.int32), axis=1)
    te_last = jnp.take(te, jnp.maximum(n_used - 1, 0))
    te = jnp.where(jnp.arange(n_tiles) < n_used, te, te_last).astype(jnp.int32)
    pos = jnp.arange(n_tiles * tm, dtype=jnp.int32)
    e_p = jnp.take(te, pos // tm)
    rank = pos - jnp.take(poff, e_p)
    valid = (rank < jnp.take(counts, e_p)) & (pos // tm < n_used)
    a = jnp.take(order, jnp.clip(jnp.take(starts, e_p) + rank, 0, 2 * t - 1))
    tok = a % t
    src = jnp.where(valid, tok, 0).astype(jnp.int32).reshape(n_tiles, 1, tm)
    dst = jnp.where(valid, a, -1).astype(jnp.int32).reshape(n_tiles, 1, tm)
    wrow = jnp.broadcast_to(jnp.where(valid, jnp.take(wts, a), 0.0)[:, None], (n_tiles * tm, LANES))

    idx_spec = pl.BlockSpec((1, 1, tm), lambda i, te_r, nu_r: (i, 0, 0), memory_space=pltpu.SMEM)
    out = pl.pallas_call(
        _moe_kernel,
        out_shape=jax.ShapeDtypeStruct((2 * t, d), F32),
        grid_spec=pltpu.PrefetchScalarGridSpec(
            num_scalar_prefetch=2,
            grid=(n_tiles,),
            in_specs=[idx_spec, idx_spec,
                      pl.BlockSpec((tm, LANES), lambda i, te_r, nu_r: (i, 0)),
                      pl.BlockSpec(memory_space=pl.ANY),
                      pl.BlockSpec((None, d, D_EXPERT), lambda i, te_r, nu_r: (te_r[i], 0, 0)),
                      pl.BlockSpec((None, d, D_EXPERT), lambda i, te_r, nu_r: (te_r[i], 0, 0)),
                      pl.BlockSpec((None, D_EXPERT, d), lambda i, te_r, nu_r: (te_r[i], 0, 0))],
            out_specs=pl.BlockSpec(memory_space=pl.ANY),
            scratch_shapes=[pltpu.VMEM((tm, d), F32), pltpu.VMEM((tm, d), F32),
                            pltpu.SemaphoreType.DMA, pltpu.SemaphoreType.DMA]),
        compiler_params=_cparams(("arbitrary",)),
        name="moe_ffn",
    )(te, n_used.reshape(1), src, dst, wrow, h, wg, wu, wd)
    return out.reshape(2, t, d)


def _pad_cols(a, n):
    return jnp.pad(a, [(0, 0)] * (a.ndim - 1) + [(0, n - a.shape[-1])])


def _pad_rows(a, n):
    return jnp.pad(a, [(0, 0)] * (a.ndim - 2) + [(0, n - a.shape[-2]), (0, 0)])


def _split_in_proj(w_in, mu):
    s = (RWKV_DIM, 2 * RWKV_DIM, 3 * RWKV_DIM, 3 * RWKV_DIM + DECAY_LORA, 3 * RWKV_DIM + DECAY_LORA + ICLR_LORA)

    def rw_layout(a):
        return jnp.concatenate([a[..., :s[2]], _pad_cols(a[..., s[2]:s[3]], LORA_PAD),
                                _pad_cols(a[..., s[3]:s[4]], LORA_PAD), a[..., s[4]:RWKV_IN]], axis=-1)

    w_rw = rw_layout(w_in)
    w_gla = _pad_cols(w_in[:, RWKV_IN:GLA_GATE_OFF], GLA_COLS)
    w_gate = w_in[:, GLA_GATE_OFF:]
    return w_rw.astype(BF16), w_gla.astype(BF16), w_gate.astype(BF16), rw_layout(mu)


def kernel(x, c, ctx, c_ctx, w_ada, b_ada, norm_mix_g, norm_ffn_g, w_in, rwkv_mu, rwkv_w0, rwkv_w2, rwkv_a0,
           rwkv_a2, rwkv_g2, rwkv_k_k, rwkv_k_a, rwkv_r_k, rwkv_ln_g, rwkv_ln_b, gla_conv, gla_alpha_w2,
           gla_alpha_b, gla_norm_g, w_branch_a, w_branch_b, w_out, router_w, router_b, exp_w_gate, exp_w_up,
           exp_w_down, final_norm_g):
    n_batch, seq, d = x.shape
    ctx_len = ctx.shape[1]
    assert d == D_MODEL and seq % GRID_W == 0 and n_batch < MOD_ROWS
    grid_rows = seq // GRID_W
    tm = min(256, ctx_len)
    assert seq % tm == 0 and ctx_len % tm == 0 and tm % CHUNK == 0
    t_lat, t_ctx = n_batch * seq, n_batch * ctx_len
    t_all = t_lat + t_ctx
    tiles = dict(tm=tm, n_lat_tiles=t_lat // tm)
    seq_tiles = dict(tps_lat=seq // tm, tps_ctx=ctx_len // tm)
    mod_tiles = dict(tiles_per_batch=seq // tm, n_batch=n_batch)

    xs = jnp.concatenate([x.reshape(t_lat, d), ctx.reshape(t_ctx, d)], axis=0)
    c_all = jnp.zeros((MOD_ROWS, d), F32).at[:n_batch].set(c).at[n_batch].set(c_ctx)
    mods = _adaln(c_all, w_ada, b_ada).reshape(DEPTH, MOD_ROWS, 6, d)
    rw_pad = _pad_cols(router_w, LANES)
    rb_pad = _pad_cols(router_b.reshape(1, -1), LANES)

    moe_out = None
    for l in range(DEPTH):
        last = l == DEPTH - 1
        if moe_out is None:
            (h,) = _prenorm(xs, None, None, mods[l], norm_mix_g[l], n_rows=t_all, **tiles, **mod_tiles)
        else:
            xs, h = _prenorm(xs, moe_out, mods[l - 1], mods[l], norm_mix_g[l], n_rows=t_all, **tiles, **mod_tiles)
        w_rw, w_gla, w_gate, mu = _split_in_proj(w_in[l], rwkv_mu[l])
        p_rw = _matmul(h, w_rw, F32, tn=RW_COLS // 7)
        p_gla = _matmul(h, w_gla, F32, tn=GLA_COLS)
        p_gate = _matmul(h, w_gate, BF16, tn=GATE_COLS // 5)

        r, v, kap, lw, bb, kd, bonus, gate = _rwkv_prep(
            p_rw, mu, rwkv_w0[l], _pad_rows(rwkv_w2[l], LORA_PAD), rwkv_a0[l], _pad_rows(rwkv_a2[l], LORA_PAD),
            rwkv_g2[l], rwkv_k_k[l].reshape(1, -1), rwkv_k_a[l].reshape(1, -1), rwkv_r_k[l].reshape(1, -1),
            **tiles, **seq_tiles)
        scan = dict(n_batch=n_batch, nc_lat=seq // CHUNK, nc_ctx=ctx_len // CHUNK)
        y0 = _rwkv_scan(r, v, kap, lw, bb, kd, rev=False, **scan)
        y1 = _rwkv_scan(r, v, kap, lw, bb, kd, rev=True, **scan)

        u_lat = p_gla[:t_lat].reshape(n_batch, grid_rows, GRID_W, GLA_COLS).transpose(0, 2, 1, 3)
        u = jnp.concatenate([u_lat.reshape(t_lat, GLA_COLS), p_gla[t_lat:]], axis=0)
        conv_w = _pad_cols(gla_conv[l], GLA_QKV)
        outs = []
        for dr in range(2):
            o = _gla_scan(u, conv_w, _pad_rows(gla_alpha_w2[l, dr], LANES), gla_alpha_b[l, dr].reshape(1, -1),
                          rev=dr == 1, tm=tm, n_batch=n_batch, nb_lat=seq // tm, nb_ctx=ctx_len // tm)
            o_lat = o[:t_lat].reshape(n_batch, GRID_W, grid_rows, GLA_VDIM).transpose(0, 2, 1, 3)
            outs.append(jnp.concatenate([o_lat.reshape(t_lat, GLA_VDIM), o[t_lat:]], axis=0))

        n_rows = t_lat if last else t_all
        xs, hf, route = _merge(
            xs, y0, y1, bonus, gate, outs[0], outs[1], p_gate, mods[l], rwkv_ln_g[l], rwkv_ln_b[l], gla_norm_g[l],
            w_branch_a[l].astype(BF16), w_branch_b[l].astype(BF16), w_out[l].astype(BF16), norm_ffn_g[l],
            rw_pad, rb_pad, n_rows=n_rows, **tiles, **mod_tiles)
        moe_out = _moe(hf, route, exp_w_gate[l].astype(BF16), exp_w_up[l].astype(BF16), exp_w_down[l].astype(BF16),
                       tm=tm)

    (out,) = _prenorm(xs, moe_out, mods[DEPTH - 1], mods[DEPTH - 1], final_norm_g, n_rows=t_lat, final=True,
                      **tiles, **mod_tiles)
    return out.reshape(n_batch, seq, d)
```

```python
import functools

import jax
import jax.numpy as jnp
from jax import lax
from jax.experimental import pallas as pl
from jax.experimental.pallas import tpu as pltpu

F32 = jnp.float32
BF16 = jnp.bfloat16

D_MODEL = 2048
DEPTH = 2
GRID_W = 64
NORM_EPS = 1e-6
RWKV_DIM = 1024
RWKV_HEAD = 64
RWKV_HEADS = 16
DECAY_LORA = 96
ICLR_LORA = 96
GATE_LORA = 256
RWKV_GN_EPS = 64e-5
GLA_HEADS = 4
GLA_KDIM = 512
GLA_VDIM = 1024
GLA_DK = 128
GLA_DV = 256
GLA_GATE_LORA = 16
GLA_TAU = 16.0
GLA_QKV = 2048
N_EXPERTS = 16
EXPERTS_PER_GROUP = 4
N_PAIRS = 24
D_EXPERT = 1408
RWKV_IN = 3520
GLA_SEQ_IN = 2064
GLA_GATE_OFF = RWKV_IN + GLA_SEQ_IN
BR_GATE_OFF = GLA_GATE_OFF + GLA_VDIM

LANES = 128
CHUNK = 64
MOD_ROWS = 16
LORA_PAD = 128
RW_XW = 3 * RWKV_DIM
RW_XA = RW_XW + LORA_PAD
RW_XG = RW_XA + LORA_PAD
RW_COLS = RW_XG + GATE_LORA
GLA_COLS = GLA_QKV + LANES
GATE_COLS = GLA_VDIM + 2 * D_MODEL
VMEM_LIMIT = 56 * 1024 * 1024


def _cparams(sem, vmem=VMEM_LIMIT):
    return pltpu.CompilerParams(dimension_semantics=sem, vmem_limit_bytes=vmem)


def _pick_tile(n, cands):
    for c in cands:
        if n % c == 0:
            return c
    raise ValueError(f"no tile for {n}")


def _mm(a, b):
    return jnp.dot(a.astype(BF16), b.astype(BF16), preferred_element_type=F32)


def _mm_nt(a, b):
    return lax.dot_general(a.astype(BF16), b.astype(BF16), (((1,), (1,)), ((), ())), preferred_element_type=F32)


def _mm_tn(a, b):
    return lax.dot_general(a.astype(BF16), b.astype(BF16), (((0,), (0,)), ((), ())), preferred_element_type=F32)


def _split2(a):
    hi = a.astype(BF16)
    lo = (a - hi.astype(F32)).astype(BF16)
    return hi, lo


def _split3(a):
    hi = a.astype(BF16)
    r1 = a - hi.astype(F32)
    mid = r1.astype(BF16)
    lo = (r1 - mid.astype(F32)).astype(BF16)
    return hi, mid, lo


def _mm_x3(a, b):
    ah, al = _split2(a)
    bh, bl = _split2(b)
    d = functools.partial(jnp.dot, preferred_element_type=F32)
    return d(ah, bh) + (d(al, bh) + d(ah, bl))


def _mm_mask(m01, x):
    h, m, l = _split3(x)
    d = functools.partial(jnp.dot, preferred_element_type=F32)
    return d(m01, h) + (d(m01, m) + d(m01, l))


def _sigmoid(x):
    return 1.0 / (1.0 + jnp.exp(-x))


def _silu(x):
    return x * _sigmoid(x)


def _softplus(z):
    return jnp.maximum(z, 0.0) + jnp.log1p(jnp.exp(-jnp.abs(z)))


def _head_sum64(x):
    ri = lax.broadcasted_iota(jnp.int32, (256, 256), 0) // RWKV_HEAD
    ci = lax.broadcasted_iota(jnp.int32, (256, 256), 1) // RWKV_HEAD
    ones = (ri == ci).astype(BF16)
    hi, lo = _split2(x)
    d = functools.partial(jnp.dot, preferred_element_type=F32)
    outs = [d(hi[:, g * 256:(g + 1) * 256], ones) + d(lo[:, g * 256:(g + 1) * 256], ones)
            for g in range(x.shape[1] // 256)]
    return jnp.concatenate(outs, axis=1)


def _adaln_kernel(c_ref, w_ref, b_ref, o_ref):
    c = c_ref[...]
    o_ref[...] = _mm_x3(_silu(c), w_ref[...]) + b_ref[...]


def _adaln(c_all, w_ada, b_ada):
    nl, d, n = w_ada.shape
    tn = 1024
    return pl.pallas_call(
        _adaln_kernel,
        out_shape=jax.ShapeDtypeStruct((nl, MOD_ROWS, n), F32),
        grid=(nl, n // tn),
        in_specs=[pl.BlockSpec((MOD_ROWS, d), lambda l, j: (0, 0)),
                  pl.BlockSpec((None, d, tn), lambda l, j: (l, 0, j)),
                  pl.BlockSpec((None, 1, tn), lambda l, j: (l, 0, j))],
        out_specs=pl.BlockSpec((None, MOD_ROWS, tn), lambda l, j: (l, 0, j)),
        compiler_params=_cparams(("parallel", "parallel")),
        name="adaln",
    )(c_all, w_ada, b_ada.reshape(nl, 1, n))


def _rms(x, g):
    return x * lax.rsqrt(jnp.mean(x * x, axis=-1, keepdims=True) + NORM_EPS) * g


def _prenorm_kernel(*refs, has_moe, final):
    if has_moe:
        x_ref, o0_ref, o1_ref, mprev_ref, mcur_ref, g_ref = refs[:6]
        outs = refs[6:]
        x = x_ref[...] + mprev_ref[0, 5:6, :] * (o0_ref[...] + o1_ref[...])
    else:
        x_ref, mcur_ref, g_ref = refs[:3]
        outs = refs[3:]
        x = x_ref[...]
    y = _rms(x, g_ref[...])
    if final:
        outs[0][...] = y
        return
    h = y * (1.0 + mcur_ref[0, 1:2, :]) + mcur_ref[0, 0:1, :]
    if has_moe:
        outs[0][...] = x
        outs[1][...] = h.astype(outs[1].dtype)
    else:
        outs[0][...] = h.astype(outs[0].dtype)


def _mod_row_map(n_lat_tiles, tiles_per_batch, n_batch):
    return lambda i: (jnp.where(i < n_lat_tiles, i // tiles_per_batch, n_batch), 0, 0)


def _prenorm(x, moe_out, mods_prev, mods_cur, g, *, tm, n_lat_tiles, tiles_per_batch, n_batch, n_rows, final=False):
    d = x.shape[1]
    has_moe = moe_out is not None
    mrow = _mod_row_map(n_lat_tiles, tiles_per_batch, n_batch)
    row = pl.BlockSpec((tm, d), lambda i: (i, 0))
    mspec = pl.BlockSpec((1, 6, d), mrow)
    gspec = pl.BlockSpec((1, d), lambda i: (0, 0))
    ins, specs = [x], [row]
    if has_moe:
        slot1 = n_rows // tm
        ins += [moe_out, moe_out, mods_prev]
        specs += [row, pl.BlockSpec((tm, d), lambda i: (slot1 + i, 0)), mspec]
    ins += [mods_cur, g.reshape(1, d)]
    specs += [mspec, gspec]
    if final:
        out_shape = [jax.ShapeDtypeStruct((n_rows, d), F32)]
    elif has_moe:
        out_shape = [jax.ShapeDtypeStruct((n_rows, d), F32), jax.ShapeDtypeStruct((n_rows, d), BF16)]
    else:
        out_shape = [jax.ShapeDtypeStruct((n_rows, d), BF16)]
    return pl.pallas_call(
        functools.partial(_prenorm_kernel, has_moe=has_moe, final=final),
        out_shape=out_shape,
        grid=(n_rows // tm,),
        in_specs=specs,
        out_specs=[row] * len(out_shape),
        compiler_params=_cparams(("parallel",)),
        name="prenorm_final" if final else "prenorm",
    )(*ins)


def _matmul_kernel(a_ref, w_ref, o_ref):
    o_ref[...] = jnp.dot(a_ref[...], w_ref[...], preferred_element_type=F32).astype(o_ref.dtype)


def _matmul(a, w, out_dtype, *, tn):
    m, k = a.shape
    n = w.shape[1]
    tm = _pick_tile(m, (1024, 512, 256, 128, 64))
    return pl.pallas_call(
        _matmul_kernel,
        out_shape=jax.ShapeDtypeStruct((m, n), out_dtype),
        grid=(m // tm, n // tn),
        in_specs=[pl.BlockSpec((tm, k), lambda i, j: (i, 0)),
                  pl.BlockSpec((k, tn), lambda i, j: (0, j))],
        out_specs=pl.BlockSpec((tm, tn), lambda i, j: (i, j)),
        compiler_params=_cparams(("parallel", "parallel")),
        name="in_proj",
    )(a, w)


def _seq_flags(i, n_lat_tiles, tps_lat, tps_ctx):
    is_lat = i < n_lat_tiles
    j = jnp.where(is_lat, i % tps_lat, (i - n_lat_tiles) % tps_ctx)
    return j == 0, j == jnp.where(is_lat, tps_lat - 1, tps_ctx - 1)


def _neighbours(ref, prev_ref, next_ref, c0, c1, first, last):
    tm = ref.shape[0]
    p = ref[:, c0:c1]
    rows = lax.broadcasted_iota(jnp.int32, (tm, 1), 0)
    pr = jnp.where(first, 0.0, prev_ref[7:8, c0:c1])
    nx = jnp.where(last, 0.0, next_ref[0:1, c0:c1])
    prev = jnp.where(rows == 0, pr, pltpu.roll(p, 1, 0))
    nxt = jnp.where(rows == tm - 1, nx, pltpu.roll(p, tm - 1, 0))
    return p, prev, nxt


def _halo_specs(tm, cols, n_rows):
    r8 = tm // 8
    last8 = n_rows // 8 - 1
    return [pl.BlockSpec((tm, cols), lambda i: (i, 0)),
            pl.BlockSpec((8, cols), lambda i: (jnp.maximum(i * r8 - 1, 0), 0)),
            pl.BlockSpec((8, cols), lambda i: (jnp.minimum((i + 1) * r8, last8), 0))]


def _rwkv_prep_kernel(p_ref, pp_ref, pn_ref, mu_ref, w0_ref, w2_ref, a0_ref, a2_ref, g2_ref, kk_ref, ka_ref, rk_ref,
                      r_o, v_o, kap_o, lw_o, b_o, kd_o, bonus_o, gate_o, *, n_lat_tiles, tps_lat, tps_ctx):
    first, last = _seq_flags(pl.program_id(0), n_lat_tiles, tps_lat, tps_ctx)

    def shifted(c0, c1):
        p, prev, nxt = _neighbours(p_ref, pp_ref, pn_ref, c0, c1, first, last)
        return p + mu_ref[0:1, c0:c1] * (prev - p) + mu_ref[1:2, c0:c1] * (nxt - p)

    r = shifted(0, RWKV_DIM)
    k = shifted(RWKV_DIM, 2 * RWKV_DIM)
    v = shifted(2 * RWKV_DIM, 3 * RWKV_DIM)
    kk = k * kk_ref[...]
    kap = kk / jnp.maximum(jnp.sqrt(_head_sum64(kk * kk)), 1e-12)
    r_o[...] = r
    v_o[...] = v
    kap_o[...] = kap
    bonus_o[...] = _head_sum64(r * k * rk_ref[...]) * v
    txw = jnp.tanh(shifted(RW_XW, RW_XA))
    xa = shifted(RW_XA, RW_XG)
    for d in range(2):
        w_log = -_softplus(-(w0_ref[d:d + 1, :] + _mm_x3(txw, w2_ref[d]))) - 0.5
        lw_o[d] = -jnp.exp(w_log)
        a = _sigmoid(a0_ref[d:d + 1, :] + _mm_x3(xa, a2_ref[d]))
        kd_o[d] = k * (1.0 + (a - 1.0) * ka_ref[...])
        b_o[d] = kap * a
    gate_o[...] = _mm_x3(_sigmoid(shifted(RW_XG, RW_COLS)), g2_ref[...])


def _rwkv_prep(p_rw, mu, w0, w2, a0, a2, g2, k_k, k_a, r_k, *, tm, n_lat_tiles, tps_lat, tps_ctx):
    n = p_rw.shape[0]
    full = lambda a: pl.BlockSpec(a.shape, lambda i: (0,) * a.ndim)
    row = pl.BlockSpec((tm, RWKV_DIM), lambda i: (i, 0))
    row2 = pl.BlockSpec((2, tm, RWKV_DIM), lambda i: (0, i, 0))
    one = jax.ShapeDtypeStruct((n, RWKV_DIM), F32)
    two = jax.ShapeDtypeStruct((2, n, RWKV_DIM), F32)
    params = [mu, w0, w2, a0, a2, g2, k_k, k_a, r_k]
    return pl.pallas_call(
        functools.partial(_rwkv_prep_kernel, n_lat_tiles=n_lat_tiles, tps_lat=tps_lat, tps_ctx=tps_ctx),
        out_shape=[one, one, one, two, two, two, one, one],
        grid=(n // tm,),
        in_specs=_halo_specs(tm, RW_COLS, n) + [full(a) for a in params],
        out_specs=[row, row, row, row2, row2, row2, row, row],
        compiler_params=_cparams(("parallel",)),
        name="rwkv_prep",
    )(p_rw, p_rw, p_rw, *params)


def _order_masks(n, rev):
    ri = lax.broadcasted_iota(jnp.int32, (n, n), 0)
    ci = lax.broadcasted_iota(jnp.int32, (n, n), 1)
    if rev:
        return ci >= ri, ci > ri, ri == ci
    return ci <= ri, ci < ri, ri == ci


HEADS_PER_GROUP = 4
GROUP = HEADS_PER_GROUP * RWKV_HEAD


def _head_block_mask():
    ri = lax.broadcasted_iota(jnp.int32, (GROUP, GROUP), 0) // RWKV_HEAD
    ci = lax.broadcasted_iota(jnp.int32, (GROUP, GROUP), 1) // RWKV_HEAD
    return ri == ci


def _block_diag(y_g, bdm):
    return jnp.where(bdm, jnp.concatenate([y_g] * HEADS_PER_GROUP, axis=0), jnp.zeros((), y_g.dtype))


def _head_prod(x, y, bdm, nt=False):
    xb, yb = x.astype(BF16), y.astype(BF16)
    dims = (((1,), (1,)), ((), ())) if nt else (((1,), (0,)), ((), ()))
    outs = [lax.dot_general(xb[:, g:g + GROUP], _block_diag(yb[:, g:g + GROUP], bdm), dims,
                            preferred_element_type=F32) for g in range(0, x.shape[1], GROUP)]
    return jnp.concatenate(outs, axis=1)


def _head_prod_tn(a, b, bdm):
    ab, bb = a.astype(BF16), b.astype(BF16)
    outs = []
    for g in range(0, a.shape[1], GROUP):
        p = lax.dot_general(ab[:, g:g + GROUP], bb[:, g:g + GROUP], (((0,), (0,)), ((), ())),
                            preferred_element_type=F32)
        p = jnp.where(bdm, p, 0.0)
        outs.append((p[0:64] + p[64:128]) + (p[128:192] + p[192:256]))
    return jnp.concatenate(outs, axis=1)


def _rwkv_chunk(r_ref, v_ref, kap_ref, lw_ref, b_ref, kd_ref, y_ref, s_ref, rev):
    row = lax.broadcasted_iota(jnp.int32, (CHUNK, RWKV_DIM), 0)
    col = lax.broadcasted_iota(jnp.int32, (CHUNK, RWKV_DIM), 1) % RWKV_HEAD
    incl, strict = (col >= row, col > row) if rev else (col <= row, col < row)
    eye_f = (col == row).astype(F32)
    bdm = _head_block_mask()
    lw = lw_ref[...]
    gam = _mm_mask(_order_masks(CHUNK, rev)[0].astype(BF16), lw)
    e_tot = jnp.exp(jnp.sum(lw, axis=0, keepdims=True))
    e_neg = jnp.exp(-gam)
    rt = r_ref[...] * jnp.exp(gam)
    kt = kap_ref[...] * jnp.exp(gam - lw)
    bt = b_ref[...] * e_neg
    kdt = kd_ref[...] * e_neg
    v = v_ref[...]
    kr = jnp.concatenate([kt, rt], axis=0)
    g_b = _head_prod(kr, bt, bdm, nt=True)
    g_k = _head_prod(kr, kdt, bdm, nt=True)
    n_ab = jnp.where(strict, g_b[:CHUNK], 0.0)
    a_kv = jnp.where(strict, g_k[:CHUNK], 0.0)
    m_b = jnp.where(incl, g_b[CHUNK:], 0.0)
    m_kd = jnp.where(incl, g_k[CHUNK:], 0.0)
    t_inv = eye_f - n_ab
    pw = _head_prod(n_ab, n_ab, bdm)
    for it in range(5):
        t_inv = t_inv + _head_prod(t_inv, pw, bdm)
        if it < 4:
            pw = _head_prod(pw, pw, bdm)
    w = _head_prod(t_inv, kt, bdm)
    u_v = _head_prod(t_inv, _head_prod(a_kv, v, bdm), bdm)
    s = s_ref[...]
    u = _head_prod(w, s, bdm, nt=True) + u_v
    y_ref[...] = _head_prod(rt, s, bdm, nt=True) + _head_prod(m_kd, v, bdm) - _head_prod(m_b, u, bdm)
    upd = _head_prod_tn(jnp.concatenate([v, u], axis=0), jnp.concatenate([kdt, -bt], axis=0), bdm)
    s_ref[...] = (s + upd) * e_tot


def _rwkv_scan_kernel(rf_ref, vf_ref, kapf_ref, lwf_ref, bf_ref, kdf_ref, rr_ref, vr_ref, kapr_ref, lwr_ref, br_ref,
                      kdr_ref, yf_ref, yr_ref, sf_ref, sr_ref):
    @pl.when(pl.program_id(1) == 0)
    def _():
        sf_ref[...] = jnp.zeros_like(sf_ref)
        sr_ref[...] = jnp.zeros_like(sr_ref)

    _rwkv_chunk(rf_ref, vf_ref, kapf_ref, lwf_ref, bf_ref, kdf_ref, yf_ref, sf_ref, False)
    _rwkv_chunk(rr_ref, vr_ref, kapr_ref, lwr_ref, br_ref, kdr_ref, yr_ref, sr_ref, True)


def _scan_block_map(rev, n_lat_blocks, nb_lat, nb_ctx):
    def f(b, s):
        c_ctx = (nb_ctx - 1 - s) if rev else s
        c_lat = (nb_lat - 1 - (s - nb_ctx)) if rev else (s - nb_ctx)
        return jnp.where(s < nb_ctx, n_lat_blocks + b * nb_ctx + c_ctx, b * nb_lat + c_lat)
    return f


def _rwkv_scan(r, v, kap, lw, bb, kd, *, n_batch, nc_lat, nc_ctx):
    n = r.shape[0]
    specs = []
    for d, rev in enumerate((False, True)):
        blk = _scan_block_map(rev, n_batch * nc_lat, nc_lat, nc_ctx)
        row = pl.BlockSpec((CHUNK, RWKV_DIM), lambda b, s, blk=blk: (blk(b, s), 0))
        rowd = pl.BlockSpec((None, CHUNK, RWKV_DIM), lambda b, s, blk=blk, d=d: (d, blk(b, s), 0))
        specs.append((row, rowd))
    (rowf, rowdf), (rowr, rowdr) = specs
    out = jax.ShapeDtypeStruct((n, RWKV_DIM), F32)
    return pl.pallas_call(
        _rwkv_scan_kernel,
        out_shape=[out, out],
        grid=(n_batch, nc_ctx + nc_lat),
        in_specs=[rowf, rowf, rowf, rowdf, rowdf, rowdf, rowr, rowr, rowr, rowdr, rowdr, rowdr],
        out_specs=[rowf, rowr],
        scratch_shapes=[pltpu.VMEM((RWKV_HEAD, RWKV_DIM), F32), pltpu.VMEM((RWKV_HEAD, RWKV_DIM), F32)],
        compiler_params=_cparams(("parallel", "arbitrary")),
        name="rwkv_scan",
    )(r, v, kap, lw, bb, kd, r, v, kap, lw, bb, kd)


def _gla_kernel(u_ref, up_ref, un_ref, cw_ref, aw_ref, ab_ref, o_ref, st_ref, qkv_s, g_s, *, rev, nb_lat, nb_ctx):
    s = pl.program_id(1)

    @pl.when(s == 0)
    def _():
        st_ref[...] = jnp.zeros_like(st_ref)

    c_ctx = (nb_ctx - 1 - s) if rev else s
    c_lat = (nb_lat - 1 - (s - nb_ctx)) if rev else (s - nb_ctx)
    is_ctx = s < nb_ctx
    c = jnp.where(is_ctx, c_ctx, c_lat)
    first = c == 0
    last = c == jnp.where(is_ctx, nb_ctx - 1, nb_lat - 1)
    for c0 in range(0, GLA_QKV, 512):
        p, prev, nxt = _neighbours(u_ref, up_ref, un_ref, c0, c0 + 512, first, last)
        conv = cw_ref[0:1, c0:c0 + 512] * prev + cw_ref[1:2, c0:c0 + 512] * p + cw_ref[2:3, c0:c0 + 512] * nxt
        act = _silu(conv)
        if c0 == 0:
            act = act * (GLA_DK ** -0.5)
        qkv_s[:, c0:c0 + 512] = act
    z = _mm_x3(u_ref[:, GLA_QKV:GLA_COLS], aw_ref[...]) + ab_ref[...]
    g_s[...] = -_softplus(-z) * (1.0 / GLA_TAU)

    incl, _, _ = _order_masks(CHUNK, rev)
    incl_b = incl.astype(BF16)
    n_chunks = u_ref.shape[0] // CHUNK
    order = range(n_chunks - 1, -1, -1) if rev else range(n_chunks)
    for ch in order:
        rows = slice(ch * CHUNK, (ch + 1) * CHUNK)
        g = g_s[rows, :]
        gc = _mm_mask(incl_b, g)
        gl = jnp.sum(g, axis=0, keepdims=True)
        q = qkv_s[rows, 0:GLA_KDIM]
        k = qkv_s[rows, GLA_KDIM:2 * GLA_KDIM]
        qg = q * jnp.exp(gc)
        kg = k * jnp.exp(-gc)
        ke = k * jnp.exp(gl - gc)
        e_l = jnp.exp(gl)
        for h in range(GLA_HEADS):
            ks = slice(h * GLA_DK, (h + 1) * GLA_DK)
            vs = slice(2 * GLA_KDIM + h * GLA_DV, 2 * GLA_KDIM + (h + 1) * GLA_DV)
            v_h = qkv_s[rows, vs]
            att = jnp.where(incl, _mm_nt(qg[:, ks], kg[:, ks]), 0.0)
            st = st_ref[h]
            o_ref[rows, h * GLA_DV:(h + 1) * GLA_DV] = _mm(att, v_h) + _mm_nt(qg[:, ks], st)
            st_ref[h] = st * e_l[:, ks] + _mm_tn(v_h, ke[:, ks])


def _gla_scan(u, conv_w, alpha_w, alpha_b, *, rev, tm, n_batch, nb_lat, nb_ctx):
    n = u.shape[0]
    blk = _scan_block_map(rev, n_batch * nb_lat, nb_lat, nb_ctx)
    r8 = tm // 8
    last8 = n // 8 - 1
    full = lambda a: pl.BlockSpec(a.shape, lambda b, s: (0,) * a.ndim)
    return pl.pallas_call(
        functools.partial(_gla_kernel, rev=rev, nb_lat=nb_lat, nb_ctx=nb_ctx),
        out_shape=jax.ShapeDtypeStruct((n, GLA_VDIM), F32),
        grid=(n_batch, nb_ctx + nb_lat),
        in_specs=[pl.BlockSpec((tm, GLA_COLS), lambda b, s: (blk(b, s), 0)),
                  pl.BlockSpec((8, GLA_COLS), lambda b, s: (jnp.maximum(blk(b, s) * r8 - 1, 0), 0)),
                  pl.BlockSpec((8, GLA_COLS), lambda b, s: (jnp.minimum((blk(b, s) + 1) * r8, last8), 0)),
                  full(conv_w), full(alpha_w), full(alpha_b)],
        out_specs=pl.BlockSpec((tm, GLA_VDIM), lambda b, s: (blk(b, s), 0)),
        scratch_shapes=[pltpu.VMEM((GLA_HEADS, GLA_DV, GLA_DK), F32),
                        pltpu.VMEM((tm, GLA_QKV), F32),
                        pltpu.VMEM((tm, GLA_KDIM), F32)],
        compiler_params=_cparams(("parallel", "arbitrary")),
        name="gla_scan_rev" if rev else "gla_scan_fwd",
    )(u, u, u, conv_w, alpha_w, alpha_b)


def _pair_tables():
    ri = lax.broadcasted_iota(jnp.int32, (LANES, LANES), 0)
    ci = lax.broadcasted_iota(jnp.int32, (LANES, LANES), 1)
    grp = ci // 6
    pr = ci % 6
    lo = jnp.where(pr < 3, 0, jnp.where(pr < 5, 1, 2))
    hi = jnp.where(pr < 3, pr + 1, jnp.where(pr < 5, pr - 1, 3))
    member = ((ri == grp * EXPERTS_PER_GROUP + lo) | (ri == grp * EXPERTS_PER_GROUP + hi)) & (ci < N_PAIRS)
    return member


def _route(h, rw_ref, rb_ref):
    lane = lax.broadcasted_iota(jnp.int32, (h.shape[0], LANES), 1)
    scores = _sigmoid(_mm_x3(h, rw_ref[...]))
    sel = jnp.where(lane < N_EXPERTS, scores + rb_ref[...], 0.0)
    member = _pair_tables()
    s3 = _split3(sel)
    mb = member.astype(BF16)
    d = functools.partial(jnp.dot, preferred_element_type=F32)
    ps = d(s3[0], mb) + (d(s3[1], mb) + d(s3[2], mb))
    ps = jnp.where(lane < N_PAIRS, ps, -jnp.inf)
    best = jnp.max(ps, axis=-1, keepdims=True)
    bucket = jnp.min(jnp.where(ps == best, lane, LANES), axis=-1, keepdims=True)
    onehot = (lane == bucket).astype(BF16)
    chosen = lax.dot_general(onehot, mb, (((1,), (1,)), ((), ())), preferred_element_type=F32) > 0.5
    picked = jnp.where(chosen, scores, 0.0)
    gates = picked / jnp.sum(picked, axis=-1, keepdims=True)
    e_lo = jnp.min(jnp.where(chosen, lane, LANES), axis=-1, keepdims=True)
    e_hi = jnp.max(jnp.where(chosen, lane, -1), axis=-1, keepdims=True)
    w_lo = jnp.sum(jnp.where(lane == e_lo, gates, 0.0), axis=-1, keepdims=True)
    w_hi = jnp.sum(jnp.where(lane == e_hi, gates, 0.0), axis=-1, keepdims=True)
    out = jnp.where(lane == 0, e_lo.astype(F32), 0.0)
    out = jnp.where(lane == 1, e_hi.astype(F32), out)
    out = jnp.where(lane == 2, w_lo, out)
    return jnp.where(lane == 3, w_hi, out)


def _merge_kernel(x_ref, y0_ref, y1_ref, bonus_ref, gate_ref, o0_ref, o1_ref, pg_ref, mods_ref,
                  lng_ref, lnb_ref, gng_ref, wa_ref, wb_ref, wo_ref, nfg_ref, rw_ref, rb_ref,
                  xs_o, h_o, route_o):
    y = y0_ref[...] + y1_ref[...]
    mean = _head_sum64(y) * (1.0 / RWKV_HEAD)
    yc = y - mean
    var = _head_sum64(yc * yc) * (1.0 / RWKV_HEAD)
    ya = (yc * lax.rsqrt(var + RWKV_GN_EPS) * lng_ref[...] + lnb_ref[...] + bonus_ref[...]) * gate_ref[...]
    o = o0_ref[...] + o1_ref[...]
    parts = []
    for h in range(GLA_HEADS):
        o_h = o[:, h * GLA_DV:(h + 1) * GLA_DV]
        parts.append(o_h * lax.rsqrt(jnp.mean(o_h * o_h, axis=-1, keepdims=True) + NORM_EPS) * gng_ref[...])
    yb = jnp.concatenate(parts, axis=1) * _silu(pg_ref[:, 0:GLA_VDIM].astype(F32))
    ga = _sigmoid(pg_ref[:, GLA_VDIM:GLA_VDIM + D_MODEL].astype(F32))
    gb = _sigmoid(pg_ref[:, GLA_VDIM + D_MODEL:GATE_COLS].astype(F32))
    z = ga * _mm(ya, wa_ref[...]) + gb * _mm(yb, wb_ref[...])
    xs = x_ref[...] + mods_ref[0, 2:3, :] * _mm(z, wo_ref[...])
    xs_o[...] = xs
    h = _rms(xs, nfg_ref[...]) * (1.0 + mods_ref[0, 4:5, :]) + mods_ref[0, 3:4, :]
    h_o[...] = h
    route_o[...] = _route(h, rw_ref, rb_ref)


def _merge(x, y0, y1, bonus, gate, o0, o1, pg, mods, ln_g, ln_b, gn_g, wa, wb, wo, nf_g, rw, rb,
           *, tm, n_rows, n_lat_tiles, tiles_per_batch, n_batch):
    d = D_MODEL
    mrow = _mod_row_map(n_lat_tiles, tiles_per_batch, n_batch)
    row = lambda c: pl.BlockSpec((tm, c), lambda i: (i, 0))
    full = lambda a: pl.BlockSpec(a.shape, lambda i: (0,) * a.ndim)
    consts = [ln_g.reshape(1, -1), ln_b.reshape(1, -1), gn_g.reshape(1, -1), wa, wb, wo, nf_g.reshape(1, -1), rw, rb]
    return pl.pallas_call(
        _merge_kernel,
        out_shape=[jax.ShapeDtypeStruct((n_rows, d), F32), jax.ShapeDtypeStruct((n_rows, d), F32),
                   jax.ShapeDtypeStruct((n_rows, LANES), F32)],
        grid=(n_rows // tm,),
        in_specs=[row(d), row(RWKV_DIM), row(RWKV_DIM), row(RWKV_DIM), row(RWKV_DIM), row(GLA_VDIM), row(GLA_VDIM),
                  row(GATE_COLS), pl.BlockSpec((1, 6, d), mrow)] + [full(a) for a in consts],
        out_specs=[row(d), row(d), row(LANES)],
        compiler_params=_cparams(("parallel",)),
        name="merge",
    )(x, y0, y1, bonus, gate, o0, o1, pg, mods, *consts)


DMA_ISSUE_UNROLL = 8


def _moe_kernel(te_ref, nu_ref, nv_ref, src_ref, src_next_ref, dst_ref, w_ref, h_hbm, wg_ref, wu_ref, wd_ref, o_hbm,
                xbuf, obuf, sem_in, sem_out):
    del te_ref
    tm = xbuf.shape[1]
    i = pl.program_id(0)
    n_used = nu_ref[0]
    slot = i % 2

    def start_gather(idx_ref, s):
        def body(r, carry):
            pltpu.make_async_copy(h_hbm.at[pl.ds(idx_ref[0, 0, r], 1)], xbuf.at[s, pl.ds(r, 1)], sem_in.at[s]).start()
            return carry
        lax.fori_loop(0, tm, body, 0, unroll=DMA_ISSUE_UNROLL)

    def wait_gather(s):
        pltpu.make_async_copy(h_hbm.at[pl.ds(0, tm)], xbuf.at[s], sem_in.at[s]).wait()

    def row_out(s, r, dst_row):
        return pltpu.make_async_copy(obuf.at[s, pl.ds(r, 1)], o_hbm.at[pl.ds(dst_row, 1)], sem_out.at[s])

    def start_scatter(s, n_rows):
        def body(r, carry):
            @pl.when(r < n_rows)
            def _():
                row_out(s, r, dst_ref[0, 0, r]).start()
            return carry
        lax.fori_loop(0, tm, body, 0, unroll=DMA_ISSUE_UNROLL)

    def wait_scatter(s, n_rows):
        @pl.when(n_rows == tm)
        def _():
            pltpu.make_async_copy(obuf.at[s], o_hbm.at[pl.ds(0, tm)], sem_out.at[s]).wait()

        @pl.when(n_rows < tm)
        def _():
            def body(r, carry):
                row_out(s, r, 0).wait()
                return carry
            lax.fori_loop(0, n_rows, body, 0)

    @pl.when(i < n_used)
    def _():
        @pl.when(i == 0)
        def _():
            start_gather(src_ref, 0)

        @pl.when(i + 1 < n_used)
        def _():
            start_gather(src_next_ref, 1 - slot)

        wait_gather(slot)

        @pl.when(i >= 2)
        def _():
            wait_scatter(slot, nv_ref[i - 2])

        xb = xbuf[slot].astype(BF16)
        g = jnp.dot(xb, wg_ref[...], preferred_element_type=F32)
        u = jnp.dot(xb, wu_ref[...], preferred_element_type=F32)
        act = (_silu(g) * u).astype(BF16)
        obuf[slot] = jnp.dot(act, wd_ref[...], preferred_element_type=F32) * w_ref[...][:, 0:1]
        start_scatter(slot, nv_ref[i])

        @pl.when(i == n_used - 1)
        def _():
            @pl.when(i >= 1)
            def _():
                wait_scatter(1 - slot, nv_ref[i - 1])
            wait_scatter(slot, nv_ref[i])


def _moe(h, route, wg, wu, wd, *, tm):
    t, d = h.shape
    e_lo = route[:, 0].astype(jnp.int32)
    e_hi = route[:, 1].astype(jnp.int32)
    ids = jnp.concatenate([e_lo, e_hi])
    wts = jnp.concatenate([route[:, 2], route[:, 3]])
    n_tiles = (2 * t) // tm + N_EXPERTS
    order = jnp.argsort(ids, stable=True).astype(jnp.int32)
    counts = jnp.sum((ids[:, None] == jnp.arange(N_EXPERTS, dtype=jnp.int32)[None, :]).astype(jnp.int32), axis=0)
    starts = jnp.cumsum(counts) - counts
    pcounts = ((counts + tm - 1) // tm) * tm
    pend = jnp.cumsum(pcounts)
    poff = pend - pcounts
    n_used = (pend[-1] // tm).astype(jnp.int32)
    tile_start = jnp.arange(n_tiles, dtype=jnp.int32) * tm
    te = jnp.sum((tile_start[:, None] >= pend[None, :]).astype(jnp.int32), axis=1)
    te_last = jnp.take(te, jnp.maximum(n_used - 1, 0))
    te = jnp.where(jnp.arange(n_tiles) < n_used, te, te_last).astype(jnp.int32)
    pos = jnp.arange(n_tiles * tm, dtype=jnp.int32)
    e_p = jnp.take(te, pos // tm)
    rank = pos - jnp.take(poff, e_p)
    valid = (rank < jnp.take(counts, e_p)) & (pos // tm < n_used)
    a = jnp.take(order, jnp.clip(jnp.take(starts, e_p) + rank, 0, 2 * t - 1))
    tok = a % t
    src = jnp.where(valid, tok, 0).astype(jnp.int32).reshape(n_tiles, 1, tm)
    dst = jnp.where(valid, a, 0).astype(jnp.int32).reshape(n_tiles, 1, tm)
    n_valid = jnp.sum(valid.reshape(n_tiles, tm).astype(jnp.int32), axis=1)
    wrow = jnp.broadcast_to(jnp.where(valid, jnp.take(wts, a), 0.0)[:, None], (n_tiles * tm, LANES))

    smem = functools.partial(pl.BlockSpec, (1, 1, tm), memory_space=pltpu.SMEM)
    return pl.pallas_call(
        _moe_kernel,
        out_shape=jax.ShapeDtypeStruct((2 * t, d), F32),
        grid_spec=pltpu.PrefetchScalarGridSpec(
            num_scalar_prefetch=3,
            grid=(n_tiles,),
            in_specs=[smem(lambda i, *_: (i, 0, 0)),
                      smem(lambda i, *_: (jnp.minimum(i + 1, n_tiles - 1), 0, 0)),
                      smem(lambda i, *_: (i, 0, 0)),
                      pl.BlockSpec((tm, LANES), lambda i, *_: (i, 0)),
                      pl.BlockSpec(memory_space=pl.ANY),
                      pl.BlockSpec((None, d, D_EXPERT), lambda i, te_r, *_: (te_r[i], 0, 0)),
                      pl.BlockSpec((None, d, D_EXPERT), lambda i, te_r, *_: (te_r[i], 0, 0)),
                      pl.BlockSpec((None, D_EXPERT, d), lambda i, te_r, *_: (te_r[i], 0, 0))],
            out_specs=pl.BlockSpec(memory_space=pl.ANY),
            scratch_shapes=[pltpu.VMEM((2, tm, d), F32), pltpu.VMEM((2, tm, d), F32),
                            pltpu.SemaphoreType.DMA((2,)), pltpu.SemaphoreType.DMA((2,))]),
        compiler_params=_cparams(("arbitrary",)),
        name="moe_ffn",
    )(te, n_used.reshape(1), n_valid, src, src, dst, wrow, h, wg, wu, wd)


def _pad_cols(a, n):
    return jnp.pad(a, [(0, 0)] * (a.ndim - 1) + [(0, n - a.shape[-1])])


def _pad_rows(a, n):
    return jnp.pad(a, [(0, 0)] * (a.ndim - 2) + [(0, n - a.shape[-2]), (0, 0)])


def _split_in_proj(w_in, mu):
    s = (RWKV_DIM, 2 * RWKV_DIM, 3 * RWKV_DIM, 3 * RWKV_DIM + DECAY_LORA, 3 * RWKV_DIM + DECAY_LORA + ICLR_LORA)

    def rw_layout(a):
        return jnp.concatenate([a[..., :s[2]], _pad_cols(a[..., s[2]:s[3]], LORA_PAD),
                                _pad_cols(a[..., s[3]:s[4]], LORA_PAD), a[..., s[4]:RWKV_IN]], axis=-1)

    w_rw = rw_layout(w_in)
    w_gla = _pad_cols(w_in[:, RWKV_IN:GLA_GATE_OFF], GLA_COLS)
    w_gate = w_in[:, GLA_GATE_OFF:]
    return w_rw.astype(BF16), w_gla.astype(BF16), w_gate.astype(BF16), rw_layout(mu)


def kernel(x, c, ctx, c_ctx, w_ada, b_ada, norm_mix_g, norm_ffn_g, w_in, rwkv_mu, rwkv_w0, rwkv_w2, rwkv_a0,
           rwkv_a2, rwkv_g2, rwkv_k_k, rwkv_k_a, rwkv_r_k, rwkv_ln_g, rwkv_ln_b, gla_conv, gla_alpha_w2,
           gla_alpha_b, gla_norm_g, w_branch_a, w_branch_b, w_out, router_w, router_b, exp_w_gate, exp_w_up,
           exp_w_down, final_norm_g):
    n_batch, seq, d = x.shape
    ctx_len = ctx.shape[1]
    assert d == D_MODEL and seq % GRID_W == 0 and n_batch < MOD_ROWS
    grid_rows = seq // GRID_W
    tm = min(256, ctx_len)
    assert seq % tm == 0 and ctx_len % tm == 0 and tm % CHUNK == 0
    t_lat, t_ctx = n_batch * seq, n_batch * ctx_len
    t_all = t_lat + t_ctx
    tiles = dict(tm=tm, n_lat_tiles=t_lat // tm)
    seq_tiles = dict(tps_lat=seq // tm, tps_ctx=ctx_len // tm)
    mod_tiles = dict(tiles_per_batch=seq // tm, n_batch=n_batch)

    xs = jnp.concatenate([x.reshape(t_lat, d), ctx.reshape(t_ctx, d)], axis=0)
    c_all = jnp.zeros((MOD_ROWS, d), F32).at[:n_batch].set(c).at[n_batch].set(c_ctx)
    mods = _adaln(c_all, w_ada, b_ada).reshape(DEPTH, MOD_ROWS, 6, d)
    rw_pad = _pad_cols(router_w, LANES)
    rb_pad = _pad_cols(router_b.reshape(1, -1), LANES)

    moe_out = None
    for l in range(DEPTH):
        last = l == DEPTH - 1
        if moe_out is None:
            (h,) = _prenorm(xs, None, None, mods[l], norm_mix_g[l], n_rows=t_all, **tiles, **mod_tiles)
        else:
            xs, h = _prenorm(xs, moe_out, mods[l - 1], mods[l], norm_mix_g[l], n_rows=t_all, **tiles, **mod_tiles)
        w_rw, w_gla, w_gate, mu = _split_in_proj(w_in[l], rwkv_mu[l])
        p_rw = _matmul(h, w_rw, F32, tn=RW_COLS // 7)
        p_gla = _matmul(h, w_gla, F32, tn=GLA_COLS)
        p_gate = _matmul(h, w_gate, BF16, tn=GATE_COLS // 5)

        r, v, kap, lw, bb, kd, bonus, gate = _rwkv_prep(
            p_rw, mu, rwkv_w0[l], _pad_rows(rwkv_w2[l], LORA_PAD), rwkv_a0[l], _pad_rows(rwkv_a2[l], LORA_PAD),
            rwkv_g2[l], rwkv_k_k[l].reshape(1, -1), rwkv_k_a[l].reshape(1, -1), rwkv_r_k[l].reshape(1, -1),
            **tiles, **seq_tiles)
        y0, y1 = _rwkv_scan(r, v, kap, lw, bb, kd, n_batch=n_batch, nc_lat=seq // CHUNK, nc_ctx=ctx_len // CHUNK)

        u_lat = p_gla[:t_lat].reshape(n_batch, grid_rows, GRID_W, GLA_COLS).transpose(0, 2, 1, 3)
        u = jnp.concatenate([u_lat.reshape(t_lat, GLA_COLS), p_gla[t_lat:]], axis=0)
        conv_w = _pad_cols(gla_conv[l], GLA_QKV)
        outs = []
        for dr in range(2):
            o = _gla_scan(u, conv_w, _pad_rows(gla_alpha_w2[l, dr], LANES), gla_alpha_b[l, dr].reshape(1, -1),
                          rev=dr == 1, tm=tm, n_batch=n_batch, nb_lat=seq // tm, nb_ctx=ctx_len // tm)
            o_lat = o[:t_lat].reshape(n_batch, GRID_W, grid_rows, GLA_VDIM).transpose(0, 2, 1, 3)
            outs.append(jnp.concatenate([o_lat.reshape(t_lat, GLA_VDIM), o[t_lat:]], axis=0))

        n_rows = t_lat if last else t_all
        xs, hf, route = _merge(
            xs, y0, y1, bonus, gate, outs[0], outs[1], p_gate, mods[l], rwkv_ln_g[l], rwkv_ln_b[l], gla_norm_g[l],
            w_branch_a[l].astype(BF16), w_branch_b[l].astype(BF16), w_out[l].astype(BF16), norm_ffn_g[l],
            rw_pad, rb_pad, n_rows=n_rows, **tiles, **mod_tiles)
        moe_out = _moe(hf, route, exp_w_gate[l].astype(BF16), exp_w_up[l].astype(BF16), exp_w_down[l].astype(BF16),
                       tm=tm)

    (out,) = _prenorm(xs, moe_out, mods[DEPTH - 1], mods[DEPTH - 1], final_norm_g, n_rows=t_lat, final=True,
                      **tiles, **mod_tiles)
    return out.reshape(n_batch, seq, d)
```

```python
import functools

import jax
import jax.numpy as jnp
from jax import lax
from jax.experimental import pallas as pl
from jax.experimental.pallas import tpu as pltpu

F32 = jnp.float32
BF16 = jnp.bfloat16

D_MODEL = 2048
DEPTH = 2
GRID_W = 64
NORM_EPS = 1e-6
RWKV_DIM = 1024
RWKV_HEAD = 64
RWKV_HEADS = 16
DECAY_LORA = 96
ICLR_LORA = 96
GATE_LORA = 256
RWKV_GN_EPS = 64e-5
GLA_HEADS = 4
GLA_KDIM = 512
GLA_VDIM = 1024
GLA_DK = 128
GLA_DV = 256
GLA_GATE_LORA = 16
GLA_TAU = 16.0
GLA_QKV = 2048
N_EXPERTS = 16
EXPERTS_PER_GROUP = 4
N_PAIRS = 24
D_EXPERT = 1408
RWKV_IN = 3520
GLA_SEQ_IN = 2064
GLA_GATE_OFF = RWKV_IN + GLA_SEQ_IN
BR_GATE_OFF = GLA_GATE_OFF + GLA_VDIM

LANES = 128
CHUNK = 64
MOD_ROWS = 16
LORA_PAD = 128
RW_XW = 3 * RWKV_DIM
RW_XA = RW_XW + LORA_PAD
RW_XG = RW_XA + LORA_PAD
RW_COLS = RW_XG + GATE_LORA
GLA_COLS = GLA_QKV + LANES
GATE_COLS = GLA_VDIM + 2 * D_MODEL
VMEM_LIMIT = 56 * 1024 * 1024


def _cparams(sem, vmem=VMEM_LIMIT):
    return pltpu.CompilerParams(dimension_semantics=sem, vmem_limit_bytes=vmem)


def _pick_tile(n, cands):
    for c in cands:
        if n % c == 0:
            return c
    raise ValueError(f"no tile for {n}")


def _mm(a, b):
    return jnp.dot(a.astype(BF16), b.astype(BF16), preferred_element_type=F32)


def _mm_nt(a, b):
    return lax.dot_general(a.astype(BF16), b.astype(BF16), (((1,), (1,)), ((), ())), preferred_element_type=F32)


def _mm_tn(a, b):
    return lax.dot_general(a.astype(BF16), b.astype(BF16), (((0,), (0,)), ((), ())), preferred_element_type=F32)


def _split2(a):
    hi = a.astype(BF16)
    lo = (a - hi.astype(F32)).astype(BF16)
    return hi, lo


def _split3(a):
    hi = a.astype(BF16)
    r1 = a - hi.astype(F32)
    mid = r1.astype(BF16)
    lo = (r1 - mid.astype(F32)).astype(BF16)
    return hi, mid, lo


def _mm_x3(a, b):
    ah, al = _split2(a)
    bh, bl = _split2(b)
    d = functools.partial(jnp.dot, preferred_element_type=F32)
    return d(ah, bh) + (d(al, bh) + d(ah, bl))


def _mm_mask(m01, x):
    h, m, l = _split3(x)
    d = functools.partial(jnp.dot, preferred_element_type=F32)
    return d(m01, h) + (d(m01, m) + d(m01, l))


def _sigmoid(x):
    return 1.0 / (1.0 + jnp.exp(-x))


def _silu(x):
    return x * _sigmoid(x)


def _softplus(z):
    return jnp.maximum(z, 0.0) + jnp.log1p(jnp.exp(-jnp.abs(z)))


def _head_sum64(x):
    ri = lax.broadcasted_iota(jnp.int32, (256, 256), 0) // RWKV_HEAD
    ci = lax.broadcasted_iota(jnp.int32, (256, 256), 1) // RWKV_HEAD
    ones = (ri == ci).astype(BF16)
    hi, lo = _split2(x)
    d = functools.partial(jnp.dot, preferred_element_type=F32)
    outs = [d(hi[:, g * 256:(g + 1) * 256], ones) + d(lo[:, g * 256:(g + 1) * 256], ones)
            for g in range(x.shape[1] // 256)]
    return jnp.concatenate(outs, axis=1)


def _adaln_kernel(c_ref, w_ref, b_ref, o_ref):
    c = c_ref[...]
    o_ref[...] = _mm_x3(_silu(c), w_ref[...]) + b_ref[...]


def _adaln(c_all, w_ada, b_ada):
    nl, d, n = w_ada.shape
    tn = 1024
    return pl.pallas_call(
        _adaln_kernel,
        out_shape=jax.ShapeDtypeStruct((nl, MOD_ROWS, n), F32),
        grid=(nl, n // tn),
        in_specs=[pl.BlockSpec((MOD_ROWS, d), lambda l, j: (0, 0)),
                  pl.BlockSpec((None, d, tn), lambda l, j: (l, 0, j)),
                  pl.BlockSpec((None, 1, tn), lambda l, j: (l, 0, j))],
        out_specs=pl.BlockSpec((None, MOD_ROWS, tn), lambda l, j: (l, 0, j)),
        compiler_params=_cparams(("parallel", "parallel")),
        name="adaln",
    )(c_all, w_ada, b_ada.reshape(nl, 1, n))


def _rms(x, g):
    return x * lax.rsqrt(jnp.mean(x * x, axis=-1, keepdims=True) + NORM_EPS) * g


def _prenorm_kernel(*refs, has_moe, final):
    if has_moe:
        x_ref, o0_ref, o1_ref, mprev_ref, mcur_ref, g_ref = refs[:6]
        outs = refs[6:]
        x = x_ref[...] + mprev_ref[0, 5:6, :] * (o0_ref[...] + o1_ref[...])
    else:
        x_ref, mcur_ref, g_ref = refs[:3]
        outs = refs[3:]
        x = x_ref[...]
    y = _rms(x, g_ref[...])
    if final:
        outs[0][...] = y
        return
    h = y * (1.0 + mcur_ref[0, 1:2, :]) + mcur_ref[0, 0:1, :]
    if has_moe:
        outs[0][...] = x
        outs[1][...] = h.astype(outs[1].dtype)
    else:
        outs[0][...] = h.astype(outs[0].dtype)


def _mod_row_map(n_lat_tiles, tiles_per_batch, n_batch):
    return lambda i: (jnp.where(i < n_lat_tiles, i // tiles_per_batch, n_batch), 0, 0)


def _prenorm(x, moe_out, mods_prev, mods_cur, g, *, tm, n_lat_tiles, tiles_per_batch, n_batch, n_rows, final=False):
    d = x.shape[1]
    has_moe = moe_out is not None
    mrow = _mod_row_map(n_lat_tiles, tiles_per_batch, n_batch)
    row = pl.BlockSpec((tm, d), lambda i: (i, 0))
    mspec = pl.BlockSpec((1, 6, d), mrow)
    gspec = pl.BlockSpec((1, d), lambda i: (0, 0))
    ins, specs = [x], [row]
    if has_moe:
        slot1 = n_rows // tm
        ins += [moe_out, moe_out, mods_prev]
        specs += [row, pl.BlockSpec((tm, d), lambda i: (slot1 + i, 0)), mspec]
    ins += [mods_cur, g.reshape(1, d)]
    specs += [mspec, gspec]
    if final:
        out_shape = [jax.ShapeDtypeStruct((n_rows, d), F32)]
    elif has_moe:
        out_shape = [jax.ShapeDtypeStruct((n_rows, d), F32), jax.ShapeDtypeStruct((n_rows, d), BF16)]
    else:
        out_shape = [jax.ShapeDtypeStruct((n_rows, d), BF16)]
    return pl.pallas_call(
        functools.partial(_prenorm_kernel, has_moe=has_moe, final=final),
        out_shape=out_shape,
        grid=(n_rows // tm,),
        in_specs=specs,
        out_specs=[row] * len(out_shape),
        compiler_params=_cparams(("parallel",)),
        name="prenorm_final" if final else "prenorm",
    )(*ins)


def _matmul_kernel(a_ref, w_ref, o_ref):
    o_ref[...] = jnp.dot(a_ref[...], w_ref[...], preferred_element_type=F32).astype(o_ref.dtype)


def _matmul(a, w, out_dtype, *, tn):
    m, k = a.shape
    n = w.shape[1]
    tm = _pick_tile(m, (1024, 512, 256, 128, 64))
    return pl.pallas_call(
        _matmul_kernel,
        out_shape=jax.ShapeDtypeStruct((m, n), out_dtype),
        grid=(m // tm, n // tn),
        in_specs=[pl.BlockSpec((tm, k), lambda i, j: (i, 0)),
                  pl.BlockSpec((k, tn), lambda i, j: (0, j))],
        out_specs=pl.BlockSpec((tm, tn), lambda i, j: (i, j)),
        compiler_params=_cparams(("parallel", "parallel")),
        name="in_proj",
    )(a, w)


def _seq_flags(i, n_lat_tiles, tps_lat, tps_ctx):
    is_lat = i < n_lat_tiles
    j = jnp.where(is_lat, i % tps_lat, (i - n_lat_tiles) % tps_ctx)
    return j == 0, j == jnp.where(is_lat, tps_lat - 1, tps_ctx - 1)


def _neighbours(ref, prev_ref, next_ref, c0, c1, first, last):
    tm = ref.shape[0]
    p = ref[:, c0:c1]
    rows = lax.broadcasted_iota(jnp.int32, (tm, 1), 0)
    pr = jnp.where(first, 0.0, prev_ref[7:8, c0:c1])
    nx = jnp.where(last, 0.0, next_ref[0:1, c0:c1])
    prev = jnp.where(rows == 0, pr, pltpu.roll(p, 1, 0))
    nxt = jnp.where(rows == tm - 1, nx, pltpu.roll(p, tm - 1, 0))
    return p, prev, nxt


def _halo_specs(tm, cols, n_rows):
    r8 = tm // 8
    last8 = n_rows // 8 - 1
    return [pl.BlockSpec((tm, cols), lambda i: (i, 0)),
            pl.BlockSpec((8, cols), lambda i: (jnp.maximum(i * r8 - 1, 0), 0)),
            pl.BlockSpec((8, cols), lambda i: (jnp.minimum((i + 1) * r8, last8), 0))]


def _rwkv_prep_kernel(p_ref, pp_ref, pn_ref, mu_ref, w0_ref, w2_ref, a0_ref, a2_ref, g2_ref, kk_ref, ka_ref, rk_ref,
                      r_o, v_o, kap_o, lw_o, b_o, kd_o, bonus_o, gate_o, *, n_lat_tiles, tps_lat, tps_ctx):
    first, last = _seq_flags(pl.program_id(0), n_lat_tiles, tps_lat, tps_ctx)

    def shifted(c0, c1):
        p, prev, nxt = _neighbours(p_ref, pp_ref, pn_ref, c0, c1, first, last)
        return p + mu_ref[0:1, c0:c1] * (prev - p) + mu_ref[1:2, c0:c1] * (nxt - p)

    r = shifted(0, RWKV_DIM)
    k = shifted(RWKV_DIM, 2 * RWKV_DIM)
    v = shifted(2 * RWKV_DIM, 3 * RWKV_DIM)
    kk = k * kk_ref[...]
    kap = kk / jnp.maximum(jnp.sqrt(_head_sum64(kk * kk)), 1e-12)
    r_o[...] = r
    v_o[...] = v
    kap_o[...] = kap
    bonus_o[...] = _head_sum64(r * k * rk_ref[...]) * v
    txw = jnp.tanh(shifted(RW_XW, RW_XA))
    xa = shifted(RW_XA, RW_XG)
    for d in range(2):
        w_log = -_softplus(-(w0_ref[d:d + 1, :] + _mm_x3(txw, w2_ref[d]))) - 0.5
        lw_o[d] = -jnp.exp(w_log)
        a = _sigmoid(a0_ref[d:d + 1, :] + _mm_x3(xa, a2_ref[d]))
        kd_o[d] = k * (1.0 + (a - 1.0) * ka_ref[...])
        b_o[d] = kap * a
    gate_o[...] = _mm_x3(_sigmoid(shifted(RW_XG, RW_COLS)), g2_ref[...])


def _rwkv_prep(p_rw, mu, w0, w2, a0, a2, g2, k_k, k_a, r_k, *, tm, n_batch, n_lat_tiles, tps_lat, tps_ctx):
    n = p_rw.shape[0]
    gm = _group_major(_scan_groups(n_batch), n_batch, n_lat_tiles, tps_lat, tps_ctx)
    full = lambda a: pl.BlockSpec(a.shape, lambda i: (0,) * a.ndim)
    tok = pl.BlockSpec((tm, RWKV_DIM), lambda i: (i, 0))
    row = pl.BlockSpec((tm, RWKV_DIM), lambda i: (gm(i), 0))
    row2 = pl.BlockSpec((2, tm, RWKV_DIM), lambda i: (0, gm(i), 0))
    one = jax.ShapeDtypeStruct((n, RWKV_DIM), F32)
    two = jax.ShapeDtypeStruct((2, n, RWKV_DIM), F32)
    params = [mu, w0, w2, a0, a2, g2, k_k, k_a, r_k]
    return pl.pallas_call(
        functools.partial(_rwkv_prep_kernel, n_lat_tiles=n_lat_tiles, tps_lat=tps_lat, tps_ctx=tps_ctx),
        out_shape=[one, one, one, two, two, two, one, one],
        grid=(n // tm,),
        in_specs=_halo_specs(tm, RW_COLS, n) + [full(a) for a in params],
        out_specs=[row, row, row, row2, row2, row2, tok, tok],
        compiler_params=_cparams(("parallel",)),
        name="rwkv_prep",
    )(p_rw, p_rw, p_rw, *params)


def _order_masks(n, rev):
    ri = lax.broadcasted_iota(jnp.int32, (n, n), 0)
    ci = lax.broadcasted_iota(jnp.int32, (n, n), 1)
    if rev:
        return ci >= ri, ci > ri, ri == ci
    return ci <= ri, ci < ri, ri == ci


HEADS_PER_GROUP = 4
GROUP = HEADS_PER_GROUP * RWKV_HEAD


def _head_block_mask():
    ri = lax.broadcasted_iota(jnp.int32, (GROUP, GROUP), 0) // RWKV_HEAD
    ci = lax.broadcasted_iota(jnp.int32, (GROUP, GROUP), 1) // RWKV_HEAD
    return ri == ci


def _block_diag(y_g, bdm):
    return jnp.where(bdm, jnp.concatenate([y_g] * HEADS_PER_GROUP, axis=0), jnp.zeros((), y_g.dtype))


def _head_prod(x, y, bdm, nt=False):
    xb, yb = x.astype(BF16), y.astype(BF16)
    dims = (((1,), (1,)), ((), ())) if nt else (((1,), (0,)), ((), ()))
    outs = [lax.dot_general(xb[:, g:g + GROUP], _block_diag(yb[:, g:g + GROUP], bdm), dims,
                            preferred_element_type=F32) for g in range(0, x.shape[1], GROUP)]
    return jnp.concatenate(outs, axis=1)


def _head_prod_tn(a, b, bdm):
    ab, bb = a.astype(BF16), b.astype(BF16)
    outs = []
    for g in range(0, a.shape[1], GROUP):
        p = lax.dot_general(ab[:, g:g + GROUP], bb[:, g:g + GROUP], (((0,), (0,)), ((), ())),
                            preferred_element_type=F32)
        p = jnp.where(bdm, p, 0.0)
        outs.append((p[0:64] + p[64:128]) + (p[128:192] + p[192:256]))
    return jnp.concatenate(outs, axis=1)


def _rwkv_chunk(r_ref, v_ref, kap_ref, lw_ref, b_ref, kd_ref, y_ref, s_ref, rev):
    row = lax.broadcasted_iota(jnp.int32, (CHUNK, RWKV_DIM), 0)
    col = lax.broadcasted_iota(jnp.int32, (CHUNK, RWKV_DIM), 1) % RWKV_HEAD
    incl, strict = (col >= row, col > row) if rev else (col <= row, col < row)
    eye_f = (col == row).astype(F32)
    bdm = _head_block_mask()
    lw = lw_ref[...]
    gam = _mm_mask(_order_masks(CHUNK, rev)[0].astype(BF16), lw)
    e_tot = jnp.exp(jnp.sum(lw, axis=0, keepdims=True))
    e_neg = jnp.exp(-gam)
    rt = r_ref[...] * jnp.exp(gam)
    kt = kap_ref[...] * jnp.exp(gam - lw)
    bt = b_ref[...] * e_neg
    kdt = kd_ref[...] * e_neg
    v = v_ref[...]
    kr = jnp.concatenate([kt, rt], axis=0)
    g_b = _head_prod(kr, bt, bdm, nt=True)
    g_k = _head_prod(kr, kdt, bdm, nt=True)
    n_ab = jnp.where(strict, g_b[:CHUNK], 0.0)
    a_kv = jnp.where(strict, g_k[:CHUNK], 0.0)
    m_b = jnp.where(incl, g_b[CHUNK:], 0.0)
    m_kd = jnp.where(incl, g_k[CHUNK:], 0.0)
    t_inv = eye_f - n_ab
    pw = _head_prod(n_ab, n_ab, bdm)
    for _ in range(4):
        both = _head_prod(jnp.concatenate([t_inv, pw], axis=0), pw, bdm)
        t_inv = t_inv + both[:CHUNK]
        pw = both[CHUNK:]
    t_inv = t_inv + _head_prod(t_inv, pw, bdm)
    av = _head_prod(jnp.concatenate([a_kv, m_kd], axis=0), v, bdm)
    s = s_ref[...]
    ks = _head_prod(kr, s, bdm, nt=True)
    u = _head_prod(t_inv, ks[:CHUNK] + av[:CHUNK], bdm)
    y_ref[...] = ks[CHUNK:] + av[CHUNK:] - _head_prod(m_b, u, bdm)
    upd = _head_prod_tn(jnp.concatenate([v, u], axis=0), jnp.concatenate([kdt, -bt], axis=0), bdm)
    s_ref[...] = (s + upd) * e_tot


def _rwkv_scan_kernel(rf_ref, vf_ref, kapf_ref, lwf_ref, bf_ref, kdf_ref, rr_ref, vr_ref, kapr_ref, lwr_ref, br_ref,
                      kdr_ref, yf_ref, yr_ref, sf_ref, sr_ref):
    @pl.when(pl.program_id(1) == 0)
    def _():
        sf_ref[...] = jnp.zeros_like(sf_ref)
        sr_ref[...] = jnp.zeros_like(sr_ref)

    for p in range(rf_ref.shape[0]):
        _rwkv_chunk(rf_ref.at[p], vf_ref.at[p], kapf_ref.at[p], lwf_ref.at[p], bf_ref.at[p], kdf_ref.at[p],
                    yf_ref.at[p], sf_ref.at[p], False)
        _rwkv_chunk(rr_ref.at[p], vr_ref.at[p], kapr_ref.at[p], lwr_ref.at[p], br_ref.at[p], kdr_ref.at[p],
                    yr_ref.at[p], sr_ref.at[p], True)


def _scan_block_map(rev, n_lat_blocks, nb_lat, nb_ctx):
    def f(b, s):
        c_ctx = (nb_ctx - 1 - s) if rev else s
        c_lat = (nb_lat - 1 - (s - nb_ctx)) if rev else (s - nb_ctx)
        return jnp.where(s < nb_ctx, n_lat_blocks + b * nb_ctx + c_ctx, b * nb_lat + c_lat)
    return f


def _scan_groups(n_batch):
    return 2 if n_batch % 2 == 0 else 1


def _group_major(n_groups, n_batch, n_lat_tiles, tps_lat, tps_ctx):
    per = n_batch // n_groups
    lat, ctx = per * tps_lat, per * tps_ctx

    def f(i):
        is_lat = i < n_lat_tiles
        b = jnp.where(is_lat, i // tps_lat, (i - n_lat_tiles) // tps_ctx)
        j = jnp.where(is_lat, i % tps_lat, (i - n_lat_tiles) % tps_ctx)
        return (b // per) * (lat + ctx) + jnp.where(is_lat, (b % per) * tps_lat + j, lat + (b % per) * tps_ctx + j)
    return f


def _rwkv_scan(r, v, kap, lw, bb, kd, *, n_batch, nc_lat, nc_ctx):
    n = r.shape[0]
    ng = _scan_groups(n_batch)
    per = n_batch // ng
    g3 = lambda a: a.reshape(ng, n // ng, RWKV_DIM)
    g4 = lambda a: a.reshape(2, ng, n // ng, RWKV_DIM)
    specs = []
    for d, rev in enumerate((False, True)):
        blk = _scan_block_map(rev, per * nc_lat, nc_lat, nc_ctx)
        row = pl.BlockSpec((ng, CHUNK, RWKV_DIM), lambda b, s, blk=blk: (0, blk(b, s), 0))
        rowd = pl.BlockSpec((None, ng, CHUNK, RWKV_DIM), lambda b, s, blk=blk, d=d: (d, 0, blk(b, s), 0))
        specs.append((row, rowd))
    (rowf, rowdf), (rowr, rowdr) = specs
    out = jax.ShapeDtypeStruct((ng, n // ng, RWKV_DIM), F32)
    state = pltpu.VMEM((ng, RWKV_HEAD, RWKV_DIM), F32)
    yf, yr = pl.pallas_call(
        _rwkv_scan_kernel,
        out_shape=[out, out],
        grid=(per, nc_ctx + nc_lat),
        in_specs=[rowf, rowf, rowf, rowdf, rowdf, rowdf, rowr, rowr, rowr, rowdr, rowdr, rowdr],
        out_specs=[rowf, rowr],
        scratch_shapes=[state, state],
        compiler_params=_cparams(("parallel", "arbitrary")),
        name="rwkv_scan",
    )(g3(r), g3(v), g3(kap), g4(lw), g4(bb), g4(kd), g3(r), g3(v), g3(kap), g4(lw), g4(bb), g4(kd))
    return yf.reshape(n, RWKV_DIM), yr.reshape(n, RWKV_DIM)


def _gla_kernel(u_ref, up_ref, un_ref, cw_ref, aw_ref, ab_ref, o_ref, st_ref, qkv_s, g_s, *, rev, nb_lat, nb_ctx):
    s = pl.program_id(1)

    @pl.when(s == 0)
    def _():
        st_ref[...] = jnp.zeros_like(st_ref)

    c_ctx = (nb_ctx - 1 - s) if rev else s
    c_lat = (nb_lat - 1 - (s - nb_ctx)) if rev else (s - nb_ctx)
    is_ctx = s < nb_ctx
    c = jnp.where(is_ctx, c_ctx, c_lat)
    first = c == 0
    last = c == jnp.where(is_ctx, nb_ctx - 1, nb_lat - 1)
    for c0 in range(0, GLA_QKV, 512):
        p, prev, nxt = _neighbours(u_ref, up_ref, un_ref, c0, c0 + 512, first, last)
        conv = cw_ref[0:1, c0:c0 + 512] * prev + cw_ref[1:2, c0:c0 + 512] * p + cw_ref[2:3, c0:c0 + 512] * nxt
        act = _silu(conv)
        if c0 == 0:
            act = act * (GLA_DK ** -0.5)
        qkv_s[:, c0:c0 + 512] = act
    z = _mm_x3(u_ref[:, GLA_QKV:GLA_COLS], aw_ref[...]) + ab_ref[...]
    g_s[...] = -_softplus(-z) * (1.0 / GLA_TAU)

    incl, _, _ = _order_masks(CHUNK, rev)
    incl_b = incl.astype(BF16)
    n_chunks = u_ref.shape[0] // CHUNK
    order = range(n_chunks - 1, -1, -1) if rev else range(n_chunks)
    for ch in order:
        rows = slice(ch * CHUNK, (ch + 1) * CHUNK)
        g = g_s[rows, :]
        gc = _mm_mask(incl_b, g)
        gl = jnp.sum(g, axis=0, keepdims=True)
        q = qkv_s[rows, 0:GLA_KDIM]
        k = qkv_s[rows, GLA_KDIM:2 * GLA_KDIM]
        qg = q * jnp.exp(gc)
        kg = k * jnp.exp(-gc)
        ke = k * jnp.exp(gl - gc)
        e_l = jnp.exp(gl)
        for h in range(GLA_HEADS):
            ks = slice(h * GLA_DK, (h + 1) * GLA_DK)
            vs = slice(2 * GLA_KDIM + h * GLA_DV, 2 * GLA_KDIM + (h + 1) * GLA_DV)
            v_h = qkv_s[rows, vs]
            att = jnp.where(incl, _mm_nt(qg[:, ks], kg[:, ks]), 0.0)
            st = st_ref[h]
            o_ref[rows, h * GLA_DV:(h + 1) * GLA_DV] = _mm(att, v_h) + _mm_nt(qg[:, ks], st)
            st_ref[h] = st * e_l[:, ks] + _mm_tn(v_h, ke[:, ks])


def _gla_scan(u, conv_w, alpha_w, alpha_b, *, rev, tm, n_batch, nb_lat, nb_ctx):
    n = u.shape[0]
    blk = _scan_block_map(rev, n_batch * nb_lat, nb_lat, nb_ctx)
    r8 = tm // 8
    last8 = n // 8 - 1
    full = lambda a: pl.BlockSpec(a.shape, lambda b, s: (0,) * a.ndim)
    return pl.pallas_call(
        functools.partial(_gla_kernel, rev=rev, nb_lat=nb_lat, nb_ctx=nb_ctx),
        out_shape=jax.ShapeDtypeStruct((n, GLA_VDIM), F32),
        grid=(n_batch, nb_ctx + nb_lat),
        in_specs=[pl.BlockSpec((tm, GLA_COLS), lambda b, s: (blk(b, s), 0)),
                  pl.BlockSpec((8, GLA_COLS), lambda b, s: (jnp.maximum(blk(b, s) * r8 - 1, 0), 0)),
                  pl.BlockSpec((8, GLA_COLS), lambda b, s: (jnp.minimum((blk(b, s) + 1) * r8, last8), 0)),
                  full(conv_w), full(alpha_w), full(alpha_b)],
        out_specs=pl.BlockSpec((tm, GLA_VDIM), lambda b, s: (blk(b, s), 0)),
        scratch_shapes=[pltpu.VMEM((GLA_HEADS, GLA_DV, GLA_DK), F32),
                        pltpu.VMEM((tm, GLA_QKV), F32),
                        pltpu.VMEM((tm, GLA_KDIM), F32)],
        compiler_params=_cparams(("parallel", "arbitrary")),
        name="gla_scan_rev" if rev else "gla_scan_fwd",
    )(u, u, u, conv_w, alpha_w, alpha_b)


def _pair_tables():
    ri = lax.broadcasted_iota(jnp.int32, (LANES, LANES), 0)
    ci = lax.broadcasted_iota(jnp.int32, (LANES, LANES), 1)
    grp = ci // 6
    pr = ci % 6
    lo = jnp.where(pr < 3, 0, jnp.where(pr < 5, 1, 2))
    hi = jnp.where(pr < 3, pr + 1, jnp.where(pr < 5, pr - 1, 3))
    member = ((ri == grp * EXPERTS_PER_GROUP + lo) | (ri == grp * EXPERTS_PER_GROUP + hi)) & (ci < N_PAIRS)
    return member


def _route(h, rw_ref, rb_ref):
    lane = lax.broadcasted_iota(jnp.int32, (h.shape[0], LANES), 1)
    scores = _sigmoid(_mm_x3(h, rw_ref[...]))
    sel = jnp.where(lane < N_EXPERTS, scores + rb_ref[...], 0.0)
    member = _pair_tables()
    s3 = _split3(sel)
    mb = member.astype(BF16)
    d = functools.partial(jnp.dot, preferred_element_type=F32)
    ps = d(s3[0], mb) + (d(s3[1], mb) + d(s3[2], mb))
    ps = jnp.where(lane < N_PAIRS, ps, -jnp.inf)
    best = jnp.max(ps, axis=-1, keepdims=True)
    bucket = jnp.min(jnp.where(ps == best, lane, LANES), axis=-1, keepdims=True)
    onehot = (lane == bucket).astype(BF16)
    chosen = lax.dot_general(onehot, mb, (((1,), (1,)), ((), ())), preferred_element_type=F32) > 0.5
    picked = jnp.where(chosen, scores, 0.0)
    gates = picked / jnp.sum(picked, axis=-1, keepdims=True)
    e_lo = jnp.min(jnp.where(chosen, lane, LANES), axis=-1, keepdims=True)
    e_hi = jnp.max(jnp.where(chosen, lane, -1), axis=-1, keepdims=True)
    w_lo = jnp.sum(jnp.where(lane == e_lo, gates, 0.0), axis=-1, keepdims=True)
    w_hi = jnp.sum(jnp.where(lane == e_hi, gates, 0.0), axis=-1, keepdims=True)
    out = jnp.where(lane == 0, e_lo.astype(F32), 0.0)
    out = jnp.where(lane == 1, e_hi.astype(F32), out)
    out = jnp.where(lane == 2, w_lo, out)
    return jnp.where(lane == 3, w_hi, out)


def _merge_kernel(x_ref, y0_ref, y1_ref, bonus_ref, gate_ref, o0_ref, o1_ref, pg_ref, mods_ref,
                  lng_ref, lnb_ref, gng_ref, wa_ref, wb_ref, wo_ref, nfg_ref, rw_ref, rb_ref,
                  xs_o, h_o, route_o):
    y = y0_ref[...] + y1_ref[...]
    mean = _head_sum64(y) * (1.0 / RWKV_HEAD)
    yc = y - mean
    var = _head_sum64(yc * yc) * (1.0 / RWKV_HEAD)
    ya = (yc * lax.rsqrt(var + RWKV_GN_EPS) * lng_ref[...] + lnb_ref[...] + bonus_ref[...]) * gate_ref[...]
    o = o0_ref[...] + o1_ref[...]
    parts = []
    for h in range(GLA_HEADS):
        o_h = o[:, h * GLA_DV:(h + 1) * GLA_DV]
        parts.append(o_h * lax.rsqrt(jnp.mean(o_h * o_h, axis=-1, keepdims=True) + NORM_EPS) * gng_ref[...])
    yb = jnp.concatenate(parts, axis=1) * _silu(pg_ref[:, 0:GLA_VDIM].astype(F32))
    ga = _sigmoid(pg_ref[:, GLA_VDIM:GLA_VDIM + D_MODEL].astype(F32))
    gb = _sigmoid(pg_ref[:, GLA_VDIM + D_MODEL:GATE_COLS].astype(F32))
    z = ga * _mm(ya, wa_ref[...]) + gb * _mm(yb, wb_ref[...])
    xs = x_ref[...] + mods_ref[0, 2:3, :] * _mm(z, wo_ref[...])
    xs_o[...] = xs
    h = _rms(xs, nfg_ref[...]) * (1.0 + mods_ref[0, 4:5, :]) + mods_ref[0, 3:4, :]
    h_o[...] = h
    route_o[...] = _route(h, rw_ref, rb_ref)


def _merge(x, y0, y1, bonus, gate, o0, o1, pg, mods, ln_g, ln_b, gn_g, wa, wb, wo, nf_g, rw, rb,
           *, tm, n_rows, n_lat_tiles, tiles_per_batch, n_batch, tps_ctx):
    d = D_MODEL
    mrow = _mod_row_map(n_lat_tiles, tiles_per_batch, n_batch)
    gm = _group_major(_scan_groups(n_batch), n_batch, n_lat_tiles, tiles_per_batch, tps_ctx)
    row = lambda c: pl.BlockSpec((tm, c), lambda i: (i, 0))
    yrow = pl.BlockSpec((tm, RWKV_DIM), lambda i: (gm(i), 0))
    full = lambda a: pl.BlockSpec(a.shape, lambda i: (0,) * a.ndim)
    consts = [ln_g.reshape(1, -1), ln_b.reshape(1, -1), gn_g.reshape(1, -1), wa, wb, wo, nf_g.reshape(1, -1), rw, rb]
    return pl.pallas_call(
        _merge_kernel,
        out_shape=[jax.ShapeDtypeStruct((n_rows, d), F32), jax.ShapeDtypeStruct((n_rows, d), F32),
                   jax.ShapeDtypeStruct((n_rows, LANES), F32)],
        grid=(n_rows // tm,),
        in_specs=[row(d), yrow, yrow, row(RWKV_DIM), row(RWKV_DIM), row(GLA_VDIM), row(GLA_VDIM),
                  row(GATE_COLS), pl.BlockSpec((1, 6, d), mrow)] + [full(a) for a in consts],
        out_specs=[row(d), row(d), row(LANES)],
        compiler_params=_cparams(("parallel",)),
        name="merge",
    )(x, y0, y1, bonus, gate, o0, o1, pg, mods, *consts)


DMA_ISSUE_UNROLL = 8
MOE_UP_CHUNK = 256


def _moe_kernel(te_ref, nu_ref, src_ref, src_next_ref, dst_prev_ref, dst_ref, w_ref, h_hbm, wg_ref, wu_ref, wd_ref,
                o_hbm, xbuf, obuf, act_s, sem_in, sem_out, sem_fill, *, n_real_rows):
    del te_ref
    tm = xbuf.shape[1]
    i = pl.program_id(0)
    n_used = nu_ref[0]
    slot = i % 2

    other = 1 - slot

    def gather_row(idx_ref, s, r):
        return pltpu.make_async_copy(h_hbm.at[pl.ds(idx_ref[0, 0, r], 1)], xbuf.at[s, pl.ds(r, 1)], sem_in.at[s])

    def scatter_row(idx_ref, s, r):
        return pltpu.make_async_copy(obuf.at[s, pl.ds(r, 1)], o_hbm.at[pl.ds(idx_ref[0, 0, r], 1)], sem_out.at[s])

    def wait_gather(s):
        pltpu.make_async_copy(h_hbm.at[pl.ds(0, tm)], xbuf.at[s], sem_in.at[s]).wait()

    def wait_scatter(s):
        pltpu.make_async_copy(obuf.at[s], o_hbm.at[pl.ds(0, tm)], sem_out.at[s]).wait()

    def issue_all(make_row):
        def body(r, carry):
            make_row(r).start()
            return carry
        lax.fori_loop(0, tm, body, 0, unroll=DMA_ISSUE_UNROLL)

    @pl.when(i < n_used)
    def _():
        @pl.when(i == 0)
        def _():
            obuf[1] = jnp.zeros(obuf.shape[1:], obuf.dtype)
            fills = [pltpu.make_async_copy(obuf.at[1], o_hbm.at[pl.ds(n_real_rows + k * tm, tm)], sem_fill)
                     for k in range((o_hbm.shape[0] - n_real_rows) // tm)]
            for f in fills:
                f.start()
            for f in fills:
                f.wait()
            issue_all(lambda r: gather_row(src_ref, 0, r))

        @pl.when(i >= 1)
        def _():
            wait_scatter(slot)

        wait_gather(slot)
        xb = xbuf[slot].astype(BF16)
        bounds = list(range(0, D_EXPERT, MOE_UP_CHUNK)) + [D_EXPERT]
        n_chunks = len(bounds) - 1
        for c in range(n_chunks):
            c0, c1 = bounds[c], bounds[c + 1]
            g = jnp.dot(xb, wg_ref[:, c0:c1], preferred_element_type=F32)
            u = jnp.dot(xb, wu_ref[:, c0:c1], preferred_element_type=F32)
            act_s[:, c0:c1] = (_silu(g) * u).astype(BF16)
            for r in range(c * tm // n_chunks, (c + 1) * tm // n_chunks):
                gather_row(src_next_ref, other, r).start()
                scatter_row(dst_prev_ref, other, r).start()

        obuf[slot] = jnp.dot(act_s[...], wd_ref[...], preferred_element_type=F32) * w_ref[...][:, 0:1]

        @pl.when(i == n_used - 1)
        def _():
            issue_all(lambda r: scatter_row(dst_ref, slot, r))
            wait_scatter(other)
            wait_scatter(slot)
            wait_gather(other)


def _moe(h, route, wg, wu, wd, *, tm):
    t, d = h.shape
    i32 = jnp.int32
    ids = jnp.concatenate([route[:, 0], route[:, 1]]).astype(i32)
    wts = jnp.concatenate([route[:, 2], route[:, 3]])
    n_tiles = (2 * t) // tm + N_EXPERTS
    n_spill = (N_EXPERTS + 1) * tm
    _, order, wts_s = lax.sort((ids, jnp.arange(2 * t, dtype=i32), wts), num_keys=1, is_stable=True)
    counts = jnp.sum((ids[:, None] == jnp.arange(N_EXPERTS, dtype=i32)[None, :]).astype(i32), axis=0)
    starts = jnp.cumsum(counts) - counts
    pcounts = ((counts + tm - 1) // tm) * tm
    pend = jnp.cumsum(pcounts)
    poff = pend - pcounts
    n_used = (pend[-1] // tm).astype(i32)
    tile = jnp.arange(n_tiles, dtype=i32)
    te = jnp.sum((tile[:, None] * tm >= pend[None, :]).astype(i32), axis=1)
    te = jnp.where(tile < n_used, te, jnp.take(te, jnp.maximum(n_used - 1, 0))).astype(i32)
    rank0 = tile * tm - jnp.take(poff, te)
    n_valid = jnp.where(tile < n_used, jnp.clip(jnp.take(counts, te) - rank0, 0, tm), 0)
    base = jnp.clip(jnp.take(starts, te) + rank0, 0, 2 * t)
    take_tile = lambda arr: jax.vmap(lambda b: lax.dynamic_slice(arr, (b,), (tm,)))(base)
    a = take_tile(jnp.concatenate([order, jnp.zeros((tm,), i32)]))
    w_tile = take_tile(jnp.concatenate([wts_s, jnp.zeros((tm,), F32)]))
    valid = jnp.arange(tm, dtype=i32)[None, :] < n_valid[:, None]
    src = jnp.where(valid, a % t, 0).astype(i32)
    pad = (~valid) & (tile < n_used)[:, None]
    spill = 2 * t + tm + jnp.minimum(jnp.cumsum(pad.reshape(-1).astype(i32)) - 1, n_spill - tm - 1).reshape(n_tiles, tm)
    dst = jnp.where(valid, a, spill).astype(i32)
    dst = jnp.concatenate([2 * t + jnp.arange(tm, dtype=i32)[None, :], dst], axis=0)
    wrow = jnp.broadcast_to(jnp.where(valid, w_tile, 0.0).reshape(-1, 1), (n_tiles * tm, LANES))

    smem = functools.partial(pl.BlockSpec, (1, 1, tm), memory_space=pltpu.SMEM)
    src3, dst3 = src.reshape(n_tiles, 1, tm), dst.reshape(n_tiles + 1, 1, tm)
    return pl.pallas_call(
        functools.partial(_moe_kernel, n_real_rows=2 * t),
        out_shape=jax.ShapeDtypeStruct((2 * t + n_spill, d), F32),
        grid_spec=pltpu.PrefetchScalarGridSpec(
            num_scalar_prefetch=2,
            grid=(n_tiles,),
            in_specs=[smem(lambda i, *_: (i, 0, 0)),
                      smem(lambda i, *_: (jnp.minimum(i + 1, n_tiles - 1), 0, 0)),
                      smem(lambda i, *_: (i, 0, 0)),
                      smem(lambda i, *_: (i + 1, 0, 0)),
                      pl.BlockSpec((tm, LANES), lambda i, *_: (i, 0)),
                      pl.BlockSpec(memory_space=pl.ANY),
                      pl.BlockSpec((None, d, D_EXPERT), lambda i, te_r, *_: (te_r[i], 0, 0)),
                      pl.BlockSpec((None, d, D_EXPERT), lambda i, te_r, *_: (te_r[i], 0, 0)),
                      pl.BlockSpec((None, D_EXPERT, d), lambda i, te_r, *_: (te_r[i], 0, 0))],
            out_specs=pl.BlockSpec(memory_space=pl.ANY),
            scratch_shapes=[pltpu.VMEM((2, tm, d), F32), pltpu.VMEM((2, tm, d), F32),
                            pltpu.VMEM((tm, D_EXPERT), BF16),
                            pltpu.SemaphoreType.DMA((2,)), pltpu.SemaphoreType.DMA((2,)), pltpu.SemaphoreType.DMA]),
        compiler_params=_cparams(("arbitrary",)),
        name="moe_ffn",
    )(te, n_used.reshape(1), src3, src3, dst3, dst3, wrow, h, wg, wu, wd)


def _pad_cols(a, n):
    return jnp.pad(a, [(0, 0)] * (a.ndim - 1) + [(0, n - a.shape[-1])])


def _pad_rows(a, n):
    return jnp.pad(a, [(0, 0)] * (a.ndim - 2) + [(0, n - a.shape[-2]), (0, 0)])


def _split_in_proj(w_in, mu):
    s = (RWKV_DIM, 2 * RWKV_DIM, 3 * RWKV_DIM, 3 * RWKV_DIM + DECAY_LORA, 3 * RWKV_DIM + DECAY_LORA + ICLR_LORA)

    def rw_layout(a):
        return jnp.concatenate([a[..., :s[2]], _pad_cols(a[..., s[2]:s[3]], LORA_PAD),
                                _pad_cols(a[..., s[3]:s[4]], LORA_PAD), a[..., s[4]:RWKV_IN]], axis=-1)

    w_rw = rw_layout(w_in)
    w_gla = _pad_cols(w_in[:, RWKV_IN:GLA_GATE_OFF], GLA_COLS)
    w_gate = w_in[:, GLA_GATE_OFF:]
    return w_rw.astype(BF16), w_gla.astype(BF16), w_gate.astype(BF16), rw_layout(mu)


def kernel(x, c, ctx, c_ctx, w_ada, b_ada, norm_mix_g, norm_ffn_g, w_in, rwkv_mu, rwkv_w0, rwkv_w2, rwkv_a0,
           rwkv_a2, rwkv_g2, rwkv_k_k, rwkv_k_a, rwkv_r_k, rwkv_ln_g, rwkv_ln_b, gla_conv, gla_alpha_w2,
           gla_alpha_b, gla_norm_g, w_branch_a, w_branch_b, w_out, router_w, router_b, exp_w_gate, exp_w_up,
           exp_w_down, final_norm_g):
    n_batch, seq, d = x.shape
    ctx_len = ctx.shape[1]
    assert d == D_MODEL and seq % GRID_W == 0 and n_batch < MOD_ROWS
    grid_rows = seq // GRID_W
    tm = min(256, ctx_len)
    assert seq % tm == 0 and ctx_len % tm == 0 and tm % CHUNK == 0
    t_lat, t_ctx = n_batch * seq, n_batch * ctx_len
    t_all = t_lat + t_ctx
    tiles = dict(tm=tm, n_lat_tiles=t_lat // tm)
    seq_tiles = dict(tps_lat=seq // tm, tps_ctx=ctx_len // tm)
    mod_tiles = dict(tiles_per_batch=seq // tm, n_batch=n_batch)

    xs = jnp.concatenate([x.reshape(t_lat, d), ctx.reshape(t_ctx, d)], axis=0)
    c_all = jnp.zeros((MOD_ROWS, d), F32).at[:n_batch].set(c).at[n_batch].set(c_ctx)
    mods = _adaln(c_all, w_ada, b_ada).reshape(DEPTH, MOD_ROWS, 6, d)
    rw_pad = _pad_cols(router_w, LANES)
    rb_pad = _pad_cols(router_b.reshape(1, -1), LANES)

    moe_out = None
    for l in range(DEPTH):
        last = l == DEPTH - 1
        if moe_out is None:
            (h,) = _prenorm(xs, None, None, mods[l], norm_mix_g[l], n_rows=t_all, **tiles, **mod_tiles)
        else:
            xs, h = _prenorm(xs, moe_out, mods[l - 1], mods[l], norm_mix_g[l], n_rows=t_all, **tiles, **mod_tiles)
        w_rw, w_gla, w_gate, mu = _split_in_proj(w_in[l], rwkv_mu[l])
        p_rw = _matmul(h, w_rw, F32, tn=RW_COLS // 7)
        p_gla = _matmul(h, w_gla, F32, tn=GLA_COLS)
        p_gate = _matmul(h, w_gate, BF16, tn=GATE_COLS // 5)

        r, v, kap, lw, bb, kd, bonus, gate = _rwkv_prep(
            p_rw, mu, rwkv_w0[l], _pad_rows(rwkv_w2[l], LORA_PAD), rwkv_a0[l], _pad_rows(rwkv_a2[l], LORA_PAD),
            rwkv_g2[l], rwkv_k_k[l].reshape(1, -1), rwkv_k_a[l].reshape(1, -1), rwkv_r_k[l].reshape(1, -1),
            n_batch=n_batch, **tiles, **seq_tiles)
        y0, y1 = _rwkv_scan(r, v, kap, lw, bb, kd, n_batch=n_batch, nc_lat=seq // CHUNK, nc_ctx=ctx_len // CHUNK)

        u_lat = p_gla[:t_lat].reshape(n_batch, grid_rows, GRID_W, GLA_COLS).transpose(0, 2, 1, 3)
        u = jnp.concatenate([u_lat.reshape(t_lat, GLA_COLS), p_gla[t_lat:]], axis=0)
        conv_w = _pad_cols(gla_conv[l], GLA_QKV)
        outs = []
        for dr in range(2):
            o = _gla_scan(u, conv_w, _pad_rows(gla_alpha_w2[l, dr], LANES), gla_alpha_b[l, dr].reshape(1, -1),
                          rev=dr == 1, tm=tm, n_batch=n_batch, nb_lat=seq // tm, nb_ctx=ctx_len // tm)
            o_lat = o[:t_lat].reshape(n_batch, GRID_W, grid_rows, GLA_VDIM).transpose(0, 2, 1, 3)
            outs.append(jnp.concatenate([o_lat.reshape(t_lat, GLA_VDIM), o[t_lat:]], axis=0))

        n_rows = t_lat if last else t_all
        xs, hf, route = _merge(
            xs, y0, y1, bonus, gate, outs[0], outs[1], p_gate, mods[l], rwkv_ln_g[l], rwkv_ln_b[l], gla_norm_g[l],
            w_branch_a[l].astype(BF16), w_branch_b[l].astype(BF16), w_out[l].astype(BF16), norm_ffn_g[l],
            rw_pad, rb_pad, n_rows=n_rows, tps_ctx=ctx_len // tm, **tiles, **mod_tiles)
        moe_out = _moe(hf, route, exp_w_gate[l].astype(BF16), exp_w_up[l].astype(BF16), exp_w_down[l].astype(BF16),
                       tm=tm)

    (out,) = _prenorm(xs, moe_out, mods[DEPTH - 1], mods[DEPTH - 1], final_norm_g, n_rows=t_lat, final=True,
                      **tiles, **mod_tiles)
    return out.reshape(n_batch, seq, d)
```

```python
import functools

import jax
import jax.numpy as jnp
from jax import lax
from jax.experimental import pallas as pl
from jax.experimental.pallas import tpu as pltpu

F32 = jnp.float32
BF16 = jnp.bfloat16

D_MODEL = 2048
DEPTH = 2
GRID_W = 64
NORM_EPS = 1e-6
RWKV_DIM = 1024
RWKV_HEAD = 64
RWKV_HEADS = 16
DECAY_LORA = 96
ICLR_LORA = 96
GATE_LORA = 256
RWKV_GN_EPS = 64e-5
GLA_HEADS = 4
GLA_KDIM = 512
GLA_VDIM = 1024
GLA_DK = 128
GLA_DV = 256
GLA_GATE_LORA = 16
GLA_TAU = 16.0
GLA_QKV = 2048
N_EXPERTS = 16
EXPERTS_PER_GROUP = 4
N_PAIRS = 24
D_EXPERT = 1408
RWKV_IN = 3520
GLA_SEQ_IN = 2064
GLA_GATE_OFF = RWKV_IN + GLA_SEQ_IN
BR_GATE_OFF = GLA_GATE_OFF + GLA_VDIM

LANES = 128
CHUNK = 64
MOD_ROWS = 16
LORA_PAD = 128
RW_XW = 3 * RWKV_DIM
RW_XA = RW_XW + LORA_PAD
RW_XG = RW_XA + LORA_PAD
RW_COLS = RW_XG + GATE_LORA
GLA_COLS = GLA_QKV + LANES
GATE_COLS = GLA_VDIM + 2 * D_MODEL
VMEM_LIMIT = 56 * 1024 * 1024


def _cparams(sem, vmem=VMEM_LIMIT):
    return pltpu.CompilerParams(dimension_semantics=sem, vmem_limit_bytes=vmem)


def _pick_tile(n, cands):
    for c in cands:
        if n % c == 0:
            return c
    raise ValueError(f"no tile for {n}")


def _mm(a, b):
    return jnp.dot(a.astype(BF16), b.astype(BF16), preferred_element_type=F32)


def _mm_nt(a, b):
    return lax.dot_general(a.astype(BF16), b.astype(BF16), (((1,), (1,)), ((), ())), preferred_element_type=F32)


def _mm_tn(a, b):
    return lax.dot_general(a.astype(BF16), b.astype(BF16), (((0,), (0,)), ((), ())), preferred_element_type=F32)


def _split2(a):
    hi = a.astype(BF16)
    lo = (a - hi.astype(F32)).astype(BF16)
    return hi, lo


def _split3(a):
    hi = a.astype(BF16)
    r1 = a - hi.astype(F32)
    mid = r1.astype(BF16)
    lo = (r1 - mid.astype(F32)).astype(BF16)
    return hi, mid, lo


def _mm_x3(a, b):
    ah, al = _split2(a)
    bh, bl = _split2(b)
    d = functools.partial(jnp.dot, preferred_element_type=F32)
    return d(ah, bh) + (d(al, bh) + d(ah, bl))


def _mm_mask(m01, x):
    h, m, l = _split3(x)
    d = functools.partial(jnp.dot, preferred_element_type=F32)
    return d(m01, h) + (d(m01, m) + d(m01, l))


def _sigmoid(x):
    return 1.0 / (1.0 + jnp.exp(-x))


def _silu(x):
    return x * _sigmoid(x)


def _softplus(z):
    return jnp.maximum(z, 0.0) + jnp.log1p(jnp.exp(-jnp.abs(z)))


def _head_sum64(x):
    ri = lax.broadcasted_iota(jnp.int32, (256, 256), 0) // RWKV_HEAD
    ci = lax.broadcasted_iota(jnp.int32, (256, 256), 1) // RWKV_HEAD
    ones = (ri == ci).astype(BF16)
    hi, lo = _split2(x)
    d = functools.partial(jnp.dot, preferred_element_type=F32)
    outs = [d(hi[:, g * 256:(g + 1) * 256], ones) + d(lo[:, g * 256:(g + 1) * 256], ones)
            for g in range(x.shape[1] // 256)]
    return jnp.concatenate(outs, axis=1)


def _adaln_kernel(c_ref, w_ref, b_ref, o_ref):
    c = c_ref[...]
    o_ref[...] = _mm_x3(_silu(c), w_ref[...]) + b_ref[...]


def _adaln(c_all, w_ada, b_ada):
    nl, d, n = w_ada.shape
    tn = 1024
    return pl.pallas_call(
        _adaln_kernel,
        out_shape=jax.ShapeDtypeStruct((nl, MOD_ROWS, n), F32),
        grid=(nl, n // tn),
        in_specs=[pl.BlockSpec((MOD_ROWS, d), lambda l, j: (0, 0)),
                  pl.BlockSpec((None, d, tn), lambda l, j: (l, 0, j)),
                  pl.BlockSpec((None, 1, tn), lambda l, j: (l, 0, j))],
        out_specs=pl.BlockSpec((None, MOD_ROWS, tn), lambda l, j: (l, 0, j)),
        compiler_params=_cparams(("parallel", "parallel")),
        name="adaln",
    )(c_all, w_ada, b_ada.reshape(nl, 1, n))


def _rms(x, g):
    return x * lax.rsqrt(jnp.mean(x * x, axis=-1, keepdims=True) + NORM_EPS) * g


def _prenorm_kernel(*refs, has_moe, final):
    if has_moe:
        x_ref, o0_ref, o1_ref, mprev_ref, mcur_ref, g_ref = refs[:6]
        outs = refs[6:]
        x = x_ref[...] + mprev_ref[0, 5:6, :] * (o0_ref[...] + o1_ref[...])
    else:
        x_ref, mcur_ref, g_ref = refs[:3]
        outs = refs[3:]
        x = x_ref[...]
    y = _rms(x, g_ref[...])
    if final:
        outs[0][...] = y
        return
    h = y * (1.0 + mcur_ref[0, 1:2, :]) + mcur_ref[0, 0:1, :]
    if has_moe:
        outs[0][...] = x
        outs[1][...] = h.astype(outs[1].dtype)
    else:
        outs[0][...] = h.astype(outs[0].dtype)


def _mod_row_map(n_lat_tiles, tiles_per_batch, n_batch):
    return lambda i: (jnp.where(i < n_lat_tiles, i // tiles_per_batch, n_batch), 0, 0)


def _prenorm(x, moe_out, mods_prev, mods_cur, g, *, tm, n_lat_tiles, tiles_per_batch, n_batch, n_rows, final=False):
    d = x.shape[1]
    has_moe = moe_out is not None
    mrow = _mod_row_map(n_lat_tiles, tiles_per_batch, n_batch)
    row = pl.BlockSpec((tm, d), lambda i: (i, 0))
    mspec = pl.BlockSpec((1, 6, d), mrow)
    gspec = pl.BlockSpec((1, d), lambda i: (0, 0))
    ins, specs = [x], [row]
    if has_moe:
        slot1 = n_rows // tm
        ins += [moe_out, moe_out, mods_prev]
        specs += [row, pl.BlockSpec((tm, d), lambda i: (slot1 + i, 0)), mspec]
    ins += [mods_cur, g.reshape(1, d)]
    specs += [mspec, gspec]
    if final:
        out_shape = [jax.ShapeDtypeStruct((n_rows, d), F32)]
    elif has_moe:
        out_shape = [jax.ShapeDtypeStruct((n_rows, d), F32), jax.ShapeDtypeStruct((n_rows, d), BF16)]
    else:
        out_shape = [jax.ShapeDtypeStruct((n_rows, d), BF16)]
    return pl.pallas_call(
        functools.partial(_prenorm_kernel, has_moe=has_moe, final=final),
        out_shape=out_shape,
        grid=(n_rows // tm,),
        in_specs=specs,
        out_specs=[row] * len(out_shape),
        compiler_params=_cparams(("parallel",)),
        name="prenorm_final" if final else "prenorm",
    )(*ins)


def _matmul_kernel(a_ref, w_ref, o_ref):
    o_ref[...] = jnp.dot(a_ref[...], w_ref[...], preferred_element_type=F32).astype(o_ref.dtype)


def _matmul(a, w, out_dtype, *, tn, row0=0, n_rows=None):
    k = a.shape[1]
    n = w.shape[1]
    m = a.shape[0] if n_rows is None else n_rows
    tm = next(c for c in (1024, 512, 256, 128, 64) if m % c == 0 and row0 % c == 0)
    i0 = row0 // tm
    return pl.pallas_call(
        _matmul_kernel,
        out_shape=jax.ShapeDtypeStruct((m, n), out_dtype),
        grid=(m // tm, n // tn),
        in_specs=[pl.BlockSpec((tm, k), lambda i, j: (i + i0, 0)),
                  pl.BlockSpec((k, tn), lambda i, j: (0, j))],
        out_specs=pl.BlockSpec((tm, tn), lambda i, j: (i, j)),
        compiler_params=_cparams(("parallel", "parallel")),
        name="in_proj",
    )(a, w)


def _seq_flags(i, n_lat_tiles, tps_lat, tps_ctx):
    is_lat = i < n_lat_tiles
    j = jnp.where(is_lat, i % tps_lat, (i - n_lat_tiles) % tps_ctx)
    return j == 0, j == jnp.where(is_lat, tps_lat - 1, tps_ctx - 1)


def _neighbours(ref, prev_ref, next_ref, c0, c1, first, last):
    tm = ref.shape[0]
    p = ref[:, c0:c1]
    rows = lax.broadcasted_iota(jnp.int32, (tm, 1), 0)
    pr = jnp.where(first, 0.0, prev_ref[7:8, c0:c1])
    nx = jnp.where(last, 0.0, next_ref[0:1, c0:c1])
    prev = jnp.where(rows == 0, pr, pltpu.roll(p, 1, 0))
    nxt = jnp.where(rows == tm - 1, nx, pltpu.roll(p, tm - 1, 0))
    return p, prev, nxt


def _halo_specs(tm, cols, n_rows):
    r8 = tm // 8
    last8 = n_rows // 8 - 1
    return [pl.BlockSpec((tm, cols), lambda i: (i, 0)),
            pl.BlockSpec((8, cols), lambda i: (jnp.maximum(i * r8 - 1, 0), 0)),
            pl.BlockSpec((8, cols), lambda i: (jnp.minimum((i + 1) * r8, last8), 0))]


def _rwkv_prep_kernel(p_ref, pp_ref, pn_ref, mu_ref, w0_ref, w2_ref, a0_ref, a2_ref, g2_ref, kk_ref, ka_ref, rk_ref,
                      r_o, v_o, kap_o, lw_o, b_o, kd_o, bonus_o, gate_o, *, n_lat_tiles, tps_lat, tps_ctx):
    first, last = _seq_flags(pl.program_id(0), n_lat_tiles, tps_lat, tps_ctx)

    def shifted(c0, c1):
        p, prev, nxt = _neighbours(p_ref, pp_ref, pn_ref, c0, c1, first, last)
        return p + mu_ref[0:1, c0:c1] * (prev - p) + mu_ref[1:2, c0:c1] * (nxt - p)

    r = shifted(0, RWKV_DIM)
    k = shifted(RWKV_DIM, 2 * RWKV_DIM)
    v = shifted(2 * RWKV_DIM, 3 * RWKV_DIM)
    kk = k * kk_ref[...]
    kap = kk / jnp.maximum(jnp.sqrt(_head_sum64(kk * kk)), 1e-12)
    r_o[...] = r.astype(r_o.dtype)
    v_o[...] = v.astype(v_o.dtype)
    kap_o[...] = kap.astype(kap_o.dtype)
    bonus_o[...] = _head_sum64(r * k * rk_ref[...]) * v
    txw = jnp.tanh(shifted(RW_XW, RW_XA))
    xa = shifted(RW_XA, RW_XG)
    for d in range(2):
        w_log = -_softplus(-(w0_ref[d:d + 1, :] + _mm_x3(txw, w2_ref[d]))) - 0.5
        lw_o[d] = -jnp.exp(w_log)
        a = _sigmoid(a0_ref[d:d + 1, :] + _mm_x3(xa, a2_ref[d]))
        kd_o[d] = (k * (1.0 + (a - 1.0) * ka_ref[...])).astype(kd_o.dtype)
        b_o[d] = (kap * a).astype(b_o.dtype)
    gate_o[...] = _mm_x3(_sigmoid(shifted(RW_XG, RW_COLS)), g2_ref[...])


def _rwkv_prep(p_rw, mu, w0, w2, a0, a2, g2, k_k, k_a, r_k, *, tm, n_batch, n_lat_tiles, tps_lat, tps_ctx):
    n = p_rw.shape[0]
    gm = _group_major(_scan_groups(n_batch), n_batch, n_lat_tiles, tps_lat, tps_ctx)
    full = lambda a: pl.BlockSpec(a.shape, lambda i: (0,) * a.ndim)
    tok = pl.BlockSpec((tm, RWKV_DIM), lambda i: (i, 0))
    row = pl.BlockSpec((tm, RWKV_DIM), lambda i: (gm(i), 0))
    row2 = pl.BlockSpec((2, tm, RWKV_DIM), lambda i: (0, gm(i), 0))
    one = lambda dt: jax.ShapeDtypeStruct((n, RWKV_DIM), dt)
    two = lambda dt: jax.ShapeDtypeStruct((2, n, RWKV_DIM), dt)
    params = [mu, w0, w2, a0, a2, g2, k_k, k_a, r_k]
    return pl.pallas_call(
        functools.partial(_rwkv_prep_kernel, n_lat_tiles=n_lat_tiles, tps_lat=tps_lat, tps_ctx=tps_ctx),
        out_shape=[one(BF16), one(BF16), one(BF16), two(F32), two(BF16), two(BF16), one(F32), one(F32)],
        grid=(n // tm,),
        in_specs=_halo_specs(tm, RW_COLS, n) + [full(a) for a in params],
        out_specs=[row, row, row, row2, row2, row2, tok, tok],
        compiler_params=_cparams(("parallel",)),
        name="rwkv_prep",
    )(p_rw, p_rw, p_rw, *params)


def _order_masks(n, rev):
    ri = lax.broadcasted_iota(jnp.int32, (n, n), 0)
    ci = lax.broadcasted_iota(jnp.int32, (n, n), 1)
    if rev:
        return ci >= ri, ci > ri, ri == ci
    return ci <= ri, ci < ri, ri == ci


HEADS_PER_GROUP = 4
GROUP = HEADS_PER_GROUP * RWKV_HEAD


def _head_block_mask():
    ri = lax.broadcasted_iota(jnp.int32, (GROUP, GROUP), 0) // RWKV_HEAD
    ci = lax.broadcasted_iota(jnp.int32, (GROUP, GROUP), 1) // RWKV_HEAD
    return ri == ci


def _block_diag(y_g, bdm):
    return jnp.where(bdm, jnp.concatenate([y_g] * HEADS_PER_GROUP, axis=0), jnp.zeros((), y_g.dtype))


def _head_prod(x, y, bdm, nt=False):
    xb, yb = x.astype(BF16), y.astype(BF16)
    dims = (((1,), (1,)), ((), ())) if nt else (((1,), (0,)), ((), ()))
    outs = [lax.dot_general(xb[:, g:g + GROUP], _block_diag(yb[:, g:g + GROUP], bdm), dims,
                            preferred_element_type=F32) for g in range(0, x.shape[1], GROUP)]
    return jnp.concatenate(outs, axis=1)


def _head_prod_tn(a, b, bdm):
    ab, bb = a.astype(BF16), b.astype(BF16)
    outs = []
    for g in range(0, a.shape[1], GROUP):
        p = lax.dot_general(ab[:, g:g + GROUP], bb[:, g:g + GROUP], (((0,), (0,)), ((), ())),
                            preferred_element_type=F32)
        p = jnp.where(bdm, p, 0.0)
        outs.append((p[0:64] + p[64:128]) + (p[128:192] + p[192:256]))
    return jnp.concatenate(outs, axis=1)


def _rwkv_chunk(r_ref, v_ref, kap_ref, lw_ref, b_ref, kd_ref, y_ref, s_ref, rev):
    row = lax.broadcasted_iota(jnp.int32, (CHUNK, RWKV_DIM), 0)
    col = lax.broadcasted_iota(jnp.int32, (CHUNK, RWKV_DIM), 1) % RWKV_HEAD
    incl, strict = (col >= row, col > row) if rev else (col <= row, col < row)
    eye_f = (col == row).astype(F32)
    bdm = _head_block_mask()
    lw = lw_ref[...]
    gam = _mm_mask(_order_masks(CHUNK, rev)[0].astype(BF16), lw)
    e_tot = jnp.exp(jnp.sum(lw, axis=0, keepdims=True))
    e_neg = jnp.exp(-gam)
    rt = r_ref[...] * jnp.exp(gam)
    kt = kap_ref[...] * jnp.exp(gam - lw)
    bt = b_ref[...] * e_neg
    kdt = kd_ref[...] * e_neg
    v = v_ref[...]
    kr = jnp.concatenate([kt, rt], axis=0)
    g_b = _head_prod(kr, bt, bdm, nt=True)
    g_k = _head_prod(kr, kdt, bdm, nt=True)
    n_ab = jnp.where(strict, g_b[:CHUNK], 0.0)
    a_kv = jnp.where(strict, g_k[:CHUNK], 0.0)
    m_b = jnp.where(incl, g_b[CHUNK:], 0.0)
    m_kd = jnp.where(incl, g_k[CHUNK:], 0.0)
    t_inv = eye_f - n_ab
    pw = _head_prod(n_ab, n_ab, bdm)
    for _ in range(4):
        both = _head_prod(jnp.concatenate([t_inv, pw], axis=0), pw, bdm)
        t_inv = t_inv + both[:CHUNK]
        pw = both[CHUNK:]
    t_inv = t_inv + _head_prod(t_inv, pw, bdm)
    av = _head_prod(jnp.concatenate([a_kv, m_kd], axis=0), v, bdm)
    s = s_ref[...]
    ks = _head_prod(kr, s, bdm, nt=True)
    u = _head_prod(t_inv, ks[:CHUNK] + av[:CHUNK], bdm)
    y_ref[...] = ks[CHUNK:] + av[CHUNK:] - _head_prod(m_b, u, bdm)
    upd = _head_prod_tn(jnp.concatenate([v, u], axis=0), jnp.concatenate([kdt, -bt], axis=0), bdm)
    s_ref[...] = (s + upd) * e_tot


def _rwkv_scan_kernel(rf_ref, vf_ref, kapf_ref, lwf_ref, bf_ref, kdf_ref, rr_ref, vr_ref, kapr_ref, lwr_ref, br_ref,
                      kdr_ref, yf_ref, yr_ref, sf_ref, sr_ref):
    @pl.when(pl.program_id(1) == 0)
    def _():
        sf_ref[...] = jnp.zeros_like(sf_ref)
        sr_ref[...] = jnp.zeros_like(sr_ref)

    for p in range(rf_ref.shape[0]):
        _rwkv_chunk(rf_ref.at[p], vf_ref.at[p], kapf_ref.at[p], lwf_ref.at[p], bf_ref.at[p], kdf_ref.at[p],
                    yf_ref.at[p], sf_ref.at[p], False)
        _rwkv_chunk(rr_ref.at[p], vr_ref.at[p], kapr_ref.at[p], lwr_ref.at[p], br_ref.at[p], kdr_ref.at[p],
                    yr_ref.at[p], sr_ref.at[p], True)


def _scan_block_map(rev, n_lat_blocks, nb_lat, nb_ctx):
    def f(b, s):
        c_ctx = (nb_ctx - 1 - s) if rev else s
        c_lat = (nb_lat - 1 - (s - nb_ctx)) if rev else (s - nb_ctx)
        return jnp.where(s < nb_ctx, n_lat_blocks + b * nb_ctx + c_ctx, b * nb_lat + c_lat)
    return f


def _scan_groups(n_batch):
    return 2 if n_batch % 2 == 0 else 1


def _group_major(n_groups, n_batch, n_lat_tiles, tps_lat, tps_ctx):
    per = n_batch // n_groups
    lat, ctx = per * tps_lat, per * tps_ctx

    def f(i):
        is_lat = i < n_lat_tiles
        b = jnp.where(is_lat, i // tps_lat, (i - n_lat_tiles) // tps_ctx)
        j = jnp.where(is_lat, i % tps_lat, (i - n_lat_tiles) % tps_ctx)
        return (b // per) * (lat + ctx) + jnp.where(is_lat, (b % per) * tps_lat + j, lat + (b % per) * tps_ctx + j)
    return f


def _rwkv_scan(r, v, kap, lw, bb, kd, *, n_batch, nc_lat, nc_ctx):
    n = r.shape[0]
    ng = _scan_groups(n_batch)
    per = n_batch // ng
    g3 = lambda a: a.reshape(ng, n // ng, RWKV_DIM)
    g4 = lambda a: a.reshape(2, ng, n // ng, RWKV_DIM)
    specs = []
    for d, rev in enumerate((False, True)):
        blk = _scan_block_map(rev, per * nc_lat, nc_lat, nc_ctx)
        row = pl.BlockSpec((ng, CHUNK, RWKV_DIM), lambda b, s, blk=blk: (0, blk(b, s), 0))
        rowd = pl.BlockSpec((None, ng, CHUNK, RWKV_DIM), lambda b, s, blk=blk, d=d: (d, 0, blk(b, s), 0))
        specs.append((row, rowd))
    (rowf, rowdf), (rowr, rowdr) = specs
    out = jax.ShapeDtypeStruct((ng, n // ng, RWKV_DIM), F32)
    state = pltpu.VMEM((ng, RWKV_HEAD, RWKV_DIM), F32)
    yf, yr = pl.pallas_call(
        _rwkv_scan_kernel,
        out_shape=[out, out],
        grid=(per, nc_ctx + nc_lat),
        in_specs=[rowf, rowf, rowf, rowdf, rowdf, rowdf, rowr, rowr, rowr, rowdr, rowdr, rowdr],
        out_specs=[rowf, rowr],
        scratch_shapes=[state, state],
        compiler_params=_cparams(("parallel", "arbitrary")),
        name="rwkv_scan",
    )(g3(r), g3(v), g3(kap), g4(lw), g4(bb), g4(kd), g3(r), g3(v), g3(kap), g4(lw), g4(bb), g4(kd))
    return yf.reshape(n, RWKV_DIM), yr.reshape(n, RWKV_DIM)


def _gla_block(load, prev_row, next_row, n_rows, cw_ref, aw_ref, ab_ref, st_ref, qkv_s, g_s, o_s, rev):
    rows = lax.broadcasted_iota(jnp.int32, (n_rows, 1), 0)
    for c0 in range(0, GLA_QKV, 512):
        c1 = c0 + 512
        p = load(c0, c1)
        prev = jnp.where(rows == 0, prev_row(c0, c1), pltpu.roll(p, 1, 0))
        nxt = jnp.where(rows == n_rows - 1, next_row(c0, c1), pltpu.roll(p, n_rows - 1, 0))
        act = _silu(cw_ref[0:1, c0:c1] * prev + cw_ref[1:2, c0:c1] * p + cw_ref[2:3, c0:c1] * nxt)
        if c0 == 0:
            act = act * (GLA_DK ** -0.5)
        qkv_s[0:n_rows, c0:c1] = act
    z = _mm_x3(load(GLA_QKV, GLA_COLS), aw_ref[...]) + ab_ref[...]
    g_s[0:n_rows, :] = -_softplus(-z) * (1.0 / GLA_TAU)

    incl, _, _ = _order_masks(CHUNK, rev)
    incl_b = incl.astype(BF16)
    n_chunks = n_rows // CHUNK
    for ch in (range(n_chunks - 1, -1, -1) if rev else range(n_chunks)):
        rs = slice(ch * CHUNK, (ch + 1) * CHUNK)
        g = g_s[rs, :]
        gc = _mm_mask(incl_b, g)
        gl = jnp.sum(g, axis=0, keepdims=True)
        q = qkv_s[rs, 0:GLA_KDIM]
        k = qkv_s[rs, GLA_KDIM:2 * GLA_KDIM]
        qg = q * jnp.exp(gc)
        kg = k * jnp.exp(-gc)
        ke = k * jnp.exp(gl - gc)
        e_l = jnp.exp(gl)
        for h in range(GLA_HEADS):
            ks = slice(h * GLA_DK, (h + 1) * GLA_DK)
            vs = slice(2 * GLA_KDIM + h * GLA_DV, 2 * GLA_KDIM + (h + 1) * GLA_DV)
            v_h = qkv_s[rs, vs]
            att = jnp.where(incl, _mm_nt(qg[:, ks], kg[:, ks]), 0.0)
            st = st_ref[h]
            o_s[rs, h * GLA_DV:(h + 1) * GLA_DV] = _mm(att, v_h) + _mm_nt(qg[:, ks], st)
            st_ref[h] = st * e_l[:, ks] + _mm_tn(v_h, ke[:, ks])


def _gla_kernel(uc_ref, ul_ref, ulp_ref, uln_ref, cw_ref, aw_ref, ab_ref, oc_ref, ol_ref, st_ref, qkv_s, g_s, o_s,
                *, rev, n_wg):
    s = pl.program_id(1)
    grid_rows, wg_cols = ul_ref.shape[0], ul_ref.shape[1]
    zero_row = lambda c0, c1: jnp.zeros((1, c1 - c0), F32)
    common = (cw_ref, aw_ref, ab_ref, st_ref, qkv_s, g_s, o_s, rev)

    @pl.when(s == 0)
    def _():
        st_ref[...] = jnp.zeros_like(st_ref)
        n = uc_ref.shape[0]
        _gla_block(lambda c0, c1: uc_ref[:, c0:c1], zero_row, zero_row, n, *common)
        oc_ref[...] = o_s[0:n, :]

    @pl.when(s > 0)
    def _():
        wg = (n_wg - s) if rev else (s - 1)
        n = grid_rows * wg_cols
        load = lambda c0, c1: jnp.concatenate([ul_ref[:, w, c0:c1] for w in range(wg_cols)], axis=0)
        hr = ulp_ref.shape[0]
        prev_row = lambda c0, c1: jnp.where(wg == 0, 0.0, ulp_ref[hr - 1:hr, 7, c0:c1])
        next_row = lambda c0, c1: jnp.where(wg == n_wg - 1, 0.0, uln_ref[0:1, 0, c0:c1])
        _gla_block(load, prev_row, next_row, n, *common)
        for w in range(wg_cols):
            ol_ref[:, w, :] = o_s[w * grid_rows:(w + 1) * grid_rows, :]


def _gla_scan(u_ctx, u_lat, conv_w, alpha_w, alpha_b, *, rev, n_batch):
    t_ctx, ch = u_ctx.shape
    lc = t_ctx // n_batch
    _, grid_rows, grid_w, _ = u_lat.shape
    wg_cols = min(grid_w, max(8, 256 // grid_rows))
    n_wg = grid_w // wg_cols
    assert (wg_cols * grid_rows) % CHUNK == 0 and lc % CHUNK == 0 and grid_w % wg_cols == 0 and wg_cols % 8 == 0
    hr = 8 if grid_rows % 8 == 0 else grid_rows
    n_rows = max(lc, wg_cols * grid_rows)

    def wg(s):
        return jnp.clip((n_wg - s) if rev else (s - 1), 0, n_wg - 1)

    full = lambda a: pl.BlockSpec(a.shape, lambda b, s: (0,) * a.ndim)
    lat = lambda c: pl.BlockSpec((None, grid_rows, wg_cols, c), lambda b, s: (b, 0, wg(s), 0))
    return pl.pallas_call(
        functools.partial(_gla_kernel, rev=rev, n_wg=n_wg),
        out_shape=[jax.ShapeDtypeStruct((t_ctx, GLA_VDIM), F32),
                   jax.ShapeDtypeStruct((n_batch, grid_rows, grid_w, GLA_VDIM), F32)],
        grid=(n_batch, 1 + n_wg),
        in_specs=[pl.BlockSpec((lc, ch), lambda b, s: (b, 0)),
                  lat(ch),
                  pl.BlockSpec((None, hr, 8, ch), lambda b, s: (
                      b, grid_rows // hr - 1, jnp.maximum(wg(s) * (wg_cols // 8) - 1, 0), 0)),
                  pl.BlockSpec((None, hr, 8, ch), lambda b, s: (
                      b, 0, jnp.minimum((wg(s) + 1) * (wg_cols // 8), grid_w // 8 - 1), 0)),
                  full(conv_w), full(alpha_w), full(alpha_b)],
        out_specs=[pl.BlockSpec((lc, GLA_VDIM), lambda b, s: (b, 0)), lat(GLA_VDIM)],
        scratch_shapes=[pltpu.VMEM((GLA_HEADS, GLA_DV, GLA_DK), F32),
                        pltpu.VMEM((n_rows, GLA_QKV), F32),
                        pltpu.VMEM((n_rows, GLA_KDIM), F32),
                        pltpu.VMEM((n_rows, GLA_VDIM), F32)],
        compiler_params=_cparams(("parallel", "arbitrary")),
        name="gla_scan_rev" if rev else "gla_scan_fwd",
    )(u_ctx, u_lat, u_lat, u_lat, conv_w, alpha_w, alpha_b)


def _pair_tables():
    ri = lax.broadcasted_iota(jnp.int32, (LANES, LANES), 0)
    ci = lax.broadcasted_iota(jnp.int32, (LANES, LANES), 1)
    grp = ci // 6
    pr = ci % 6
    lo = jnp.where(pr < 3, 0, jnp.where(pr < 5, 1, 2))
    hi = jnp.where(pr < 3, pr + 1, jnp.where(pr < 5, pr - 1, 3))
    member = ((ri == grp * EXPERTS_PER_GROUP + lo) | (ri == grp * EXPERTS_PER_GROUP + hi)) & (ci < N_PAIRS)
    return member


def _route(h, rw_ref, rb_ref):
    lane = lax.broadcasted_iota(jnp.int32, (h.shape[0], LANES), 1)
    scores = _sigmoid(_mm_x3(h, rw_ref[...]))
    sel = jnp.where(lane < N_EXPERTS, scores + rb_ref[...], 0.0)
    member = _pair_tables()
    s3 = _split3(sel)
    mb = member.astype(BF16)
    d = functools.partial(jnp.dot, preferred_element_type=F32)
    ps = d(s3[0], mb) + (d(s3[1], mb) + d(s3[2], mb))
    ps = jnp.where(lane < N_PAIRS, ps, -jnp.inf)
    best = jnp.max(ps, axis=-1, keepdims=True)
    bucket = jnp.min(jnp.where(ps == best, lane, LANES), axis=-1, keepdims=True)
    onehot = (lane == bucket).astype(BF16)
    chosen = lax.dot_general(onehot, mb, (((1,), (1,)), ((), ())), preferred_element_type=F32) > 0.5
    picked = jnp.where(chosen, scores, 0.0)
    gates = picked / jnp.sum(picked, axis=-1, keepdims=True)
    e_lo = jnp.min(jnp.where(chosen, lane, LANES), axis=-1, keepdims=True)
    e_hi = jnp.max(jnp.where(chosen, lane, -1), axis=-1, keepdims=True)
    w_lo = jnp.sum(jnp.where(lane == e_lo, gates, 0.0), axis=-1, keepdims=True)
    w_hi = jnp.sum(jnp.where(lane == e_hi, gates, 0.0), axis=-1, keepdims=True)
    out = jnp.where(lane == 0, e_lo.astype(F32), 0.0)
    out = jnp.where(lane == 1, e_hi.astype(F32), out)
    out = jnp.where(lane == 2, w_lo, out)
    return jnp.where(lane == 3, w_hi, out)


def _merge_kernel(x_ref, y0_ref, y1_ref, bonus_ref, gate_ref, ol0_ref, ol1_ref, oc0_ref, oc1_ref, pg_ref, mods_ref,
                  lng_ref, lnb_ref, gng_ref, wa_ref, wb_ref, wo_ref, nfg_ref, rw_ref, rb_ref,
                  xs_o, h_o, route_o, *, n_lat_tiles):
    y = y0_ref[...] + y1_ref[...]
    mean = _head_sum64(y) * (1.0 / RWKV_HEAD)
    yc = y - mean
    var = _head_sum64(yc * yc) * (1.0 / RWKV_HEAD)
    ya = (yc * lax.rsqrt(var + RWKV_GN_EPS) * lng_ref[...] + lnb_ref[...] + bonus_ref[...]) * gate_ref[...]
    o = jnp.where(pl.program_id(0) < n_lat_tiles, ol0_ref[...] + ol1_ref[...], oc0_ref[...] + oc1_ref[...])
    parts = []
    for h in range(GLA_HEADS):
        o_h = o[:, h * GLA_DV:(h + 1) * GLA_DV]
        parts.append(o_h * lax.rsqrt(jnp.mean(o_h * o_h, axis=-1, keepdims=True) + NORM_EPS) * gng_ref[...])
    yb = jnp.concatenate(parts, axis=1) * _silu(pg_ref[:, 0:GLA_VDIM].astype(F32))
    ga = _sigmoid(pg_ref[:, GLA_VDIM:GLA_VDIM + D_MODEL].astype(F32))
    gb = _sigmoid(pg_ref[:, GLA_VDIM + D_MODEL:GATE_COLS].astype(F32))
    z = ga * _mm(ya, wa_ref[...]) + gb * _mm(yb, wb_ref[...])
    xs = x_ref[...] + mods_ref[0, 2:3, :] * _mm(z, wo_ref[...])
    xs_o[...] = xs
    h = _rms(xs, nfg_ref[...]) * (1.0 + mods_ref[0, 4:5, :]) + mods_ref[0, 3:4, :]
    h_o[...] = h
    route_o[...] = _route(h, rw_ref, rb_ref)


def _merge(x, y0, y1, bonus, gate, o_lat, o_ctx, pg, mods, ln_g, ln_b, gn_g, wa, wb, wo, nf_g, rw, rb,
           *, tm, n_rows, n_lat_tiles, tiles_per_batch, n_batch, tps_ctx):
    d = D_MODEL
    mrow = _mod_row_map(n_lat_tiles, tiles_per_batch, n_batch)
    gm = _group_major(_scan_groups(n_batch), n_batch, n_lat_tiles, tiles_per_batch, tps_ctx)
    row = lambda c: pl.BlockSpec((tm, c), lambda i: (i, 0))
    yrow = pl.BlockSpec((tm, RWKV_DIM), lambda i: (gm(i), 0))
    lat = pl.BlockSpec((tm, GLA_VDIM), lambda i: (jnp.minimum(i, n_lat_tiles - 1), 0))
    ctx = pl.BlockSpec((tm, GLA_VDIM), lambda i: (jnp.maximum(i - n_lat_tiles, 0), 0))
    full = lambda a: pl.BlockSpec(a.shape, lambda i: (0,) * a.ndim)
    consts = [ln_g.reshape(1, -1), ln_b.reshape(1, -1), gn_g.reshape(1, -1), wa, wb, wo, nf_g.reshape(1, -1), rw, rb]
    return pl.pallas_call(
        functools.partial(_merge_kernel, n_lat_tiles=n_lat_tiles),
        out_shape=[jax.ShapeDtypeStruct((n_rows, d), F32), jax.ShapeDtypeStruct((n_rows, d), F32),
                   jax.ShapeDtypeStruct((n_rows, LANES), F32)],
        grid=(n_rows // tm,),
        in_specs=[row(d), yrow, yrow, row(RWKV_DIM), row(RWKV_DIM), lat, lat, ctx, ctx,
                  row(GATE_COLS), pl.BlockSpec((1, 6, d), mrow)] + [full(a) for a in consts],
        out_specs=[row(d), row(d), row(LANES)],
        compiler_params=_cparams(("parallel",)),
        name="merge",
    )(x, y0, y1, bonus, gate, o_lat[0], o_lat[1], o_ctx[0], o_ctx[1], pg, mods, *consts)


DMA_ISSUE_UNROLL = 8
MOE_UP_CHUNK = 256


def _moe_kernel(te_ref, nu_ref, src_ref, src_next_ref, dst_prev_ref, dst_ref, w_ref, h_hbm, wg_ref, wu_ref, wd_ref,
                o_hbm, xbuf, obuf, act_s, sem_in, sem_out, sem_fill, *, n_real_rows):
    del te_ref
    tm = xbuf.shape[1]
    i = pl.program_id(0)
    n_used = nu_ref[0]
    slot = i % 2

    other = 1 - slot

    def gather_row(idx_ref, s, r):
        return pltpu.make_async_copy(h_hbm.at[pl.ds(idx_ref[0, 0, r], 1)], xbuf.at[s, pl.ds(r, 1)], sem_in.at[s])

    def scatter_row(idx_ref, s, r):
        return pltpu.make_async_copy(obuf.at[s, pl.ds(r, 1)], o_hbm.at[pl.ds(idx_ref[0, 0, r], 1)], sem_out.at[s])

    def wait_gather(s):
        pltpu.make_async_copy(h_hbm.at[pl.ds(0, tm)], xbuf.at[s], sem_in.at[s]).wait()

    def wait_scatter(s):
        pltpu.make_async_copy(obuf.at[s], o_hbm.at[pl.ds(0, tm)], sem_out.at[s]).wait()

    def issue_all(make_row):
        def body(r, carry):
            make_row(r).start()
            return carry
        lax.fori_loop(0, tm, body, 0, unroll=DMA_ISSUE_UNROLL)

    @pl.when(i < n_used)
    def _():
        @pl.when(i == 0)
        def _():
            obuf[1] = jnp.zeros(obuf.shape[1:], obuf.dtype)
            fills = [pltpu.make_async_copy(obuf.at[1], o_hbm.at[pl.ds(n_real_rows + k * tm, tm)], sem_fill)
                     for k in range((o_hbm.shape[0] - n_real_rows) // tm)]
            for f in fills:
                f.start()
            for f in fills:
                f.wait()
            issue_all(lambda r: gather_row(src_ref, 0, r))

        @pl.when(i >= 1)
        def _():
            wait_scatter(slot)

        wait_gather(slot)
        xb = xbuf[slot].astype(BF16)
        bounds = list(range(0, D_EXPERT, MOE_UP_CHUNK)) + [D_EXPERT]
        n_chunks = len(bounds) - 1
        for c in range(n_chunks):
            c0, c1 = bounds[c], bounds[c + 1]
            g = jnp.dot(xb, wg_ref[:, c0:c1], preferred_element_type=F32)
            u = jnp.dot(xb, wu_ref[:, c0:c1], preferred_element_type=F32)
            act_s[:, c0:c1] = (_silu(g) * u).astype(BF16)
            for r in range(c * tm // n_chunks, (c + 1) * tm // n_chunks):
                gather_row(src_next_ref, other, r).start()
                scatter_row(dst_prev_ref, other, r).start()

        obuf[slot] = jnp.dot(act_s[...], wd_ref[...], preferred_element_type=F32) * w_ref[...][:, 0:1]

        @pl.when(i == n_used - 1)
        def _():
            issue_all(lambda r: scatter_row(dst_ref, slot, r))
            wait_scatter(other)
            wait_scatter(slot)
            wait_gather(other)


def _moe(h, route, wg, wu, wd, *, tm):
    t, d = h.shape
    i32 = jnp.int32
    ids = jnp.concatenate([route[:, 0], route[:, 1]]).astype(i32)
    wts = jnp.concatenate([route[:, 2], route[:, 3]])
    n_tiles = (2 * t) // tm + N_EXPERTS
    n_spill = (N_EXPERTS + 1) * tm
    _, order, wts_s = lax.sort((ids, jnp.arange(2 * t, dtype=i32), wts), num_keys=1, is_stable=True)
    counts = jnp.sum((ids[:, None] == jnp.arange(N_EXPERTS, dtype=i32)[None, :]).astype(i32), axis=0)
    starts = jnp.cumsum(counts) - counts
    pcounts = ((counts + tm - 1) // tm) * tm
    pend = jnp.cumsum(pcounts)
    poff = pend - pcounts
    n_used = (pend[-1] // tm).astype(i32)
    tile = jnp.arange(n_tiles, dtype=i32)
    te = jnp.sum((tile[:, None] * tm >= pend[None, :]).astype(i32), axis=1)
    te = jnp.where(tile < n_used, te, jnp.take(te, jnp.maximum(n_used - 1, 0))).astype(i32)
    rank0 = tile * tm - jnp.take(poff, te)
    n_valid = jnp.where(tile < n_used, jnp.clip(jnp.take(counts, te) - rank0, 0, tm), 0)
    n_pos = n_tiles * tm
    shift = poff - starts
    e_tile = te[:, None]

    def spread(arr):
        ext = jnp.concatenate([jnp.zeros((n_pos,), arr.dtype), arr, jnp.zeros((n_pos,), arr.dtype)])
        out = jnp.zeros((n_tiles, tm), arr.dtype)
        for e in range(N_EXPERTS):
            seg = lax.dynamic_slice(ext, (n_pos - shift[e],), (n_pos,)).reshape(n_tiles, tm)
            out = jnp.where(e_tile == e, seg, out)
        return out

    a = spread(order)
    w_tile = spread(wts_s)
    valid = jnp.arange(tm, dtype=i32)[None, :] < n_valid[:, None]
    src = jnp.where(valid, a % t, 0).astype(i32)
    pad = (~valid) & (tile < n_used)[:, None]
    spill = 2 * t + tm + jnp.minimum(jnp.cumsum(pad.reshape(-1).astype(i32)) - 1, n_spill - tm - 1).reshape(n_tiles, tm)
    dst = jnp.where(valid, a, spill).astype(i32)
    dst = jnp.concatenate([2 * t + jnp.arange(tm, dtype=i32)[None, :], dst], axis=0)
    wrow = jnp.broadcast_to(jnp.where(valid, w_tile, 0.0).reshape(-1, 1), (n_tiles * tm, LANES))

    smem = functools.partial(pl.BlockSpec, (1, 1, tm), memory_space=pltpu.SMEM)
    src3, dst3 = src.reshape(n_tiles, 1, tm), dst.reshape(n_tiles + 1, 1, tm)
    return pl.pallas_call(
        functools.partial(_moe_kernel, n_real_rows=2 * t),
        out_shape=jax.ShapeDtypeStruct((2 * t + n_spill, d), F32),
        grid_spec=pltpu.PrefetchScalarGridSpec(
            num_scalar_prefetch=2,
            grid=(n_tiles,),
            in_specs=[smem(lambda i, *_: (i, 0, 0)),
                      smem(lambda i, *_: (jnp.minimum(i + 1, n_tiles - 1), 0, 0)),
                      smem(lambda i, *_: (i, 0, 0)),
                      smem(lambda i, *_: (i + 1, 0, 0)),
                      pl.BlockSpec((tm, LANES), lambda i, *_: (i, 0)),
                      pl.BlockSpec(memory_space=pl.ANY),
                      pl.BlockSpec((None, d, D_EXPERT), lambda i, te_r, *_: (te_r[i], 0, 0)),
                      pl.BlockSpec((None, d, D_EXPERT), lambda i, te_r, *_: (te_r[i], 0, 0)),
                      pl.BlockSpec((None, D_EXPERT, d), lambda i, te_r, *_: (te_r[i], 0, 0))],
            out_specs=pl.BlockSpec(memory_space=pl.ANY),
            scratch_shapes=[pltpu.VMEM((2, tm, d), F32), pltpu.VMEM((2, tm, d), F32),
                            pltpu.VMEM((tm, D_EXPERT), BF16),
                            pltpu.SemaphoreType.DMA((2,)), pltpu.SemaphoreType.DMA((2,)), pltpu.SemaphoreType.DMA]),
        compiler_params=_cparams(("arbitrary",)),
        name="moe_ffn",
    )(te, n_used.reshape(1), src3, src3, dst3, dst3, wrow, h, wg, wu, wd)


def _pad_cols(a, n):
    return jnp.pad(a, [(0, 0)] * (a.ndim - 1) + [(0, n - a.shape[-1])])


def _pad_rows(a, n):
    return jnp.pad(a, [(0, 0)] * (a.ndim - 2) + [(0, n - a.shape[-2]), (0, 0)])


def _split_in_proj(w_in, mu):
    s = (RWKV_DIM, 2 * RWKV_DIM, 3 * RWKV_DIM, 3 * RWKV_DIM + DECAY_LORA, 3 * RWKV_DIM + DECAY_LORA + ICLR_LORA)

    def rw_layout(a):
        return jnp.concatenate([a[..., :s[2]], _pad_cols(a[..., s[2]:s[3]], LORA_PAD),
                                _pad_cols(a[..., s[3]:s[4]], LORA_PAD), a[..., s[4]:RWKV_IN]], axis=-1)

    w_in = w_in.astype(BF16)
    w_rw = rw_layout(w_in)
    w_gla = _pad_cols(w_in[:, RWKV_IN:GLA_GATE_OFF], GLA_COLS)
    w_gate = w_in[:, GLA_GATE_OFF:]
    return w_rw, w_gla, w_gate, rw_layout(mu)


def kernel(x, c, ctx, c_ctx, w_ada, b_ada, norm_mix_g, norm_ffn_g, w_in, rwkv_mu, rwkv_w0, rwkv_w2, rwkv_a0,
           rwkv_a2, rwkv_g2, rwkv_k_k, rwkv_k_a, rwkv_r_k, rwkv_ln_g, rwkv_ln_b, gla_conv, gla_alpha_w2,
           gla_alpha_b, gla_norm_g, w_branch_a, w_branch_b, w_out, router_w, router_b, exp_w_gate, exp_w_up,
           exp_w_down, final_norm_g):
    n_batch, seq, d = x.shape
    ctx_len = ctx.shape[1]
    assert d == D_MODEL and seq % GRID_W == 0 and n_batch < MOD_ROWS
    grid_rows = seq // GRID_W
    tm = min(256, ctx_len)
    assert seq % tm == 0 and ctx_len % tm == 0 and tm % CHUNK == 0
    t_lat, t_ctx = n_batch * seq, n_batch * ctx_len
    t_all = t_lat + t_ctx
    tiles = dict(tm=tm, n_lat_tiles=t_lat // tm)
    seq_tiles = dict(tps_lat=seq // tm, tps_ctx=ctx_len // tm)
    mod_tiles = dict(tiles_per_batch=seq // tm, n_batch=n_batch)

    xs = jnp.concatenate([x.reshape(t_lat, d), ctx.reshape(t_ctx, d)], axis=0)
    c_all = jnp.zeros((MOD_ROWS, d), F32).at[:n_batch].set(c).at[n_batch].set(c_ctx)
    mods = _adaln(c_all, w_ada, b_ada).reshape(DEPTH, MOD_ROWS, 6, d)
    rw_pad = _pad_cols(router_w, LANES)
    rb_pad = _pad_cols(router_b.reshape(1, -1), LANES)

    moe_out = None
    for l in range(DEPTH):
        last = l == DEPTH - 1
        if moe_out is None:
            (h,) = _prenorm(xs, None, None, mods[l], norm_mix_g[l], n_rows=t_all, **tiles, **mod_tiles)
        else:
            xs, h = _prenorm(xs, moe_out, mods[l - 1], mods[l], norm_mix_g[l], n_rows=t_all, **tiles, **mod_tiles)
        w_rw, w_gla, w_gate, mu = _split_in_proj(w_in[l], rwkv_mu[l])
        p_rw = _matmul(h, w_rw, F32, tn=RW_COLS // 7)
        p_gla_lat = _matmul(h, w_gla, F32, tn=GLA_COLS, n_rows=t_lat)
        p_gla_ctx = _matmul(h, w_gla, F32, tn=GLA_COLS, row0=t_lat, n_rows=t_ctx)
        p_gate = _matmul(h, w_gate, BF16, tn=GATE_COLS // 5)

        r, v, kap, lw, bb, kd, bonus, gate = _rwkv_prep(
            p_rw, mu, rwkv_w0[l], _pad_rows(rwkv_w2[l], LORA_PAD), rwkv_a0[l], _pad_rows(rwkv_a2[l], LORA_PAD),
            rwkv_g2[l], rwkv_k_k[l].reshape(1, -1), rwkv_k_a[l].reshape(1, -1), rwkv_r_k[l].reshape(1, -1),
            n_batch=n_batch, **tiles, **seq_tiles)
        y0, y1 = _rwkv_scan(r, v, kap, lw, bb, kd, n_batch=n_batch, nc_lat=seq // CHUNK, nc_ctx=ctx_len // CHUNK)

        u_lat = p_gla_lat.reshape(n_batch, grid_rows, GRID_W, GLA_COLS)
        o_lat, o_ctx = [], []
        for dr in range(2):
            oc, ol = _gla_scan(p_gla_ctx, u_lat, gla_conv[l], _pad_rows(gla_alpha_w2[l, dr], LANES),
                               gla_alpha_b[l, dr].reshape(1, -1), rev=dr == 1, n_batch=n_batch)
            o_ctx.append(oc)
            o_lat.append(ol.reshape(t_lat, GLA_VDIM))

        n_rows = t_lat if last else t_all
        xs, hf, route = _merge(
            xs, y0, y1, bonus, gate, o_lat, o_ctx, p_gate, mods[l], rwkv_ln_g[l], rwkv_ln_b[l], gla_norm_g[l],
            w_branch_a[l].astype(BF16), w_branch_b[l].astype(BF16), w_out[l].astype(BF16), norm_ffn_g[l],
            rw_pad, rb_pad, n_rows=n_rows, tps_ctx=ctx_len // tm, **tiles, **mod_tiles)
        moe_out = _moe(hf, route, exp_w_gate[l].astype(BF16), exp_w_up[l].astype(BF16), exp_w_down[l].astype(BF16),
                       tm=tm)

    (out,) = _prenorm(xs, moe_out, mods[DEPTH - 1], mods[DEPTH - 1], final_norm_g, n_rows=t_lat, final=True,
                      **tiles, **mod_tiles)
    return out.reshape(n_batch, seq, d)
```

```python
import functools

import jax
import jax.numpy as jnp
from jax import lax
from jax.experimental import pallas as pl
from jax.experimental.pallas import tpu as pltpu

F32 = jnp.float32
BF16 = jnp.bfloat16

D_MODEL = 2048
DEPTH = 2
GRID_W = 64
NORM_EPS = 1e-6
RWKV_DIM = 1024
RWKV_HEAD = 64
RWKV_HEADS = 16
DECAY_LORA = 96
ICLR_LORA = 96
GATE_LORA = 256
RWKV_GN_EPS = 64e-5
GLA_HEADS = 4
GLA_KDIM = 512
GLA_VDIM = 1024
GLA_DK = 128
GLA_DV = 256
GLA_GATE_LORA = 16
GLA_TAU = 16.0
GLA_QKV = 2048
N_EXPERTS = 16
EXPERTS_PER_GROUP = 4
N_PAIRS = 24
D_EXPERT = 1408
RWKV_IN = 3520
GLA_SEQ_IN = 2064
GLA_GATE_OFF = RWKV_IN + GLA_SEQ_IN
BR_GATE_OFF = GLA_GATE_OFF + GLA_VDIM

LANES = 128
CHUNK = 64
MOD_ROWS = 16
LORA_PAD = 128
RW_XW = 3 * RWKV_DIM
RW_XA = RW_XW + LORA_PAD
RW_XG = RW_XA + LORA_PAD
RW_COLS = RW_XG + GATE_LORA
GLA_COLS = GLA_QKV + LANES
GATE_COLS = GLA_VDIM + 2 * D_MODEL
VMEM_LIMIT = 56 * 1024 * 1024


def _cparams(sem, vmem=VMEM_LIMIT):
    return pltpu.CompilerParams(dimension_semantics=sem, vmem_limit_bytes=vmem)


def _pick_tile(n, cands):
    for c in cands:
        if n % c == 0:
            return c
    raise ValueError(f"no tile for {n}")


def _mm(a, b):
    return jnp.dot(a.astype(BF16), b.astype(BF16), preferred_element_type=F32)


def _mm_nt(a, b):
    return lax.dot_general(a.astype(BF16), b.astype(BF16), (((1,), (1,)), ((), ())), preferred_element_type=F32)


def _mm_tn(a, b):
    return lax.dot_general(a.astype(BF16), b.astype(BF16), (((0,), (0,)), ((), ())), preferred_element_type=F32)


def _split2(a):
    hi = a.astype(BF16)
    lo = (a - hi.astype(F32)).astype(BF16)
    return hi, lo


def _split3(a):
    hi = a.astype(BF16)
    r1 = a - hi.astype(F32)
    mid = r1.astype(BF16)
    lo = (r1 - mid.astype(F32)).astype(BF16)
    return hi, mid, lo


def _mm_x3(a, b):
    ah, al = _split2(a)
    bh, bl = _split2(b)
    d = functools.partial(jnp.dot, preferred_element_type=F32)
    return d(ah, bh) + (d(al, bh) + d(ah, bl))


def _mm_mask(m01, x):
    h, m, l = _split3(x)
    d = functools.partial(jnp.dot, preferred_element_type=F32)
    return d(m01, h) + (d(m01, m) + d(m01, l))


def _sigmoid(x):
    return 1.0 / (1.0 + jnp.exp(-x))


def _silu(x):
    return x * _sigmoid(x)


def _softplus(z):
    return jnp.maximum(z, 0.0) + jnp.log1p(jnp.exp(-jnp.abs(z)))


def _head_sum64(x):
    ri = lax.broadcasted_iota(jnp.int32, (256, 256), 0) // RWKV_HEAD
    ci = lax.broadcasted_iota(jnp.int32, (256, 256), 1) // RWKV_HEAD
    ones = (ri == ci).astype(BF16)
    hi, lo = _split2(x)
    d = functools.partial(jnp.dot, preferred_element_type=F32)
    outs = [d(hi[:, g * 256:(g + 1) * 256], ones) + d(lo[:, g * 256:(g + 1) * 256], ones)
            for g in range(x.shape[1] // 256)]
    return jnp.concatenate(outs, axis=1)


def _adaln_kernel(c_ref, w_ref, b_ref, o_ref):
    c = c_ref[...]
    o_ref[...] = _mm_x3(_silu(c), w_ref[...]) + b_ref[...]


def _adaln(c_all, w_ada, b_ada):
    nl, d, n = w_ada.shape
    tn = 1024
    return pl.pallas_call(
        _adaln_kernel,
        out_shape=jax.ShapeDtypeStruct((nl, MOD_ROWS, n), F32),
        grid=(nl, n // tn),
        in_specs=[pl.BlockSpec((MOD_ROWS, d), lambda l, j: (0, 0)),
                  pl.BlockSpec((None, d, tn), lambda l, j: (l, 0, j)),
                  pl.BlockSpec((None, 1, tn), lambda l, j: (l, 0, j))],
        out_specs=pl.BlockSpec((None, MOD_ROWS, tn), lambda l, j: (l, 0, j)),
        compiler_params=_cparams(("parallel", "parallel")),
        name="adaln",
    )(c_all, w_ada, b_ada.reshape(nl, 1, n))


def _rms(x, g):
    return x * lax.rsqrt(jnp.mean(x * x, axis=-1, keepdims=True) + NORM_EPS) * g


def _prenorm_kernel(*refs, has_moe, final):
    if has_moe:
        x_ref, o0_ref, o1_ref, mprev_ref, mcur_ref, g_ref = refs[:6]
        outs = refs[6:]
        x = x_ref[...] + mprev_ref[0, 5:6, :] * (o0_ref[...] + o1_ref[...])
    else:
        x_ref, mcur_ref, g_ref = refs[:3]
        outs = refs[3:]
        x = x_ref[...]
    y = _rms(x, g_ref[...])
    if final:
        outs[0][...] = y
        return
    h = y * (1.0 + mcur_ref[0, 1:2, :]) + mcur_ref[0, 0:1, :]
    if has_moe:
        outs[0][...] = x
        outs[1][...] = h.astype(outs[1].dtype)
    else:
        outs[0][...] = h.astype(outs[0].dtype)


def _mod_row_map(n_lat_tiles, tiles_per_batch, n_batch):
    return lambda i: (jnp.where(i < n_lat_tiles, i // tiles_per_batch, n_batch), 0, 0)


def _prenorm(x, moe_out, mods_prev, mods_cur, g, *, tm, n_lat_tiles, tiles_per_batch, n_batch, n_rows, final=False):
    d = x.shape[1]
    has_moe = moe_out is not None
    mrow = _mod_row_map(n_lat_tiles, tiles_per_batch, n_batch)
    row = pl.BlockSpec((tm, d), lambda i: (i, 0))
    mspec = pl.BlockSpec((1, 6, d), mrow)
    gspec = pl.BlockSpec((1, d), lambda i: (0, 0))
    ins, specs = [x], [row]
    if has_moe:
        slot1 = n_rows // tm
        ins += [moe_out, moe_out, mods_prev]
        specs += [row, pl.BlockSpec((tm, d), lambda i: (slot1 + i, 0)), mspec]
    ins += [mods_cur, g.reshape(1, d)]
    specs += [mspec, gspec]
    if final:
        out_shape = [jax.ShapeDtypeStruct((n_rows, d), F32)]
    elif has_moe:
        out_shape = [jax.ShapeDtypeStruct((n_rows, d), F32), jax.ShapeDtypeStruct((n_rows, d), BF16)]
    else:
        out_shape = [jax.ShapeDtypeStruct((n_rows, d), BF16)]
    return pl.pallas_call(
        functools.partial(_prenorm_kernel, has_moe=has_moe, final=final),
        out_shape=out_shape,
        grid=(n_rows // tm,),
        in_specs=specs,
        out_specs=[row] * len(out_shape),
        compiler_params=_cparams(("parallel",)),
        name="prenorm_final" if final else "prenorm",
    )(*ins)


def _matmul_kernel(a_ref, w_ref, o_ref):
    o_ref[...] = jnp.dot(a_ref[...], w_ref[...], preferred_element_type=F32).astype(o_ref.dtype)


def _matmul(a, w, out_dtype, *, tn, row0=0, n_rows=None):
    k = a.shape[1]
    n = w.shape[1]
    m = a.shape[0] if n_rows is None else n_rows
    tm = next(c for c in (1024, 512, 256, 128, 64) if m % c == 0 and row0 % c == 0)
    i0 = row0 // tm
    return pl.pallas_call(
        _matmul_kernel,
        out_shape=jax.ShapeDtypeStruct((m, n), out_dtype),
        grid=(m // tm, n // tn),
        in_specs=[pl.BlockSpec((tm, k), lambda i, j: (i + i0, 0)),
                  pl.BlockSpec((k, tn), lambda i, j: (0, j))],
        out_specs=pl.BlockSpec((tm, tn), lambda i, j: (i, j)),
        compiler_params=_cparams(("parallel", "parallel")),
        name="in_proj",
    )(a, w)


def _seq_flags(i, n_lat_tiles, tps_lat, tps_ctx):
    is_lat = i < n_lat_tiles
    j = jnp.where(is_lat, i % tps_lat, (i - n_lat_tiles) % tps_ctx)
    return j == 0, j == jnp.where(is_lat, tps_lat - 1, tps_ctx - 1)


def _neighbours(ref, prev_ref, next_ref, c0, c1, first, last):
    tm = ref.shape[0]
    p = ref[:, c0:c1]
    rows = lax.broadcasted_iota(jnp.int32, (tm, 1), 0)
    pr = jnp.where(first, 0.0, prev_ref[7:8, c0:c1])
    nx = jnp.where(last, 0.0, next_ref[0:1, c0:c1])
    prev = jnp.where(rows == 0, pr, pltpu.roll(p, 1, 0))
    nxt = jnp.where(rows == tm - 1, nx, pltpu.roll(p, tm - 1, 0))
    return p, prev, nxt


def _halo_specs(tm, cols, n_rows):
    r8 = tm // 8
    last8 = n_rows // 8 - 1
    return [pl.BlockSpec((tm, cols), lambda i: (i, 0)),
            pl.BlockSpec((8, cols), lambda i: (jnp.maximum(i * r8 - 1, 0), 0)),
            pl.BlockSpec((8, cols), lambda i: (jnp.minimum((i + 1) * r8, last8), 0))]


def _rwkv_prep_kernel(p_ref, pp_ref, pn_ref, mu_ref, w0_ref, w2_ref, a0_ref, a2_ref, g2_ref, kk_ref, ka_ref, rk_ref,
                      r_o, v_o, kap_o, lw_o, b_o, kd_o, bonus_o, gate_o, *, n_lat_tiles, tps_lat, tps_ctx):
    first, last = _seq_flags(pl.program_id(0), n_lat_tiles, tps_lat, tps_ctx)

    def shifted(c0, c1):
        p, prev, nxt = _neighbours(p_ref, pp_ref, pn_ref, c0, c1, first, last)
        return p + mu_ref[0:1, c0:c1] * (prev - p) + mu_ref[1:2, c0:c1] * (nxt - p)

    r = shifted(0, RWKV_DIM)
    k = shifted(RWKV_DIM, 2 * RWKV_DIM)
    v = shifted(2 * RWKV_DIM, 3 * RWKV_DIM)
    kk = k * kk_ref[...]
    kap = kk / jnp.maximum(jnp.sqrt(_head_sum64(kk * kk)), 1e-12)
    r_o[...] = r.astype(r_o.dtype)
    v_o[...] = v.astype(v_o.dtype)
    kap_o[...] = kap.astype(kap_o.dtype)
    bonus_o[...] = _head_sum64(r * k * rk_ref[...]) * v
    txw = jnp.tanh(shifted(RW_XW, RW_XA))
    xa = shifted(RW_XA, RW_XG)
    for d in range(2):
        w_log = -_softplus(-(w0_ref[d:d + 1, :] + _mm_x3(txw, w2_ref[d]))) - 0.5
        lw_o[d] = -jnp.exp(w_log)
        a = _sigmoid(a0_ref[d:d + 1, :] + _mm_x3(xa, a2_ref[d]))
        kd_o[d] = (k * (1.0 + (a - 1.0) * ka_ref[...])).astype(kd_o.dtype)
        b_o[d] = (kap * a).astype(b_o.dtype)
    gate_o[...] = _mm_x3(_sigmoid(shifted(RW_XG, RW_COLS)), g2_ref[...])


def _rwkv_prep(p_rw, mu, w0, w2, a0, a2, g2, k_k, k_a, r_k, *, tm, n_batch, n_lat_tiles, tps_lat, tps_ctx):
    n = p_rw.shape[0]
    gm = _group_major(_scan_groups(n_batch), n_batch, n_lat_tiles, tps_lat, tps_ctx)
    full = lambda a: pl.BlockSpec(a.shape, lambda i: (0,) * a.ndim)
    tok = pl.BlockSpec((tm, RWKV_DIM), lambda i: (i, 0))
    row = pl.BlockSpec((tm, RWKV_DIM), lambda i: (gm(i), 0))
    row2 = pl.BlockSpec((2, tm, RWKV_DIM), lambda i: (0, gm(i), 0))
    one = lambda dt: jax.ShapeDtypeStruct((n, RWKV_DIM), dt)
    two = lambda dt: jax.ShapeDtypeStruct((2, n, RWKV_DIM), dt)
    params = [mu, w0, w2, a0, a2, g2, k_k, k_a, r_k]
    return pl.pallas_call(
        functools.partial(_rwkv_prep_kernel, n_lat_tiles=n_lat_tiles, tps_lat=tps_lat, tps_ctx=tps_ctx),
        out_shape=[one(BF16), one(BF16), one(BF16), two(F32), two(BF16), two(BF16), one(F32), one(F32)],
        grid=(n // tm,),
        in_specs=_halo_specs(tm, RW_COLS, n) + [full(a) for a in params],
        out_specs=[row, row, row, row2, row2, row2, tok, tok],
        compiler_params=_cparams(("parallel",)),
        name="rwkv_prep",
    )(p_rw, p_rw, p_rw, *params)


def _order_masks(n, rev):
    ri = lax.broadcasted_iota(jnp.int32, (n, n), 0)
    ci = lax.broadcasted_iota(jnp.int32, (n, n), 1)
    if rev:
        return ci >= ri, ci > ri, ri == ci
    return ci <= ri, ci < ri, ri == ci


HEADS_PER_GROUP = 4
GROUP = HEADS_PER_GROUP * RWKV_HEAD


def _head_block_mask():
    ri = lax.broadcasted_iota(jnp.int32, (GROUP, GROUP), 0) // RWKV_HEAD
    ci = lax.broadcasted_iota(jnp.int32, (GROUP, GROUP), 1) // RWKV_HEAD
    return ri == ci


def _block_diag(y_g, bdm):
    return jnp.where(bdm, jnp.concatenate([y_g] * HEADS_PER_GROUP, axis=0), jnp.zeros((), y_g.dtype))


def _head_prod(x, y, bdm, nt=False):
    xb, yb = x.astype(BF16), y.astype(BF16)
    dims = (((1,), (1,)), ((), ())) if nt else (((1,), (0,)), ((), ()))
    outs = [lax.dot_general(xb[:, g:g + GROUP], _block_diag(yb[:, g:g + GROUP], bdm), dims,
                            preferred_element_type=F32) for g in range(0, x.shape[1], GROUP)]
    return jnp.concatenate(outs, axis=1)


def _head_prod_tn(a, b, bdm):
    ab, bb = a.astype(BF16), b.astype(BF16)
    outs = []
    for g in range(0, a.shape[1], GROUP):
        p = lax.dot_general(ab[:, g:g + GROUP], bb[:, g:g + GROUP], (((0,), (0,)), ((), ())),
                            preferred_element_type=F32)
        p = jnp.where(bdm, p, 0.0)
        outs.append((p[0:64] + p[64:128]) + (p[128:192] + p[192:256]))
    return jnp.concatenate(outs, axis=1)


def _rwkv_chunks(chains):
    row = lax.broadcasted_iota(jnp.int32, (CHUNK, RWKV_DIM), 0)
    col = lax.broadcasted_iota(jnp.int32, (CHUNK, RWKV_DIM), 1) % RWKV_HEAD
    eye_f = (col == row).astype(F32)
    bdm = _head_block_mask()
    prod = functools.partial(_head_prod, bdm=bdm)
    each = lambda f, *cols: [f(*a) for a in zip(*cols)]
    top, bottom = (lambda a: a[:CHUNK]), (lambda a: a[CHUNK:])
    stack = lambda a, b: jnp.concatenate([a, b], axis=0)

    incl, strict, kr, bt, kdt, vs, e_tot = [], [], [], [], [], [], []
    for r_ref, v_ref, kap_ref, lw_ref, b_ref, kd_ref, _, _, rev in chains:
        incl.append(col >= row if rev else col <= row)
        strict.append(col > row if rev else col < row)
        lw = lw_ref[...]
        gam = _mm_mask(_order_masks(CHUNK, rev)[0].astype(BF16), lw)
        e_tot.append(jnp.exp(jnp.sum(lw, axis=0, keepdims=True)))
        e_neg = jnp.exp(-gam)
        kr.append(stack(kap_ref[...] * jnp.exp(gam - lw), r_ref[...] * jnp.exp(gam)))
        bt.append(b_ref[...] * e_neg)
        kdt.append(kd_ref[...] * e_neg)
        vs.append(v_ref[...])
    g_b = each(lambda a, b: prod(a, b, nt=True), kr, bt)
    g_k = each(lambda a, b: prod(a, b, nt=True), kr, kdt)
    n_ab = each(lambda m, g: jnp.where(m, top(g), 0.0), strict, g_b)
    a_kv = each(lambda m, g: jnp.where(m, top(g), 0.0), strict, g_k)
    m_b = each(lambda m, g: jnp.where(m, bottom(g), 0.0), incl, g_b)
    m_kd = each(lambda m, g: jnp.where(m, bottom(g), 0.0), incl, g_k)
    t_inv = each(lambda n: eye_f - n, n_ab)
    pw = each(lambda n: prod(n, n), n_ab)
    for _ in range(4):
        both = each(lambda t, p: prod(stack(t, p), p), t_inv, pw)
        t_inv = each(lambda t, b: t + top(b), t_inv, both)
        pw = each(bottom, both)
    t_inv = each(lambda t, p: t + prod(t, p), t_inv, pw)
    av = each(lambda a, m, v: prod(stack(a, m), v), a_kv, m_kd, vs)
    s = [c[7][...] for c in chains]
    ks = each(lambda a, b: prod(a, b, nt=True), kr, s)
    u = each(lambda t, k, a: prod(t, top(k) + top(a)), t_inv, ks, av)
    mbu = each(prod, m_b, u)
    upd = each(lambda v, uu, kd, b: _head_prod_tn(stack(v, uu), stack(kd, -b), bdm), vs, u, kdt, bt)
    for c, k, a, m, s0, up, e in zip(chains, ks, av, mbu, s, upd, e_tot):
        c[6][...] = bottom(k) + bottom(a) - m
        c[7][...] = (s0 + up) * e


def _rwkv_scan_kernel(rf_ref, vf_ref, kapf_ref, lwf_ref, bf_ref, kdf_ref, rr_ref, vr_ref, kapr_ref, lwr_ref, br_ref,
                      kdr_ref, yf_ref, yr_ref, sf_ref, sr_ref):
    @pl.when(pl.program_id(1) == 0)
    def _():
        sf_ref[...] = jnp.zeros_like(sf_ref)
        sr_ref[...] = jnp.zeros_like(sr_ref)

    chains = []
    for p in range(rf_ref.shape[0]):
        chains.append((rf_ref.at[p], vf_ref.at[p], kapf_ref.at[p], lwf_ref.at[p], bf_ref.at[p], kdf_ref.at[p],
                       yf_ref.at[p], sf_ref.at[p], False))
        chains.append((rr_ref.at[p], vr_ref.at[p], kapr_ref.at[p], lwr_ref.at[p], br_ref.at[p], kdr_ref.at[p],
                       yr_ref.at[p], sr_ref.at[p], True))
    _rwkv_chunks(chains)


def _scan_block_map(rev, n_lat_blocks, nb_lat, nb_ctx):
    def f(b, s):
        c_ctx = (nb_ctx - 1 - s) if rev else s
        c_lat = (nb_lat - 1 - (s - nb_ctx)) if rev else (s - nb_ctx)
        return jnp.where(s < nb_ctx, n_lat_blocks + b * nb_ctx + c_ctx, b * nb_lat + c_lat)
    return f


def _scan_groups(n_batch):
    return 2 if n_batch % 2 == 0 else 1


def _group_major(n_groups, n_batch, n_lat_tiles, tps_lat, tps_ctx):
    per = n_batch // n_groups
    lat, ctx = per * tps_lat, per * tps_ctx

    def f(i):
        is_lat = i < n_lat_tiles
        b = jnp.where(is_lat, i // tps_lat, (i - n_lat_tiles) // tps_ctx)
        j = jnp.where(is_lat, i % tps_lat, (i - n_lat_tiles) % tps_ctx)
        return (b // per) * (lat + ctx) + jnp.where(is_lat, (b % per) * tps_lat + j, lat + (b % per) * tps_ctx + j)
    return f


def _rwkv_scan(r, v, kap, lw, bb, kd, *, n_batch, nc_lat, nc_ctx):
    n = r.shape[0]
    ng = _scan_groups(n_batch)
    per = n_batch // ng
    g3 = lambda a: a.reshape(ng, n // ng, RWKV_DIM)
    g4 = lambda a: a.reshape(2, ng, n // ng, RWKV_DIM)
    specs = []
    for d, rev in enumerate((False, True)):
        blk = _scan_block_map(rev, per * nc_lat, nc_lat, nc_ctx)
        row = pl.BlockSpec((ng, CHUNK, RWKV_DIM), lambda b, s, blk=blk: (0, blk(b, s), 0))
        rowd = pl.BlockSpec((None, ng, CHUNK, RWKV_DIM), lambda b, s, blk=blk, d=d: (d, 0, blk(b, s), 0))
        specs.append((row, rowd))
    (rowf, rowdf), (rowr, rowdr) = specs
    out = jax.ShapeDtypeStruct((ng, n // ng, RWKV_DIM), F32)
    state = pltpu.VMEM((ng, RWKV_HEAD, RWKV_DIM), F32)
    yf, yr = pl.pallas_call(
        _rwkv_scan_kernel,
        out_shape=[out, out],
        grid=(per, nc_ctx + nc_lat),
        in_specs=[rowf, rowf, rowf, rowdf, rowdf, rowdf, rowr, rowr, rowr, rowdr, rowdr, rowdr],
        out_specs=[rowf, rowr],
        scratch_shapes=[state, state],
        compiler_params=_cparams(("parallel", "arbitrary")),
        name="rwkv_scan",
    )(g3(r), g3(v), g3(kap), g4(lw), g4(bb), g4(kd), g3(r), g3(v), g3(kap), g4(lw), g4(bb), g4(kd))
    return yf.reshape(n, RWKV_DIM), yr.reshape(n, RWKV_DIM)


def _gla_blocks(dirs, n_rows, cw_ref):
    rows = lax.broadcasted_iota(jnp.int32, (n_rows, 1), 0)
    for d in dirs:
        for c0 in range(0, GLA_QKV, 512):
            c1 = c0 + 512
            p = d["load"](c0, c1)
            prev = jnp.where(rows == 0, d["prev_row"](c0, c1), pltpu.roll(p, 1, 0))
            nxt = jnp.where(rows == n_rows - 1, d["next_row"](c0, c1), pltpu.roll(p, n_rows - 1, 0))
            act = _silu(cw_ref[0:1, c0:c1] * prev + cw_ref[1:2, c0:c1] * p + cw_ref[2:3, c0:c1] * nxt)
            if c0 == 0:
                act = act * (GLA_DK ** -0.5)
            d["qkv"][0:n_rows, c0:c1] = act
        z = _mm_x3(d["load"](GLA_QKV, GLA_COLS), d["aw"][...]) + d["ab"][...]
        d["g"][0:n_rows, :] = -_softplus(-z) * (1.0 / GLA_TAU)

    each = lambda f, *cols: [f(*a) for a in zip(*cols)]
    incl = [_order_masks(CHUNK, d["rev"])[0] for d in dirs]
    incl_b = [m.astype(BF16) for m in incl]
    n_chunks = n_rows // CHUNK
    for step in range(n_chunks):
        rs = [slice(c * CHUNK, (c + 1) * CHUNK) for c in ((n_chunks - 1 - step) if d["rev"] else step for d in dirs)]
        g = each(lambda d, r: d["g"][r, :], dirs, rs)
        gc = each(_mm_mask, incl_b, g)
        gl = each(lambda a: jnp.sum(a, axis=0, keepdims=True), g)
        q = each(lambda d, r: d["qkv"][r, 0:GLA_KDIM], dirs, rs)
        k = each(lambda d, r: d["qkv"][r, GLA_KDIM:2 * GLA_KDIM], dirs, rs)
        qg = each(lambda a, c: a * jnp.exp(c), q, gc)
        kg = each(lambda a, c: a * jnp.exp(-c), k, gc)
        ke = each(lambda a, l, c: a * jnp.exp(l - c), k, gl, gc)
        e_l = each(jnp.exp, gl)
        for h in range(GLA_HEADS):
            ks = slice(h * GLA_DK, (h + 1) * GLA_DK)
            vs = slice(2 * GLA_KDIM + h * GLA_DV, 2 * GLA_KDIM + (h + 1) * GLA_DV)
            v_h = each(lambda d, r: d["qkv"][r, vs], dirs, rs)
            att = each(lambda m, a, b: jnp.where(m, _mm_nt(a[:, ks], b[:, ks]), 0.0), incl, qg, kg)
            st = [d["st"][h] for d in dirs]
            out = each(lambda a, v, qq, s0: _mm(a, v) + _mm_nt(qq[:, ks], s0), att, v_h, qg, st)
            new = each(lambda s0, e, v, kk: s0 * e[:, ks] + _mm_tn(v, kk[:, ks]), st, e_l, v_h, ke)
            for d, r, o_h, s1 in zip(dirs, rs, out, new):
                d["o"][r, h * GLA_DV:(h + 1) * GLA_DV] = o_h
                d["st"][h] = s1


def _gla_kernel(uc_ref, ulf_ref, ulfp_ref, ulfn_ref, ulr_ref, ulrp_ref, ulrn_ref, cw_ref, awf_ref, abf_ref, awr_ref,
                abr_ref, ocf_ref, ocr_ref, olf_ref, olr_ref, stf_ref, str_ref, qkvf_s, qkvr_s, gf_s, gr_s, of_s,
                or_s, *, n_wg):
    s = pl.program_id(1)
    grid_rows, wg_cols = ulf_ref.shape[0], ulf_ref.shape[1]
    zero_row = lambda c0, c1: jnp.zeros((1, c1 - c0), F32)
    fwd = dict(aw=awf_ref, ab=abf_ref, st=stf_ref, qkv=qkvf_s, g=gf_s, o=of_s, rev=False)
    bwd = dict(aw=awr_ref, ab=abr_ref, st=str_ref, qkv=qkvr_s, g=gr_s, o=or_s, rev=True)

    @pl.when(s == 0)
    def _():
        stf_ref[...] = jnp.zeros_like(stf_ref)
        str_ref[...] = jnp.zeros_like(str_ref)
        n = uc_ref.shape[0]
        ctx = dict(load=lambda c0, c1: uc_ref[:, c0:c1], prev_row=zero_row, next_row=zero_row)
        _gla_blocks([{**fwd, **ctx}, {**bwd, **ctx}], n, cw_ref)
        ocf_ref[...] = of_s[0:n, :]
        ocr_ref[...] = or_s[0:n, :]

    @pl.when(s > 0)
    def _():
        n = grid_rows * wg_cols

        def lat(u_ref, up_ref, un_ref, wg):
            hr = up_ref.shape[0]
            return dict(
                load=lambda c0, c1: jnp.concatenate([u_ref[:, w, c0:c1] for w in range(wg_cols)], axis=0),
                prev_row=lambda c0, c1: jnp.where(wg == 0, 0.0, up_ref[hr - 1:hr, 7, c0:c1]),
                next_row=lambda c0, c1: jnp.where(wg == n_wg - 1, 0.0, un_ref[0:1, 0, c0:c1]))

        _gla_blocks([{**fwd, **lat(ulf_ref, ulfp_ref, ulfn_ref, s - 1)},
                     {**bwd, **lat(ulr_ref, ulrp_ref, ulrn_ref, n_wg - s)}], n, cw_ref)
        for w in range(wg_cols):
            olf_ref[:, w, :] = of_s[w * grid_rows:(w + 1) * grid_rows, :]
            olr_ref[:, w, :] = or_s[w * grid_rows:(w + 1) * grid_rows, :]


def _gla_scan(u_ctx, u_lat, conv_w, alpha_w, alpha_b, *, n_batch):
    t_ctx, ch = u_ctx.shape
    lc = t_ctx // n_batch
    _, grid_rows, grid_w, _ = u_lat.shape
    wg_cols = min(grid_w, max(8, 256 // grid_rows))
    n_wg = grid_w // wg_cols
    assert (wg_cols * grid_rows) % CHUNK == 0 and lc % CHUNK == 0 and grid_w % wg_cols == 0 and wg_cols % 8 == 0
    hr = 8 if grid_rows % 8 == 0 else grid_rows
    n_rows = max(lc, wg_cols * grid_rows)

    def group(rev):
        return lambda s: jnp.clip((n_wg - s) if rev else (s - 1), 0, n_wg - 1)

    full = lambda a: pl.BlockSpec(a.shape, lambda b, s: (0,) * a.ndim)
    lat = lambda c, wg: pl.BlockSpec((None, grid_rows, wg_cols, c), lambda b, s: (b, 0, wg(s), 0))

    def lat_in(wg):
        return [lat(ch, wg),
                pl.BlockSpec((None, hr, 8, ch), lambda b, s: (
                    b, grid_rows // hr - 1, jnp.maximum(wg(s) * (wg_cols // 8) - 1, 0), 0)),
                pl.BlockSpec((None, hr, 8, ch), lambda b, s: (
                    b, 0, jnp.minimum((wg(s) + 1) * (wg_cols // 8), grid_w // 8 - 1), 0))]

    ctx_out = jax.ShapeDtypeStruct((t_ctx, GLA_VDIM), F32)
    lat_out = jax.ShapeDtypeStruct((n_batch, grid_rows, grid_w, GLA_VDIM), F32)
    ctx_spec = pl.BlockSpec((lc, GLA_VDIM), lambda b, s: (b, 0))
    per_dir = lambda shape: [pltpu.VMEM(shape, F32)] * 2
    params = [conv_w, alpha_w[0], alpha_b[0], alpha_w[1], alpha_b[1]]
    return pl.pallas_call(
        functools.partial(_gla_kernel, n_wg=n_wg),
        out_shape=[ctx_out, ctx_out, lat_out, lat_out],
        grid=(n_batch, 1 + n_wg),
        in_specs=[pl.BlockSpec((lc, ch), lambda b, s: (b, 0))] + lat_in(group(False)) + lat_in(group(True))
        + [full(a) for a in params],
        out_specs=[ctx_spec, ctx_spec, lat(GLA_VDIM, group(False)), lat(GLA_VDIM, group(True))],
        scratch_shapes=per_dir((GLA_HEADS, GLA_DV, GLA_DK)) + per_dir((n_rows, GLA_QKV))
        + per_dir((n_rows, GLA_KDIM)) + per_dir((n_rows, GLA_VDIM)),
        compiler_params=_cparams(("parallel", "arbitrary")),
        name="gla_scan",
    )(u_ctx, u_lat, u_lat, u_lat, u_lat, u_lat, u_lat, *params)


def _pair_tables():
    ri = lax.broadcasted_iota(jnp.int32, (LANES, LANES), 0)
    ci = lax.broadcasted_iota(jnp.int32, (LANES, LANES), 1)
    grp = ci // 6
    pr = ci % 6
    lo = jnp.where(pr < 3, 0, jnp.where(pr < 5, 1, 2))
    hi = jnp.where(pr < 3, pr + 1, jnp.where(pr < 5, pr - 1, 3))
    member = ((ri == grp * EXPERTS_PER_GROUP + lo) | (ri == grp * EXPERTS_PER_GROUP + hi)) & (ci < N_PAIRS)
    return member


def _route(h, rw_ref, rb_ref):
    lane = lax.broadcasted_iota(jnp.int32, (h.shape[0], LANES), 1)
    scores = _sigmoid(_mm_x3(h, rw_ref[...]))
    sel = jnp.where(lane < N_EXPERTS, scores + rb_ref[...], 0.0)
    member = _pair_tables()
    s3 = _split3(sel)
    mb = member.astype(BF16)
    d = functools.partial(jnp.dot, preferred_element_type=F32)
    ps = d(s3[0], mb) + (d(s3[1], mb) + d(s3[2], mb))
    ps = jnp.where(lane < N_PAIRS, ps, -jnp.inf)
    best = jnp.max(ps, axis=-1, keepdims=True)
    bucket = jnp.min(jnp.where(ps == best, lane, LANES), axis=-1, keepdims=True)
    onehot = (lane == bucket).astype(BF16)
    chosen = lax.dot_general(onehot, mb, (((1,), (1,)), ((), ())), preferred_element_type=F32) > 0.5
    picked = jnp.where(chosen, scores, 0.0)
    gates = picked / jnp.sum(picked, axis=-1, keepdims=True)
    e_lo = jnp.min(jnp.where(chosen, lane, LANES), axis=-1, keepdims=True)
    e_hi = jnp.max(jnp.where(chosen, lane, -1), axis=-1, keepdims=True)
    w_lo = jnp.sum(jnp.where(lane == e_lo, gates, 0.0), axis=-1, keepdims=True)
    w_hi = jnp.sum(jnp.where(lane == e_hi, gates, 0.0), axis=-1, keepdims=True)
    out = jnp.where(lane == 0, e_lo.astype(F32), 0.0)
    out = jnp.where(lane == 1, e_hi.astype(F32), out)
    out = jnp.where(lane == 2, w_lo, out)
    return jnp.where(lane == 3, w_hi, out)


def _merge_kernel(x_ref, y0_ref, y1_ref, bonus_ref, gate_ref, ol0_ref, ol1_ref, oc0_ref, oc1_ref, pg_ref, mods_ref,
                  lng_ref, lnb_ref, gng_ref, wa_ref, wb_ref, wo_ref, nfg_ref, rw_ref, rb_ref,
                  xs_o, h_o, route_o, *, n_lat_tiles):
    y = y0_ref[...] + y1_ref[...]
    mean = _head_sum64(y) * (1.0 / RWKV_HEAD)
    yc = y - mean
    var = _head_sum64(yc * yc) * (1.0 / RWKV_HEAD)
    ya = (yc * lax.rsqrt(var + RWKV_GN_EPS) * lng_ref[...] + lnb_ref[...] + bonus_ref[...]) * gate_ref[...]
    o = jnp.where(pl.program_id(0) < n_lat_tiles, ol0_ref[...] + ol1_ref[...], oc0_ref[...] + oc1_ref[...])
    parts = []
    for h in range(GLA_HEADS):
        o_h = o[:, h * GLA_DV:(h + 1) * GLA_DV]
        parts.append(o_h * lax.rsqrt(jnp.mean(o_h * o_h, axis=-1, keepdims=True) + NORM_EPS) * gng_ref[...])
    yb = jnp.concatenate(parts, axis=1) * _silu(pg_ref[:, 0:GLA_VDIM].astype(F32))
    ga = _sigmoid(pg_ref[:, GLA_VDIM:GLA_VDIM + D_MODEL].astype(F32))
    gb = _sigmoid(pg_ref[:, GLA_VDIM + D_MODEL:GATE_COLS].astype(F32))
    z = ga * _mm(ya, wa_ref[...]) + gb * _mm(yb, wb_ref[...])
    xs = x_ref[...] + mods_ref[0, 2:3, :] * _mm(z, wo_ref[...])
    xs_o[...] = xs
    h = _rms(xs, nfg_ref[...]) * (1.0 + mods_ref[0, 4:5, :]) + mods_ref[0, 3:4, :]
    h_o[...] = h
    route_o[...] = _route(h, rw_ref, rb_ref)


def _merge(x, y0, y1, bonus, gate, o_lat, o_ctx, pg, mods, ln_g, ln_b, gn_g, wa, wb, wo, nf_g, rw, rb,
           *, tm, n_rows, n_lat_tiles, tiles_per_batch, n_batch, tps_ctx):
    d = D_MODEL
    mrow = _mod_row_map(n_lat_tiles, tiles_per_batch, n_batch)
    gm = _group_major(_scan_groups(n_batch), n_batch, n_lat_tiles, tiles_per_batch, tps_ctx)
    row = lambda c: pl.BlockSpec((tm, c), lambda i: (i, 0))
    yrow = pl.BlockSpec((tm, RWKV_DIM), lambda i: (gm(i), 0))
    lat = pl.BlockSpec((tm, GLA_VDIM), lambda i: (jnp.minimum(i, n_lat_tiles - 1), 0))
    ctx = pl.BlockSpec((tm, GLA_VDIM), lambda i: (jnp.maximum(i - n_lat_tiles, 0), 0))
    full = lambda a: pl.BlockSpec(a.shape, lambda i: (0,) * a.ndim)
    consts = [ln_g.reshape(1, -1), ln_b.reshape(1, -1), gn_g.reshape(1, -1), wa, wb, wo, nf_g.reshape(1, -1), rw, rb]
    return pl.pallas_call(
        functools.partial(_merge_kernel, n_lat_tiles=n_lat_tiles),
        out_shape=[jax.ShapeDtypeStruct((n_rows, d), F32), jax.ShapeDtypeStruct((n_rows, d), F32),
                   jax.ShapeDtypeStruct((n_rows, LANES), F32)],
        grid=(n_rows // tm,),
        in_specs=[row(d), yrow, yrow, row(RWKV_DIM), row(RWKV_DIM), lat, lat, ctx, ctx,
                  row(GATE_COLS), pl.BlockSpec((1, 6, d), mrow)] + [full(a) for a in consts],
        out_specs=[row(d), row(d), row(LANES)],
        compiler_params=_cparams(("parallel",)),
        name="merge",
    )(x, y0, y1, bonus, gate, o_lat[0], o_lat[1], o_ctx[0], o_ctx[1], pg, mods, *consts)


DMA_ISSUE_UNROLL = 8
MOE_UP_CHUNK = 256


def _moe_kernel(te_ref, nu_ref, src_ref, src_next_ref, dst_prev_ref, dst_ref, w_ref, h_hbm, wg_ref, wu_ref, wd_ref,
                o_hbm, xbuf, obuf, act_s, sem_in, sem_out, sem_fill, *, n_real_rows):
    del te_ref
    tm = xbuf.shape[1]
    i = pl.program_id(0)
    n_used = nu_ref[0]
    slot = i % 2

    other = 1 - slot

    def gather_row(idx_ref, s, r):
        return pltpu.make_async_copy(h_hbm.at[pl.ds(idx_ref[0, 0, r], 1)], xbuf.at[s, pl.ds(r, 1)], sem_in.at[s])

    def scatter_row(idx_ref, s, r):
        return pltpu.make_async_copy(obuf.at[s, pl.ds(r, 1)], o_hbm.at[pl.ds(idx_ref[0, 0, r], 1)], sem_out.at[s])

    def wait_gather(s):
        pltpu.make_async_copy(h_hbm.at[pl.ds(0, tm)], xbuf.at[s], sem_in.at[s]).wait()

    def wait_scatter(s):
        pltpu.make_async_copy(obuf.at[s], o_hbm.at[pl.ds(0, tm)], sem_out.at[s]).wait()

    def issue_all(make_row):
        def body(r, carry):
            make_row(r).start()
            return carry
        lax.fori_loop(0, tm, body, 0, unroll=DMA_ISSUE_UNROLL)

    @pl.when(i < n_used)
    def _():
        @pl.when(i == 0)
        def _():
            obuf[1] = jnp.zeros(obuf.shape[1:], obuf.dtype)
            fills = [pltpu.make_async_copy(obuf.at[1], o_hbm.at[pl.ds(n_real_rows + k * tm, tm)], sem_fill)
                     for k in range((o_hbm.shape[0] - n_real_rows) // tm)]
            for f in fills:
                f.start()
            for f in fills:
                f.wait()
            issue_all(lambda r: gather_row(src_ref, 0, r))

        @pl.when(i >= 1)
        def _():
            wait_scatter(slot)

        wait_gather(slot)
        bounds = list(range(0, D_EXPERT, MOE_UP_CHUNK)) + [D_EXPERT]
        n_chunks = len(bounds) - 1
        for c in range(n_chunks):
            c0, c1 = bounds[c], bounds[c + 1]
            xb = xbuf[slot].astype(BF16)
            g = jnp.dot(xb, wg_ref[:, c0:c1], preferred_element_type=F32)
            u = jnp.dot(xb, wu_ref[:, c0:c1], preferred_element_type=F32)
            act_s[:, c0:c1] = (_silu(g) * u).astype(BF16)
            for r in range(c * tm // n_chunks, (c + 1) * tm // n_chunks):
                gather_row(src_next_ref, other, r).start()
                scatter_row(dst_prev_ref, other, r).start()

        obuf[slot] = jnp.dot(act_s[...], wd_ref[...], preferred_element_type=F32) * w_ref[...][:, 0:1]

        @pl.when(i == n_used - 1)
        def _():
            issue_all(lambda r: scatter_row(dst_ref, slot, r))
            wait_scatter(other)
            wait_scatter(slot)
            wait_gather(other)


def _moe(h, route, wg, wu, wd, *, tm):
    t, d = h.shape
    i32 = jnp.int32
    ids = jnp.concatenate([route[:, 0], route[:, 1]]).astype(i32)
    wts = jnp.concatenate([route[:, 2], route[:, 3]])
    n_tiles = (2 * t) // tm + N_EXPERTS
    n_spill = (N_EXPERTS + 1) * tm
    _, order, wts_s = lax.sort((ids, jnp.arange(2 * t, dtype=i32), wts), num_keys=1, is_stable=True)
    counts = jnp.sum((ids[:, None] == jnp.arange(N_EXPERTS, dtype=i32)[None, :]).astype(i32), axis=0)
    starts = jnp.cumsum(counts) - counts
    pcounts = ((counts + tm - 1) // tm) * tm
    pend = jnp.cumsum(pcounts)
    poff = pend - pcounts
    n_used = (pend[-1] // tm).astype(i32)
    tile = jnp.arange(n_tiles, dtype=i32)
    te = jnp.sum((tile[:, None] * tm >= pend[None, :]).astype(i32), axis=1)
    te = jnp.where(tile < n_used, te, jnp.take(te, jnp.maximum(n_used - 1, 0))).astype(i32)
    rank0 = tile * tm - jnp.take(poff, te)
    n_valid = jnp.where(tile < n_used, jnp.clip(jnp.take(counts, te) - rank0, 0, tm), 0)
    n_pos = n_tiles * tm
    shift = poff - starts
    e_tile = te[:, None]

    def spread(arr):
        ext = jnp.concatenate([jnp.zeros((n_pos,), arr.dtype), arr, jnp.zeros((n_pos,), arr.dtype)])
        out = jnp.zeros((n_tiles, tm), arr.dtype)
        for e in range(N_EXPERTS):
            seg = lax.dynamic_slice(ext, (n_pos - shift[e],), (n_pos,)).reshape(n_tiles, tm)
            out = jnp.where(e_tile == e, seg, out)
        return out

    a = spread(order)
    w_tile = spread(wts_s)
    valid = jnp.arange(tm, dtype=i32)[None, :] < n_valid[:, None]
    src = jnp.where(valid, a % t, 0).astype(i32)
    pad = (~valid) & (tile < n_used)[:, None]
    spill = 2 * t + tm + jnp.minimum(jnp.cumsum(pad.reshape(-1).astype(i32)) - 1, n_spill - tm - 1).reshape(n_tiles, tm)
    dst = jnp.where(valid, a, spill).astype(i32)
    dst = jnp.concatenate([2 * t + jnp.arange(tm, dtype=i32)[None, :], dst], axis=0)
    wrow = jnp.broadcast_to(jnp.where(valid, w_tile, 0.0).reshape(-1, 1), (n_tiles * tm, LANES))

    smem = functools.partial(pl.BlockSpec, (1, 1, tm), memory_space=pltpu.SMEM)
    src3, dst3 = src.reshape(n_tiles, 1, tm), dst.reshape(n_tiles + 1, 1, tm)
    return pl.pallas_call(
        functools.partial(_moe_kernel, n_real_rows=2 * t),
        out_shape=jax.ShapeDtypeStruct((2 * t + n_spill, d), F32),
        grid_spec=pltpu.PrefetchScalarGridSpec(
            num_scalar_prefetch=2,
            grid=(n_tiles,),
            in_specs=[smem(lambda i, *_: (i, 0, 0)),
                      smem(lambda i, *_: (jnp.minimum(i + 1, n_tiles - 1), 0, 0)),
                      smem(lambda i, *_: (i, 0, 0)),
                      smem(lambda i, *_: (i + 1, 0, 0)),
                      pl.BlockSpec((tm, LANES), lambda i, *_: (i, 0)),
                      pl.BlockSpec(memory_space=pl.ANY),
                      pl.BlockSpec((None, d, D_EXPERT), lambda i, te_r, *_: (te_r[i], 0, 0)),
                      pl.BlockSpec((None, d, D_EXPERT), lambda i, te_r, *_: (te_r[i], 0, 0)),
                      pl.BlockSpec((None, D_EXPERT, d), lambda i, te_r, *_: (te_r[i], 0, 0))],
            out_specs=pl.BlockSpec(memory_space=pl.ANY),
            scratch_shapes=[pltpu.VMEM((2, tm, d), F32), pltpu.VMEM((2, tm, d), F32),
                            pltpu.VMEM((tm, D_EXPERT), BF16),
                            pltpu.SemaphoreType.DMA((2,)), pltpu.SemaphoreType.DMA((2,)), pltpu.SemaphoreType.DMA]),
        compiler_params=_cparams(("arbitrary",)),
        name="moe_ffn",
    )(te, n_used.reshape(1), src3, src3, dst3, dst3, wrow, h, wg, wu, wd)


def _pad_cols(a, n):
    return jnp.pad(a, [(0, 0)] * (a.ndim - 1) + [(0, n - a.shape[-1])])


def _pad_rows(a, n):
    return jnp.pad(a, [(0, 0)] * (a.ndim - 2) + [(0, n - a.shape[-2]), (0, 0)])


def _split_in_proj(w_in, mu):
    s = (RWKV_DIM, 2 * RWKV_DIM, 3 * RWKV_DIM, 3 * RWKV_DIM + DECAY_LORA, 3 * RWKV_DIM + DECAY_LORA + ICLR_LORA)

    def rw_layout(a):
        return jnp.concatenate([a[..., :s[2]], _pad_cols(a[..., s[2]:s[3]], LORA_PAD),
                                _pad_cols(a[..., s[3]:s[4]], LORA_PAD), a[..., s[4]:RWKV_IN]], axis=-1)

    w_in = w_in.astype(BF16)
    w_rw = rw_layout(w_in)
    w_gla = _pad_cols(w_in[:, RWKV_IN:GLA_GATE_OFF], GLA_COLS)
    w_gate = w_in[:, GLA_GATE_OFF:]
    return w_rw, w_gla, w_gate, rw_layout(mu)


def kernel(x, c, ctx, c_ctx, w_ada, b_ada, norm_mix_g, norm_ffn_g, w_in, rwkv_mu, rwkv_w0, rwkv_w2, rwkv_a0,
           rwkv_a2, rwkv_g2, rwkv_k_k, rwkv_k_a, rwkv_r_k, rwkv_ln_g, rwkv_ln_b, gla_conv, gla_alpha_w2,
           gla_alpha_b, gla_norm_g, w_branch_a, w_branch_b, w_out, router_w, router_b, exp_w_gate, exp_w_up,
           exp_w_down, final_norm_g):
    n_batch, seq, d = x.shape
    ctx_len = ctx.shape[1]
    assert d == D_MODEL and seq % GRID_W == 0 and n_batch < MOD_ROWS
    grid_rows = seq // GRID_W
    tm = min(256, ctx_len)
    assert seq % tm == 0 and ctx_len % tm == 0 and tm % CHUNK == 0
    t_lat, t_ctx = n_batch * seq, n_batch * ctx_len
    t_all = t_lat + t_ctx
    tiles = dict(tm=tm, n_lat_tiles=t_lat // tm)
    seq_tiles = dict(tps_lat=seq // tm, tps_ctx=ctx_len // tm)
    mod_tiles = dict(tiles_per_batch=seq // tm, n_batch=n_batch)

    xs = jnp.concatenate([x.reshape(t_lat, d), ctx.reshape(t_ctx, d)], axis=0)
    c_all = jnp.zeros((MOD_ROWS, d), F32).at[:n_batch].set(c).at[n_batch].set(c_ctx)
    mods = _adaln(c_all, w_ada, b_ada).reshape(DEPTH, MOD_ROWS, 6, d)
    rw_pad = _pad_cols(router_w, LANES)
    rb_pad = _pad_cols(router_b.reshape(1, -1), LANES)

    moe_out = None
    for l in range(DEPTH):
        last = l == DEPTH - 1
        if moe_out is None:
            (h,) = _prenorm(xs, None, None, mods[l], norm_mix_g[l], n_rows=t_all, **tiles, **mod_tiles)
        else:
            xs, h = _prenorm(xs, moe_out, mods[l - 1], mods[l], norm_mix_g[l], n_rows=t_all, **tiles, **mod_tiles)
        w_rw, w_gla, w_gate, mu = _split_in_proj(w_in[l], rwkv_mu[l])
        p_rw = _matmul(h, w_rw, F32, tn=RW_COLS // 7)
        p_gla_lat = _matmul(h, w_gla, F32, tn=GLA_COLS, n_rows=t_lat)
        p_gla_ctx = _matmul(h, w_gla, F32, tn=GLA_COLS, row0=t_lat, n_rows=t_ctx)
        p_gate = _matmul(h, w_gate, BF16, tn=GATE_COLS // 5)

        r, v, kap, lw, bb, kd, bonus, gate = _rwkv_prep(
            p_rw, mu, rwkv_w0[l], _pad_rows(rwkv_w2[l], LORA_PAD), rwkv_a0[l], _pad_rows(rwkv_a2[l], LORA_PAD),
            rwkv_g2[l], rwkv_k_k[l].reshape(1, -1), rwkv_k_a[l].reshape(1, -1), rwkv_r_k[l].reshape(1, -1),
            n_batch=n_batch, **tiles, **seq_tiles)
        y0, y1 = _rwkv_scan(r, v, kap, lw, bb, kd, n_batch=n_batch, nc_lat=seq // CHUNK, nc_ctx=ctx_len // CHUNK)

        u_lat = p_gla_lat.reshape(n_batch, grid_rows, GRID_W, GLA_COLS)
        ocf, ocr, olf, olr = _gla_scan(p_gla_ctx, u_lat, gla_conv[l], _pad_rows(gla_alpha_w2[l], LANES),
                                       gla_alpha_b[l].reshape(2, 1, -1), n_batch=n_batch)
        o_ctx = [ocf, ocr]
        o_lat = [olf.reshape(t_lat, GLA_VDIM), olr.reshape(t_lat, GLA_VDIM)]

        n_rows = t_lat if last else t_all
        xs, hf, route = _merge(
            xs, y0, y1, bonus, gate, o_lat, o_ctx, p_gate, mods[l], rwkv_ln_g[l], rwkv_ln_b[l], gla_norm_g[l],
            w_branch_a[l].astype(BF16), w_branch_b[l].astype(BF16), w_out[l].astype(BF16), norm_ffn_g[l],
            rw_pad, rb_pad, n_rows=n_rows, tps_ctx=ctx_len // tm, **tiles, **mod_tiles)
        moe_out = _moe(hf, route, exp_w_gate[l].astype(BF16), exp_w_up[l].astype(BF16), exp_w_down[l].astype(BF16),
                       tm=tm)

    (out,) = _prenorm(xs, moe_out, mods[DEPTH - 1], mods[DEPTH - 1], final_norm_g, n_rows=t_lat, final=True,
                      **tiles, **mod_tiles)
    return out.reshape(n_batch, seq, d)
```

```python
import functools

import jax
import jax.numpy as jnp
from jax import lax
from jax.experimental import pallas as pl
from jax.experimental.pallas import tpu as pltpu

F32 = jnp.float32
BF16 = jnp.bfloat16

D_MODEL = 2048
DEPTH = 2
GRID_W = 64
NORM_EPS = 1e-6
RWKV_DIM = 1024
RWKV_HEAD = 64
RWKV_HEADS = 16
DECAY_LORA = 96
ICLR_LORA = 96
GATE_LORA = 256
RWKV_GN_EPS = 64e-5
GLA_HEADS = 4
GLA_KDIM = 512
GLA_VDIM = 1024
GLA_DK = 128
GLA_DV = 256
GLA_GATE_LORA = 16
GLA_TAU = 16.0
GLA_QKV = 2048
N_EXPERTS = 16
EXPERTS_PER_GROUP = 4
N_PAIRS = 24
D_EXPERT = 1408
RWKV_IN = 3520
GLA_SEQ_IN = 2064
GLA_GATE_OFF = RWKV_IN + GLA_SEQ_IN
BR_GATE_OFF = GLA_GATE_OFF + GLA_VDIM

LANES = 128
CHUNK = 64
MOD_ROWS = 16
LORA_PAD = 128
RW_XW = 3 * RWKV_DIM
RW_XA = RW_XW + LORA_PAD
RW_XG = RW_XA + LORA_PAD
RW_COLS = RW_XG + GATE_LORA
GLA_COLS = GLA_QKV + LANES
GATE_COLS = GLA_VDIM + 2 * D_MODEL
VMEM_LIMIT = 56 * 1024 * 1024


def _cparams(sem, vmem=VMEM_LIMIT):
    return pltpu.CompilerParams(dimension_semantics=sem, vmem_limit_bytes=vmem)


def _pick_tile(n, cands):
    for c in cands:
        if n % c == 0:
            return c
    raise ValueError(f"no tile for {n}")


def _mm(a, b):
    return jnp.dot(a.astype(BF16), b.astype(BF16), preferred_element_type=F32)


def _mm_nt(a, b):
    return lax.dot_general(a.astype(BF16), b.astype(BF16), (((1,), (1,)), ((), ())), preferred_element_type=F32)


def _mm_tn(a, b):
    return lax.dot_general(a.astype(BF16), b.astype(BF16), (((0,), (0,)), ((), ())), preferred_element_type=F32)


def _split2(a):
    hi = a.astype(BF16)
    lo = (a - hi.astype(F32)).astype(BF16)
    return hi, lo


def _split3(a):
    hi = a.astype(BF16)
    r1 = a - hi.astype(F32)
    mid = r1.astype(BF16)
    lo = (r1 - mid.astype(F32)).astype(BF16)
    return hi, mid, lo


def _mm_x3(a, b):
    ah, al = _split2(a)
    bh, bl = b if isinstance(b, tuple) else _split2(b)
    d = functools.partial(jnp.dot, preferred_element_type=F32)
    return d(ah, bh) + (d(al, bh) + d(ah, bl))


def _presplit(w):
    hi = w.astype(BF16)
    return jnp.stack([hi, (w - hi.astype(F32)).astype(BF16)], axis=-3)


def _mm_mask(m01, x):
    h, m, l = _split3(x)
    d = functools.partial(jnp.dot, preferred_element_type=F32)
    return d(m01, h) + (d(m01, m) + d(m01, l))


def _sigmoid(x):
    return 1.0 / (1.0 + jnp.exp(-x))


def _silu(x):
    return x * _sigmoid(x)


def _softplus(z):
    return jnp.maximum(z, 0.0) + jnp.log1p(jnp.exp(-jnp.abs(z)))


def _head_sum64(x):
    ri = lax.broadcasted_iota(jnp.int32, (256, 256), 0) // RWKV_HEAD
    ci = lax.broadcasted_iota(jnp.int32, (256, 256), 1) // RWKV_HEAD
    ones = (ri == ci).astype(BF16)
    hi, lo = _split2(x)
    d = functools.partial(jnp.dot, preferred_element_type=F32)
    outs = [d(hi[:, g * 256:(g + 1) * 256], ones) + d(lo[:, g * 256:(g + 1) * 256], ones)
            for g in range(x.shape[1] // 256)]
    return jnp.concatenate(outs, axis=1)


def _adaln_kernel(c_ref, w_ref, b_ref, o_ref):
    c = c_ref[...]
    o_ref[...] = _mm_x3(_silu(c), w_ref[...]) + b_ref[...]


def _adaln(c_all, w_ada, b_ada):
    nl, d, n = w_ada.shape
    tn = 1024
    return pl.pallas_call(
        _adaln_kernel,
        out_shape=jax.ShapeDtypeStruct((nl, MOD_ROWS, n), F32),
        grid=(nl, n // tn),
        in_specs=[pl.BlockSpec((MOD_ROWS, d), lambda l, j: (0, 0)),
                  pl.BlockSpec((None, d, tn), lambda l, j: (l, 0, j)),
                  pl.BlockSpec((None, 1, tn), lambda l, j: (l, 0, j))],
        out_specs=pl.BlockSpec((None, MOD_ROWS, tn), lambda l, j: (l, 0, j)),
        compiler_params=_cparams(("parallel", "parallel")),
        name="adaln",
    )(c_all, w_ada, b_ada.reshape(nl, 1, n))


def _rms(x, g):
    return x * lax.rsqrt(jnp.mean(x * x, axis=-1, keepdims=True) + NORM_EPS) * g


def _prenorm_kernel(x_ref, o0_ref, o1_ref, mprev_ref, mcur_ref, g_ref, *outs, final):
    x = x_ref[...] + mprev_ref[0, 5:6, :] * (o0_ref[...] + o1_ref[...])
    y = _rms(x, g_ref[...])
    if final:
        outs[0][...] = y
        return
    outs[0][...] = x
    outs[1][...] = (y * (1.0 + mcur_ref[0, 1:2, :]) + mcur_ref[0, 0:1, :]).astype(outs[1].dtype)


def _prenorm_first_kernel(xl_ref, xc_ref, mcur_ref, g_ref, xs_o, h_o, *, n_lat_tiles):
    x = jnp.where(pl.program_id(0) < n_lat_tiles, xl_ref[...], xc_ref[...])
    xs_o[...] = x
    h_o[...] = (_rms(x, g_ref[...]) * (1.0 + mcur_ref[0, 1:2, :]) + mcur_ref[0, 0:1, :]).astype(h_o.dtype)


def _mod_row_map(n_lat_tiles, tiles_per_batch, n_batch):
    return lambda i: (jnp.where(i < n_lat_tiles, i // tiles_per_batch, n_batch), 0, 0)


def _prenorm_first(x_lat, x_ctx, mods_cur, g, *, tm, n_lat_tiles, tiles_per_batch, n_batch):
    d = x_lat.shape[1]
    n_rows = x_lat.shape[0] + x_ctx.shape[0]
    row = pl.BlockSpec((tm, d), lambda i: (i, 0))
    return pl.pallas_call(
        functools.partial(_prenorm_first_kernel, n_lat_tiles=n_lat_tiles),
        out_shape=[jax.ShapeDtypeStruct((n_rows, d), F32), jax.ShapeDtypeStruct((n_rows, d), BF16)],
        grid=(n_rows // tm,),
        in_specs=[pl.BlockSpec((tm, d), lambda i: (jnp.minimum(i, n_lat_tiles - 1), 0)),
                  pl.BlockSpec((tm, d), lambda i: (jnp.maximum(i - n_lat_tiles, 0), 0)),
                  pl.BlockSpec((1, 6, d), _mod_row_map(n_lat_tiles, tiles_per_batch, n_batch)),
                  pl.BlockSpec((1, d), lambda i: (0, 0))],
        out_specs=[row, row],
        compiler_params=_cparams(("parallel",)),
        name="prenorm_first",
    )(x_lat, x_ctx, mods_cur, g.reshape(1, d))


def _prenorm(x, moe_out, mods_prev, mods_cur, g, *, tm, n_lat_tiles, tiles_per_batch, n_batch, n_rows, final=False):
    d = x.shape[1]
    row = pl.BlockSpec((tm, d), lambda i: (i, 0))
    mspec = pl.BlockSpec((1, 6, d), _mod_row_map(n_lat_tiles, tiles_per_batch, n_batch))
    slot1 = n_rows // tm
    out_shape = [jax.ShapeDtypeStruct((n_rows, d), F32)]
    if not final:
        out_shape.append(jax.ShapeDtypeStruct((n_rows, d), BF16))
    return pl.pallas_call(
        functools.partial(_prenorm_kernel, final=final),
        out_shape=out_shape,
        grid=(n_rows // tm,),
        in_specs=[row, row, pl.BlockSpec((tm, d), lambda i: (slot1 + i, 0)), mspec, mspec,
                  pl.BlockSpec((1, d), lambda i: (0, 0))],
        out_specs=[row] * len(out_shape),
        compiler_params=_cparams(("parallel",)),
        name="prenorm_final" if final else "prenorm",
    )(x, moe_out, moe_out, mods_prev, mods_cur, g.reshape(1, d))


def _matmul_kernel(a_ref, w_ref, o_ref):
    o_ref[...] = jnp.dot(a_ref[...], w_ref[...], preferred_element_type=F32).astype(o_ref.dtype)


def _matmul(a, w, out_dtype, *, tn, row0=0, n_rows=None):
    k = a.shape[1]
    n = w.shape[1]
    m = a.shape[0] if n_rows is None else n_rows
    tm = next(c for c in (1024, 512, 256, 128, 64) if m % c == 0 and row0 % c == 0)
    i0 = row0 // tm
    return pl.pallas_call(
        _matmul_kernel,
        out_shape=jax.ShapeDtypeStruct((m, n), out_dtype),
        grid=(m // tm, n // tn),
        in_specs=[pl.BlockSpec((tm, k), lambda i, j: (i + i0, 0)),
                  pl.BlockSpec((k, tn), lambda i, j: (0, j))],
        out_specs=pl.BlockSpec((tm, tn), lambda i, j: (i, j)),
        compiler_params=_cparams(("parallel", "parallel")),
        name="in_proj",
    )(a, w)


def _seq_flags(i, n_lat_tiles, tps_lat, tps_ctx):
    is_lat = i < n_lat_tiles
    j = jnp.where(is_lat, i % tps_lat, (i - n_lat_tiles) % tps_ctx)
    return j == 0, j == jnp.where(is_lat, tps_lat - 1, tps_ctx - 1)


def _neighbours(ref, prev_ref, next_ref, c0, c1, first, last):
    tm = ref.shape[0]
    p = ref[:, c0:c1]
    rows = lax.broadcasted_iota(jnp.int32, (tm, 1), 0)
    pr = jnp.where(first, 0.0, prev_ref[7:8, c0:c1])
    nx = jnp.where(last, 0.0, next_ref[0:1, c0:c1])
    prev = jnp.where(rows == 0, pr, pltpu.roll(p, 1, 0))
    nxt = jnp.where(rows == tm - 1, nx, pltpu.roll(p, tm - 1, 0))
    return p, prev, nxt


def _halo_specs(tm, cols, n_rows):
    r8 = tm // 8
    last8 = n_rows // 8 - 1
    return [pl.BlockSpec((tm, cols), lambda i: (i, 0)),
            pl.BlockSpec((8, cols), lambda i: (jnp.maximum(i * r8 - 1, 0), 0)),
            pl.BlockSpec((8, cols), lambda i: (jnp.minimum((i + 1) * r8, last8), 0))]


def _rwkv_prep_kernel(p_ref, pp_ref, pn_ref, mu_ref, w0_ref, w2_ref, a0_ref, a2_ref, g2_ref, kk_ref, ka_ref, rk_ref,
                      r_o, v_o, kap_o, lw_o, b_o, kd_o, bonus_o, gate_o, *, n_lat_tiles, tps_lat, tps_ctx):
    first, last = _seq_flags(pl.program_id(0), n_lat_tiles, tps_lat, tps_ctx)

    def shifted(c0, c1):
        p, prev, nxt = _neighbours(p_ref, pp_ref, pn_ref, c0, c1, first, last)
        mu0, mu1 = mu_ref[0:1, c0:c1], mu_ref[1:2, c0:c1]
        return (1.0 - mu0 - mu1) * p + mu0 * prev + mu1 * nxt

    pair = lambda ref, *idx: (ref[idx + (0,)], ref[idx + (1,)])

    r = shifted(0, RWKV_DIM)
    k = shifted(RWKV_DIM, 2 * RWKV_DIM)
    v = shifted(2 * RWKV_DIM, 3 * RWKV_DIM)
    kk = k * kk_ref[...]
    kap = kk / jnp.maximum(jnp.sqrt(_head_sum64(kk * kk)), 1e-12)
    r_o[...] = r.astype(r_o.dtype)
    v_o[...] = v.astype(v_o.dtype)
    kap_o[...] = kap.astype(kap_o.dtype)
    bonus_o[...] = _head_sum64(r * k * rk_ref[...]) * v
    txw = jnp.tanh(shifted(RW_XW, RW_XA))
    xa = shifted(RW_XA, RW_XG)
    for d in range(2):
        w_log = -_softplus(-(w0_ref[d:d + 1, :] + _mm_x3(txw, pair(w2_ref, d)))) - 0.5
        lw_o[d] = -jnp.exp(w_log)
        a = _sigmoid(a0_ref[d:d + 1, :] + _mm_x3(xa, pair(a2_ref, d)))
        kd_o[d] = (k * (1.0 + (a - 1.0) * ka_ref[...])).astype(kd_o.dtype)
        b_o[d] = (kap * a).astype(b_o.dtype)
    gate_o[...] = _mm_x3(_sigmoid(shifted(RW_XG, RW_COLS)), pair(g2_ref))


def _rwkv_prep(p_rw, mu, w0, w2, a0, a2, g2, k_k, k_a, r_k, *, tm, n_batch, n_lat_tiles, tps_lat, tps_ctx):
    n = p_rw.shape[0]
    gm = _group_major(_scan_groups(n_batch), n_batch, n_lat_tiles, tps_lat, tps_ctx)
    full = lambda a: pl.BlockSpec(a.shape, lambda i: (0,) * a.ndim)
    tok = pl.BlockSpec((tm, RWKV_DIM), lambda i: (i, 0))
    row = pl.BlockSpec((tm, RWKV_DIM), lambda i: (gm(i), 0))
    row2 = pl.BlockSpec((2, tm, RWKV_DIM), lambda i: (0, gm(i), 0))
    one = lambda dt: jax.ShapeDtypeStruct((n, RWKV_DIM), dt)
    two = lambda dt: jax.ShapeDtypeStruct((2, n, RWKV_DIM), dt)
    params = [mu, w0, w2, a0, a2, g2, k_k, k_a, r_k]
    return pl.pallas_call(
        functools.partial(_rwkv_prep_kernel, n_lat_tiles=n_lat_tiles, tps_lat=tps_lat, tps_ctx=tps_ctx),
        out_shape=[one(BF16), one(BF16), one(BF16), two(F32), two(BF16), two(BF16), one(F32), one(F32)],
        grid=(n // tm,),
        in_specs=_halo_specs(tm, RW_COLS, n) + [full(a) for a in params],
        out_specs=[row, row, row, row2, row2, row2, tok, tok],
        compiler_params=_cparams(("parallel",)),
        name="rwkv_prep",
    )(p_rw, p_rw, p_rw, *params)


def _order_masks(n, rev):
    ri = lax.broadcasted_iota(jnp.int32, (n, n), 0)
    ci = lax.broadcasted_iota(jnp.int32, (n, n), 1)
    if rev:
        return ci >= ri, ci > ri, ri == ci
    return ci <= ri, ci < ri, ri == ci


HEADS_PER_GROUP = 4
GROUP = HEADS_PER_GROUP * RWKV_HEAD


def _head_block_mask():
    ri = lax.broadcasted_iota(jnp.int32, (GROUP, GROUP), 0) // RWKV_HEAD
    ci = lax.broadcasted_iota(jnp.int32, (GROUP, GROUP), 1) // RWKV_HEAD
    return ri == ci


def _block_diag(y_g, bdm):
    return jnp.where(bdm, jnp.concatenate([y_g] * HEADS_PER_GROUP, axis=0), jnp.zeros((), y_g.dtype))


def _head_prod(x, y, bdm, nt=False):
    xb, yb = x.astype(BF16), y.astype(BF16)
    dims = (((1,), (1,)), ((), ())) if nt else (((1,), (0,)), ((), ()))
    outs = [lax.dot_general(xb[:, g:g + GROUP], _block_diag(yb[:, g:g + GROUP], bdm), dims,
                            preferred_element_type=F32) for g in range(0, x.shape[1], GROUP)]
    return jnp.concatenate(outs, axis=1)


def _head_prod_tn(a, b, bdm):
    ab, bb = a.astype(BF16), b.astype(BF16)
    outs = []
    for g in range(0, a.shape[1], GROUP):
        p = lax.dot_general(ab[:, g:g + GROUP], bb[:, g:g + GROUP], (((0,), (0,)), ((), ())),
                            preferred_element_type=F32)
        p = jnp.where(bdm, p, 0.0)
        outs.append((p[0:64] + p[64:128]) + (p[128:192] + p[192:256]))
    return jnp.concatenate(outs, axis=1)


def _rwkv_chunks(chains):
    row = lax.broadcasted_iota(jnp.int32, (CHUNK, RWKV_DIM), 0)
    col = lax.broadcasted_iota(jnp.int32, (CHUNK, RWKV_DIM), 1) % RWKV_HEAD
    eye_f = (col == row).astype(F32)
    bdm = _head_block_mask()
    prod = functools.partial(_head_prod, bdm=bdm)
    each = lambda f, *cols: [f(*a) for a in zip(*cols)]
    top, bottom = (lambda a: a[:CHUNK]), (lambda a: a[CHUNK:])
    stack = lambda a, b: jnp.concatenate([a, b], axis=0)

    incl, strict, kr, bt, kdt, vs, e_tot = [], [], [], [], [], [], []
    for r_ref, v_ref, kap_ref, lw_ref, b_ref, kd_ref, _, _, rev in chains:
        incl.append(col >= row if rev else col <= row)
        strict.append(col > row if rev else col < row)
        lw = lw_ref[...]
        gam = _mm_mask(_order_masks(CHUNK, rev)[0].astype(BF16), lw)
        e_tot.append(jnp.exp(jnp.sum(lw, axis=0, keepdims=True)))
        e_neg = jnp.exp(-gam)
        kr.append(stack(kap_ref[...] * jnp.exp(gam - lw), r_ref[...] * jnp.exp(gam)))
        bt.append(b_ref[...] * e_neg)
        kdt.append(kd_ref[...] * e_neg)
        vs.append(v_ref[...])
    g_b = each(lambda a, b: prod(a, b, nt=True), kr, bt)
    g_k = each(lambda a, b: prod(a, b, nt=True), kr, kdt)
    n_ab = each(lambda m, g: jnp.where(m, top(g), 0.0), strict, g_b)
    a_kv = each(lambda m, g: jnp.where(m, top(g), 0.0), strict, g_k)
    m_b = each(lambda m, g: jnp.where(m, bottom(g), 0.0), incl, g_b)
    m_kd = each(lambda m, g: jnp.where(m, bottom(g), 0.0), incl, g_k)
    t_inv = each(lambda n: eye_f - n, n_ab)
    pw = each(lambda n: prod(n, n), n_ab)
    for _ in range(4):
        both = each(lambda t, p: prod(stack(t, p), p), t_inv, pw)
        t_inv = each(lambda t, b: t + top(b), t_inv, both)
        pw = each(bottom, both)
    t_inv = each(lambda t, p: t + prod(t, p), t_inv, pw)
    av = each(lambda a, m, v: prod(stack(a, m), v), a_kv, m_kd, vs)
    s = [c[7][...] for c in chains]
    ks = each(lambda a, b: prod(a, b, nt=True), kr, s)
    u = each(lambda t, k, a: prod(t, top(k) + top(a)), t_inv, ks, av)
    mbu = each(prod, m_b, u)
    upd = each(lambda v, uu, kd, b: _head_prod_tn(stack(v, uu), stack(kd, -b), bdm), vs, u, kdt, bt)
    for c, k, a, m, s0, up, e in zip(chains, ks, av, mbu, s, upd, e_tot):
        c[6][...] = bottom(k) + bottom(a) - m
        c[7][...] = (s0 + up) * e


def _rwkv_scan_kernel(rf_ref, vf_ref, kapf_ref, lwf_ref, bf_ref, kdf_ref, rr_ref, vr_ref, kapr_ref, lwr_ref, br_ref,
                      kdr_ref, yf_ref, yr_ref, sf_ref, sr_ref):
    @pl.when(pl.program_id(1) == 0)
    def _():
        sf_ref[...] = jnp.zeros_like(sf_ref)
        sr_ref[...] = jnp.zeros_like(sr_ref)

    chains = []
    for p in range(rf_ref.shape[0]):
        chains.append((rf_ref.at[p], vf_ref.at[p], kapf_ref.at[p], lwf_ref.at[p], bf_ref.at[p], kdf_ref.at[p],
                       yf_ref.at[p], sf_ref.at[p], False))
        chains.append((rr_ref.at[p], vr_ref.at[p], kapr_ref.at[p], lwr_ref.at[p], br_ref.at[p], kdr_ref.at[p],
                       yr_ref.at[p], sr_ref.at[p], True))
    _rwkv_chunks(chains)


def _scan_block_map(rev, n_lat_blocks, nb_lat, nb_ctx):
    def f(b, s):
        c_ctx = (nb_ctx - 1 - s) if rev else s
        c_lat = (nb_lat - 1 - (s - nb_ctx)) if rev else (s - nb_ctx)
        return jnp.where(s < nb_ctx, n_lat_blocks + b * nb_ctx + c_ctx, b * nb_lat + c_lat)
    return f


def _scan_groups(n_batch):
    return 2 if n_batch % 2 == 0 else 1


def _group_major(n_groups, n_batch, n_lat_tiles, tps_lat, tps_ctx):
    per = n_batch // n_groups
    lat, ctx = per * tps_lat, per * tps_ctx

    def f(i):
        is_lat = i < n_lat_tiles
        b = jnp.where(is_lat, i // tps_lat, (i - n_lat_tiles) // tps_ctx)
        j = jnp.where(is_lat, i % tps_lat, (i - n_lat_tiles) % tps_ctx)
        return (b // per) * (lat + ctx) + jnp.where(is_lat, (b % per) * tps_lat + j, lat + (b % per) * tps_ctx + j)
    return f


def _rwkv_scan(r, v, kap, lw, bb, kd, *, n_batch, nc_lat, nc_ctx):
    n = r.shape[0]
    ng = _scan_groups(n_batch)
    per = n_batch // ng
    g3 = lambda a: a.reshape(ng, n // ng, RWKV_DIM)
    g4 = lambda a: a.reshape(2, ng, n // ng, RWKV_DIM)
    specs = []
    for d, rev in enumerate((False, True)):
        blk = _scan_block_map(rev, per * nc_lat, nc_lat, nc_ctx)
        row = pl.BlockSpec((ng, CHUNK, RWKV_DIM), lambda b, s, blk=blk: (0, blk(b, s), 0))
        rowd = pl.BlockSpec((None, ng, CHUNK, RWKV_DIM), lambda b, s, blk=blk, d=d: (d, 0, blk(b, s), 0))
        specs.append((row, rowd))
    (rowf, rowdf), (rowr, rowdr) = specs
    out = jax.ShapeDtypeStruct((ng, n // ng, RWKV_DIM), F32)
    state = pltpu.VMEM((ng, RWKV_HEAD, RWKV_DIM), F32)
    yf, yr = pl.pallas_call(
        _rwkv_scan_kernel,
        out_shape=[out, out],
        grid=(per, nc_ctx + nc_lat),
        in_specs=[rowf, rowf, rowf, rowdf, rowdf, rowdf, rowr, rowr, rowr, rowdr, rowdr, rowdr],
        out_specs=[rowf, rowr],
        scratch_shapes=[state, state],
        compiler_params=_cparams(("parallel", "arbitrary")),
        name="rwkv_scan",
    )(g3(r), g3(v), g3(kap), g4(lw), g4(bb), g4(kd), g3(r), g3(v), g3(kap), g4(lw), g4(bb), g4(kd))
    return yf.reshape(n, RWKV_DIM), yr.reshape(n, RWKV_DIM)


def _gla_blocks(dirs, n_rows, cw_ref):
    rows = lax.broadcasted_iota(jnp.int32, (n_rows, 1), 0)
    for d in dirs:
        for c0 in range(0, GLA_QKV, 512):
            c1 = c0 + 512
            p = d["load"](c0, c1)
            prev = jnp.where(rows == 0, d["prev_row"](c0, c1), pltpu.roll(p, 1, 0))
            nxt = jnp.where(rows == n_rows - 1, d["next_row"](c0, c1), pltpu.roll(p, n_rows - 1, 0))
            act = _silu(cw_ref[0:1, c0:c1] * prev + cw_ref[1:2, c0:c1] * p + cw_ref[2:3, c0:c1] * nxt)
            if c0 == 0:
                act = act * (GLA_DK ** -0.5)
            d["qkv"][0:n_rows, c0:c1] = act
        z = _mm_x3(d["load"](GLA_QKV, GLA_COLS), d["aw"][...]) + d["ab"][...]
        d["g"][0:n_rows, :] = -_softplus(-z) * (1.0 / GLA_TAU)

    each = lambda f, *cols: [f(*a) for a in zip(*cols)]
    incl = [_order_masks(CHUNK, d["rev"])[0] for d in dirs]
    incl_b = [m.astype(BF16) for m in incl]
    n_chunks = n_rows // CHUNK
    for step in range(n_chunks):
        rs = [slice(c * CHUNK, (c + 1) * CHUNK) for c in ((n_chunks - 1 - step) if d["rev"] else step for d in dirs)]
        g = each(lambda d, r: d["g"][r, :], dirs, rs)
        gc = each(_mm_mask, incl_b, g)
        gl = each(lambda a: jnp.sum(a, axis=0, keepdims=True), g)
        q = each(lambda d, r: d["qkv"][r, 0:GLA_KDIM], dirs, rs)
        k = each(lambda d, r: d["qkv"][r, GLA_KDIM:2 * GLA_KDIM], dirs, rs)
        qg = each(lambda a, c: a * jnp.exp(c), q, gc)
        kg = each(lambda a, c: a * jnp.exp(-c), k, gc)
        ke = each(lambda a, l, c: a * jnp.exp(l - c), k, gl, gc)
        e_l = each(jnp.exp, gl)
        for h in range(GLA_HEADS):
            ks = slice(h * GLA_DK, (h + 1) * GLA_DK)
            vs = slice(2 * GLA_KDIM + h * GLA_DV, 2 * GLA_KDIM + (h + 1) * GLA_DV)
            v_h = each(lambda d, r: d["qkv"][r, vs], dirs, rs)
            att = each(lambda m, a, b: jnp.where(m, _mm_nt(a[:, ks], b[:, ks]), 0.0), incl, qg, kg)
            st = [d["st"][h] for d in dirs]
            out = each(lambda a, v, qq, s0: _mm(a, v) + _mm_nt(qq[:, ks], s0), att, v_h, qg, st)
            new = each(lambda s0, e, v, kk: s0 * e[:, ks] + _mm_tn(v, kk[:, ks]), st, e_l, v_h, ke)
            for d, r, o_h, s1 in zip(dirs, rs, out, new):
                d["o"][r, h * GLA_DV:(h + 1) * GLA_DV] = o_h
                d["st"][h] = s1


def _gla_kernel(uc_ref, ulf_ref, ulfp_ref, ulfn_ref, ulr_ref, ulrp_ref, ulrn_ref, cw_ref, awf_ref, abf_ref, awr_ref,
                abr_ref, ocf_ref, ocr_ref, olf_ref, olr_ref, stf_ref, str_ref, qkvf_s, qkvr_s, gf_s, gr_s, of_s,
                or_s, *, n_wg):
    s = pl.program_id(1)
    grid_rows, wg_cols = ulf_ref.shape[0], ulf_ref.shape[1]
    zero_row = lambda c0, c1: jnp.zeros((1, c1 - c0), F32)
    fwd = dict(aw=awf_ref, ab=abf_ref, st=stf_ref, qkv=qkvf_s, g=gf_s, o=of_s, rev=False)
    bwd = dict(aw=awr_ref, ab=abr_ref, st=str_ref, qkv=qkvr_s, g=gr_s, o=or_s, rev=True)

    @pl.when(s == 0)
    def _():
        stf_ref[...] = jnp.zeros_like(stf_ref)
        str_ref[...] = jnp.zeros_like(str_ref)
        n = uc_ref.shape[0]
        ctx = dict(load=lambda c0, c1: uc_ref[:, c0:c1], prev_row=zero_row, next_row=zero_row)
        _gla_blocks([{**fwd, **ctx}, {**bwd, **ctx}], n, cw_ref)
        ocf_ref[...] = of_s[0:n, :]
        ocr_ref[...] = or_s[0:n, :]

    @pl.when(s > 0)
    def _():
        n = grid_rows * wg_cols

        def lat(u_ref, up_ref, un_ref, wg):
            hr = up_ref.shape[0]
            return dict(
                load=lambda c0, c1: jnp.concatenate([u_ref[:, w, c0:c1] for w in range(wg_cols)], axis=0),
                prev_row=lambda c0, c1: jnp.where(wg == 0, 0.0, up_ref[hr - 1:hr, 7, c0:c1]),
                next_row=lambda c0, c1: jnp.where(wg == n_wg - 1, 0.0, un_ref[0:1, 0, c0:c1]))

        _gla_blocks([{**fwd, **lat(ulf_ref, ulfp_ref, ulfn_ref, s - 1)},
                     {**bwd, **lat(ulr_ref, ulrp_ref, ulrn_ref, n_wg - s)}], n, cw_ref)
        for w in range(wg_cols):
            olf_ref[:, w, :] = of_s[w * grid_rows:(w + 1) * grid_rows, :]
            olr_ref[:, w, :] = or_s[w * grid_rows:(w + 1) * grid_rows, :]


def _gla_scan(u_ctx, u_lat, conv_w, alpha_w, alpha_b, *, n_batch):
    t_ctx, ch = u_ctx.shape
    lc = t_ctx // n_batch
    _, grid_rows, grid_w, _ = u_lat.shape
    wg_cols = min(grid_w, max(8, 256 // grid_rows))
    n_wg = grid_w // wg_cols
    assert (wg_cols * grid_rows) % CHUNK == 0 and lc % CHUNK == 0 and grid_w % wg_cols == 0 and wg_cols % 8 == 0
    hr = 8 if grid_rows % 8 == 0 else grid_rows
    n_rows = max(lc, wg_cols * grid_rows)

    def group(rev):
        return lambda s: jnp.clip((n_wg - s) if rev else (s - 1), 0, n_wg - 1)

    full = lambda a: pl.BlockSpec(a.shape, lambda b, s: (0,) * a.ndim)
    lat = lambda c, wg: pl.BlockSpec((None, grid_rows, wg_cols, c), lambda b, s: (b, 0, wg(s), 0))

    def lat_in(wg):
        return [lat(ch, wg),
                pl.BlockSpec((None, hr, 8, ch), lambda b, s: (
                    b, grid_rows // hr - 1, jnp.maximum(wg(s) * (wg_cols // 8) - 1, 0), 0)),
                pl.BlockSpec((None, hr, 8, ch), lambda b, s: (
                    b, 0, jnp.minimum((wg(s) + 1) * (wg_cols // 8), grid_w // 8 - 1), 0))]

    ctx_out = jax.ShapeDtypeStruct((t_ctx, GLA_VDIM), F32)
    lat_out = jax.ShapeDtypeStruct((n_batch, grid_rows, grid_w, GLA_VDIM), F32)
    ctx_spec = pl.BlockSpec((lc, GLA_VDIM), lambda b, s: (b, 0))
    per_dir = lambda shape: [pltpu.VMEM(shape, F32)] * 2
    params = [conv_w, alpha_w[0], alpha_b[0], alpha_w[1], alpha_b[1]]
    return pl.pallas_call(
        functools.partial(_gla_kernel, n_wg=n_wg),
        out_shape=[ctx_out, ctx_out, lat_out, lat_out],
        grid=(n_batch, 1 + n_wg),
        in_specs=[pl.BlockSpec((lc, ch), lambda b, s: (b, 0))] + lat_in(group(False)) + lat_in(group(True))
        + [full(a) for a in params],
        out_specs=[ctx_spec, ctx_spec, lat(GLA_VDIM, group(False)), lat(GLA_VDIM, group(True))],
        scratch_shapes=per_dir((GLA_HEADS, GLA_DV, GLA_DK)) + per_dir((n_rows, GLA_QKV))
        + per_dir((n_rows, GLA_KDIM)) + per_dir((n_rows, GLA_VDIM)),
        compiler_params=_cparams(("parallel", "arbitrary")),
        name="gla_scan",
    )(u_ctx, u_lat, u_lat, u_lat, u_lat, u_lat, u_lat, *params)


def _pair_tables():
    ri = lax.broadcasted_iota(jnp.int32, (LANES, LANES), 0)
    ci = lax.broadcasted_iota(jnp.int32, (LANES, LANES), 1)
    grp = ci // 6
    pr = ci % 6
    lo = jnp.where(pr < 3, 0, jnp.where(pr < 5, 1, 2))
    hi = jnp.where(pr < 3, pr + 1, jnp.where(pr < 5, pr - 1, 3))
    member = ((ri == grp * EXPERTS_PER_GROUP + lo) | (ri == grp * EXPERTS_PER_GROUP + hi)) & (ci < N_PAIRS)
    return member


def _route(h, rw_ref, rb_ref):
    lane = lax.broadcasted_iota(jnp.int32, (h.shape[0], LANES), 1)
    scores = _sigmoid(_mm_x3(h, (rw_ref[0], rw_ref[1])))
    sel = jnp.where(lane < N_EXPERTS, scores + rb_ref[...], 0.0)
    member = _pair_tables()
    s3 = _split3(sel)
    mb = member.astype(BF16)
    d = functools.partial(jnp.dot, preferred_element_type=F32)
    ps = d(s3[0], mb) + (d(s3[1], mb) + d(s3[2], mb))
    ps = jnp.where(lane < N_PAIRS, ps, -jnp.inf)
    best = jnp.max(ps, axis=-1, keepdims=True)
    bucket = jnp.min(jnp.where(ps == best, lane, LANES), axis=-1, keepdims=True)
    onehot = (lane == bucket).astype(BF16)
    chosen = lax.dot_general(onehot, mb, (((1,), (1,)), ((), ())), preferred_element_type=F32) > 0.5
    picked = jnp.where(chosen, scores, 0.0)
    gates = picked / jnp.sum(picked, axis=-1, keepdims=True)
    e_lo = jnp.min(jnp.where(chosen, lane, LANES), axis=-1, keepdims=True)
    e_hi = jnp.max(jnp.where(chosen, lane, -1), axis=-1, keepdims=True)
    w_lo = jnp.sum(jnp.where(lane == e_lo, gates, 0.0), axis=-1, keepdims=True)
    w_hi = jnp.sum(jnp.where(lane == e_hi, gates, 0.0), axis=-1, keepdims=True)
    out = jnp.where(lane == 0, e_lo.astype(F32), 0.0)
    out = jnp.where(lane == 1, e_hi.astype(F32), out)
    out = jnp.where(lane == 2, w_lo, out)
    return jnp.where(lane == 3, w_hi, out)


def _merge_kernel(x_ref, y0_ref, y1_ref, bonus_ref, gate_ref, ol0_ref, ol1_ref, oc0_ref, oc1_ref, pg_ref, mods_ref,
                  lng_ref, lnb_ref, gng_ref, wa_ref, wb_ref, wo_ref, nfg_ref, rw_ref, rb_ref,
                  xs_o, h_o, route_o, *, n_lat_tiles):
    y = y0_ref[...] + y1_ref[...]
    mean = _head_sum64(y) * (1.0 / RWKV_HEAD)
    yc = y - mean
    var = _head_sum64(yc * yc) * (1.0 / RWKV_HEAD)
    ya = (yc * lax.rsqrt(var + RWKV_GN_EPS) * lng_ref[...] + lnb_ref[...] + bonus_ref[...]) * gate_ref[...]
    o = jnp.where(pl.program_id(0) < n_lat_tiles, ol0_ref[...] + ol1_ref[...], oc0_ref[...] + oc1_ref[...])
    parts = []
    for h in range(GLA_HEADS):
        o_h = o[:, h * GLA_DV:(h + 1) * GLA_DV]
        parts.append(o_h * lax.rsqrt(jnp.mean(o_h * o_h, axis=-1, keepdims=True) + NORM_EPS) * gng_ref[...])
    yb = jnp.concatenate(parts, axis=1) * _silu(pg_ref[:, 0:GLA_VDIM].astype(F32))
    ga = _sigmoid(pg_ref[:, GLA_VDIM:GLA_VDIM + D_MODEL].astype(F32))
    gb = _sigmoid(pg_ref[:, GLA_VDIM + D_MODEL:GATE_COLS].astype(F32))
    z = ga * _mm(ya, wa_ref[...]) + gb * _mm(yb, wb_ref[...])
    xs = x_ref[...] + mods_ref[0, 2:3, :] * _mm(z, wo_ref[...])
    xs_o[...] = xs
    h = _rms(xs, nfg_ref[...]) * (1.0 + mods_ref[0, 4:5, :]) + mods_ref[0, 3:4, :]
    h_o[...] = h
    route_o[...] = _route(h, rw_ref, rb_ref)


def _merge(x, y0, y1, bonus, gate, o_lat, o_ctx, pg, mods, ln_g, ln_b, gn_g, wa, wb, wo, nf_g, rw, rb,
           *, tm, n_rows, n_lat_tiles, tiles_per_batch, n_batch, tps_ctx):
    d = D_MODEL
    mrow = _mod_row_map(n_lat_tiles, tiles_per_batch, n_batch)
    gm = _group_major(_scan_groups(n_batch), n_batch, n_lat_tiles, tiles_per_batch, tps_ctx)
    row = lambda c: pl.BlockSpec((tm, c), lambda i: (i, 0))
    yrow = pl.BlockSpec((tm, RWKV_DIM), lambda i: (gm(i), 0))
    lat = pl.BlockSpec((tm, GLA_VDIM), lambda i: (jnp.minimum(i, n_lat_tiles - 1), 0))
    ctx = pl.BlockSpec((tm, GLA_VDIM), lambda i: (jnp.maximum(i - n_lat_tiles, 0), 0))
    full = lambda a: pl.BlockSpec(a.shape, lambda i: (0,) * a.ndim)
    consts = [ln_g.reshape(1, -1), ln_b.reshape(1, -1), gn_g.reshape(1, -1), wa, wb, wo, nf_g.reshape(1, -1), rw, rb]
    return pl.pallas_call(
        functools.partial(_merge_kernel, n_lat_tiles=n_lat_tiles),
        out_shape=[jax.ShapeDtypeStruct((n_rows, d), F32), jax.ShapeDtypeStruct((n_rows, d), F32),
                   jax.ShapeDtypeStruct((n_rows, LANES), F32)],
        grid=(n_rows // tm,),
        in_specs=[row(d), yrow, yrow, row(RWKV_DIM), row(RWKV_DIM), lat, lat, ctx, ctx,
                  row(GATE_COLS), pl.BlockSpec((1, 6, d), mrow)] + [full(a) for a in consts],
        out_specs=[row(d), row(d), row(LANES)],
        compiler_params=_cparams(("parallel",)),
        name="merge",
    )(x, y0, y1, bonus, gate, o_lat[0], o_lat[1], o_ctx[0], o_ctx[1], pg, mods, *consts)


DMA_ISSUE_UNROLL = 8
MOE_UP_CHUNK = 256


def _moe_kernel(te_ref, nu_ref, src_ref, src_next_ref, dst_prev_ref, dst_ref, w_ref, h_hbm, wg_ref, wu_ref, wd_ref,
                o_hbm, xbuf, obuf, act_s, sem_in, sem_out, sem_fill, *, n_real_rows):
    del te_ref
    tm = xbuf.shape[1]
    i = pl.program_id(0)
    n_used = nu_ref[0]
    slot = i % 2

    other = 1 - slot

    def gather_row(idx_ref, s, r):
        return pltpu.make_async_copy(h_hbm.at[pl.ds(idx_ref[0, 0, r], 1)], xbuf.at[s, pl.ds(r, 1)], sem_in.at[s])

    def scatter_row(idx_ref, s, r):
        return pltpu.make_async_copy(obuf.at[s, pl.ds(r, 1)], o_hbm.at[pl.ds(idx_ref[0, 0, r], 1)], sem_out.at[s])

    def wait_gather(s):
        pltpu.make_async_copy(h_hbm.at[pl.ds(0, tm)], xbuf.at[s], sem_in.at[s]).wait()

    def wait_scatter(s):
        pltpu.make_async_copy(obuf.at[s], o_hbm.at[pl.ds(0, tm)], sem_out.at[s]).wait()

    def issue_all(make_row):
        def body(r, carry):
            make_row(r).start()
            return carry
        lax.fori_loop(0, tm, body, 0, unroll=DMA_ISSUE_UNROLL)

    @pl.when(i < n_used)
    def _():
        @pl.when(i == 0)
        def _():
            obuf[1] = jnp.zeros(obuf.shape[1:], obuf.dtype)
            fills = [pltpu.make_async_copy(obuf.at[1], o_hbm.at[pl.ds(n_real_rows + k * tm, tm)], sem_fill)
                     for k in range((o_hbm.shape[0] - n_real_rows) // tm)]
            for f in fills:
                f.start()
            for f in fills:
                f.wait()
            issue_all(lambda r: gather_row(src_ref, 0, r))

        @pl.when(i >= 1)
        def _():
            wait_scatter(slot)

        wait_gather(slot)
        bounds = list(range(0, D_EXPERT, MOE_UP_CHUNK)) + [D_EXPERT]
        n_chunks = len(bounds) - 1
        def up(c):
            xb = xbuf[slot].astype(BF16)
            cols = slice(bounds[c], bounds[c + 1])
            return (jnp.dot(xb, wg_ref[:, cols], preferred_element_type=F32),
                    jnp.dot(xb, wu_ref[:, cols], preferred_element_type=F32))

        g, u = up(0)
        for c in range(n_chunks):
            g_next, u_next = up(c + 1) if c + 1 < n_chunks else (None, None)
            act_s[:, bounds[c]:bounds[c + 1]] = (_silu(g) * u).astype(BF16)
            g, u = g_next, u_next
            for r in range(c * tm // n_chunks, (c + 1) * tm // n_chunks):
                gather_row(src_next_ref, other, r).start()
                scatter_row(dst_prev_ref, other, r).start()

        obuf[slot] = jnp.dot(act_s[...], wd_ref[...], preferred_element_type=F32) * w_ref[...][:, 0:1]

        @pl.when(i == n_used - 1)
        def _():
            issue_all(lambda r: scatter_row(dst_ref, slot, r))
            wait_scatter(other)
            wait_scatter(slot)
            wait_gather(other)


def _moe(h, route, wg, wu, wd, *, tm):
    t, d = h.shape
    i32 = jnp.int32
    ids = jnp.concatenate([route[:, 0], route[:, 1]]).astype(i32)
    wts = jnp.concatenate([route[:, 2], route[:, 3]])
    n_tiles = (2 * t) // tm + N_EXPERTS
    n_spill = (N_EXPERTS + 1) * tm
    _, order, wts_s = lax.sort((ids, jnp.arange(2 * t, dtype=i32), wts), num_keys=1, is_stable=True)
    counts = jnp.sum((ids[:, None] == jnp.arange(N_EXPERTS, dtype=i32)[None, :]).astype(i32), axis=0)
    starts = jnp.cumsum(counts) - counts
    pcounts = ((counts + tm - 1) // tm) * tm
    pend = jnp.cumsum(pcounts)
    poff = pend - pcounts
    n_used = (pend[-1] // tm).astype(i32)
    tile = jnp.arange(n_tiles, dtype=i32)
    te = jnp.sum((tile[:, None] * tm >= pend[None, :]).astype(i32), axis=1)
    te = jnp.where(tile < n_used, te, jnp.take(te, jnp.maximum(n_used - 1, 0))).astype(i32)
    rank0 = tile * tm - jnp.take(poff, te)
    n_valid = jnp.where(tile < n_used, jnp.clip(jnp.take(counts, te) - rank0, 0, tm), 0)
    n_pos = n_tiles * tm
    shift = poff - starts
    e_tile = te[:, None]

    def spread(arr):
        ext = jnp.concatenate([jnp.zeros((n_pos,), arr.dtype), arr, jnp.zeros((n_pos,), arr.dtype)])
        out = jnp.zeros((n_tiles, tm), arr.dtype)
        for e in range(N_EXPERTS):
            seg = lax.dynamic_slice(ext, (n_pos - shift[e],), (n_pos,)).reshape(n_tiles, tm)
            out = jnp.where(e_tile == e, seg, out)
        return out

    a = spread(order)
    w_tile = spread(wts_s)
    valid = jnp.arange(tm, dtype=i32)[None, :] < n_valid[:, None]
    src = jnp.where(valid, a % t, 0).astype(i32)
    pad = (~valid) & (tile < n_used)[:, None]
    spill = 2 * t + tm + jnp.minimum(jnp.cumsum(pad.reshape(-1).astype(i32)) - 1, n_spill - tm - 1).reshape(n_tiles, tm)
    dst = jnp.where(valid, a, spill).astype(i32)
    dst = jnp.concatenate([2 * t + jnp.arange(tm, dtype=i32)[None, :], dst], axis=0)
    wrow = jnp.broadcast_to(jnp.where(valid, w_tile, 0.0).reshape(-1, 1), (n_tiles * tm, LANES))

    smem = functools.partial(pl.BlockSpec, (1, 1, tm), memory_space=pltpu.SMEM)
    src3, dst3 = src.reshape(n_tiles, 1, tm), dst.reshape(n_tiles + 1, 1, tm)
    return pl.pallas_call(
        functools.partial(_moe_kernel, n_real_rows=2 * t),
        out_shape=jax.ShapeDtypeStruct((2 * t + n_spill, d), F32),
        grid_spec=pltpu.PrefetchScalarGridSpec(
            num_scalar_prefetch=2,
            grid=(n_tiles,),
            in_specs=[smem(lambda i, *_: (i, 0, 0)),
                      smem(lambda i, *_: (jnp.minimum(i + 1, n_tiles - 1), 0, 0)),
                      smem(lambda i, *_: (i, 0, 0)),
                      smem(lambda i, *_: (i + 1, 0, 0)),
                      pl.BlockSpec((tm, LANES), lambda i, *_: (i, 0)),
                      pl.BlockSpec(memory_space=pl.ANY),
                      pl.BlockSpec((None, d, D_EXPERT), lambda i, te_r, *_: (te_r[i], 0, 0)),
                      pl.BlockSpec((None, d, D_EXPERT), lambda i, te_r, *_: (te_r[i], 0, 0)),
                      pl.BlockSpec((None, D_EXPERT, d), lambda i, te_r, *_: (te_r[i], 0, 0))],
            out_specs=pl.BlockSpec(memory_space=pl.ANY),
            scratch_shapes=[pltpu.VMEM((2, tm, d), F32), pltpu.VMEM((2, tm, d), F32),
                            pltpu.VMEM((tm, D_EXPERT), BF16),
                            pltpu.SemaphoreType.DMA((2,)), pltpu.SemaphoreType.DMA((2,)), pltpu.SemaphoreType.DMA]),
        compiler_params=_cparams(("arbitrary",)),
        name="moe_ffn",
    )(te, n_used.reshape(1), src3, src3, dst3, dst3, wrow, h, wg, wu, wd)


def _pad_cols(a, n):
    return jnp.pad(a, [(0, 0)] * (a.ndim - 1) + [(0, n - a.shape[-1])])


def _pad_rows(a, n):
    return jnp.pad(a, [(0, 0)] * (a.ndim - 2) + [(0, n - a.shape[-2]), (0, 0)])


def _split_in_proj(w_in, mu):
    s = (RWKV_DIM, 2 * RWKV_DIM, 3 * RWKV_DIM, 3 * RWKV_DIM + DECAY_LORA, 3 * RWKV_DIM + DECAY_LORA + ICLR_LORA)

    def rw_layout(a):
        return jnp.concatenate([a[..., :s[2]], _pad_cols(a[..., s[2]:s[3]], LORA_PAD),
                                _pad_cols(a[..., s[3]:s[4]], LORA_PAD), a[..., s[4]:RWKV_IN]], axis=-1)

    w_in = w_in.astype(BF16)
    w_rw = rw_layout(w_in)
    w_gla = _pad_cols(w_in[:, RWKV_IN:GLA_GATE_OFF], GLA_COLS)
    w_gate = w_in[:, GLA_GATE_OFF:]
    return w_rw, w_gla, w_gate, rw_layout(mu)


def kernel(x, c, ctx, c_ctx, w_ada, b_ada, norm_mix_g, norm_ffn_g, w_in, rwkv_mu, rwkv_w0, rwkv_w2, rwkv_a0,
           rwkv_a2, rwkv_g2, rwkv_k_k, rwkv_k_a, rwkv_r_k, rwkv_ln_g, rwkv_ln_b, gla_conv, gla_alpha_w2,
           gla_alpha_b, gla_norm_g, w_branch_a, w_branch_b, w_out, router_w, router_b, exp_w_gate, exp_w_up,
           exp_w_down, final_norm_g):
    n_batch, seq, d = x.shape
    ctx_len = ctx.shape[1]
    assert d == D_MODEL and seq % GRID_W == 0 and n_batch < MOD_ROWS
    grid_rows = seq // GRID_W
    tm = min(256, ctx_len)
    assert seq % tm == 0 and ctx_len % tm == 0 and tm % CHUNK == 0
    t_lat, t_ctx = n_batch * seq, n_batch * ctx_len
    t_all = t_lat + t_ctx
    tiles = dict(tm=tm, n_lat_tiles=t_lat // tm)
    seq_tiles = dict(tps_lat=seq // tm, tps_ctx=ctx_len // tm)
    mod_tiles = dict(tiles_per_batch=seq // tm, n_batch=n_batch)

    c_all = jnp.zeros((MOD_ROWS, d), F32).at[:n_batch].set(c).at[n_batch].set(c_ctx)
    mods = _adaln(c_all, w_ada, b_ada).reshape(DEPTH, MOD_ROWS, 6, d)
    rw_pad = _presplit(_pad_cols(router_w, LANES))
    rb_pad = _pad_cols(router_b.reshape(1, -1), LANES)

    moe_out = None
    for l in range(DEPTH):
        last = l == DEPTH - 1
        if moe_out is None:
            xs, h = _prenorm_first(x.reshape(t_lat, d), ctx.reshape(t_ctx, d), mods[l], norm_mix_g[l],
                                   **tiles, **mod_tiles)
        else:
            xs, h = _prenorm(xs, moe_out, mods[l - 1], mods[l], norm_mix_g[l], n_rows=t_all, **tiles, **mod_tiles)
        w_rw, w_gla, w_gate, mu = _split_in_proj(w_in[l], rwkv_mu[l])
        p_rw = _matmul(h, w_rw, F32, tn=RW_COLS // 2)
        p_gla_lat = _matmul(h, w_gla, F32, tn=GLA_COLS, n_rows=t_lat)
        p_gla_ctx = _matmul(h, w_gla, F32, tn=GLA_COLS, row0=t_lat, n_rows=t_ctx)
        p_gate = _matmul(h, w_gate, BF16, tn=GATE_COLS // 5)

        r, v, kap, lw, bb, kd, bonus, gate = _rwkv_prep(
            p_rw, mu, rwkv_w0[l], _presplit(_pad_rows(rwkv_w2[l], LORA_PAD)), rwkv_a0[l],
            _presplit(_pad_rows(rwkv_a2[l], LORA_PAD)), _presplit(rwkv_g2[l]), rwkv_k_k[l].reshape(1, -1), rwkv_k_a[l].reshape(1, -1), rwkv_r_k[l].reshape(1, -1),
            n_batch=n_batch, **tiles, **seq_tiles)
        y0, y1 = _rwkv_scan(r, v, kap, lw, bb, kd, n_batch=n_batch, nc_lat=seq // CHUNK, nc_ctx=ctx_len // CHUNK)

        u_lat = p_gla_lat.reshape(n_batch, grid_rows, GRID_W, GLA_COLS)
        ocf, ocr, olf, olr = _gla_scan(p_gla_ctx, u_lat, gla_conv[l], _pad_rows(gla_alpha_w2[l], LANES),
                                       gla_alpha_b[l].reshape(2, 1, -1), n_batch=n_batch)
        o_ctx = [ocf, ocr]
        o_lat = [olf.reshape(t_lat, GLA_VDIM), olr.reshape(t_lat, GLA_VDIM)]

        n_rows = t_lat if last else t_all
        xs, hf, route = _merge(
            xs, y0, y1, bonus, gate, o_lat, o_ctx, p_gate, mods[l], rwkv_ln_g[l], rwkv_ln_b[l], gla_norm_g[l],
            w_branch_a[l].astype(BF16), w_branch_b[l].astype(BF16), w_out[l].astype(BF16), norm_ffn_g[l],
            rw_pad, rb_pad, n_rows=n_rows, tps_ctx=ctx_len // tm, **tiles, **mod_tiles)
        moe_out = _moe(hf, route, exp_w_gate[l].astype(BF16), exp_w_up[l].astype(BF16), exp_w_down[l].astype(BF16),
                       tm=tm)

    (out,) = _prenorm(xs, moe_out, mods[DEPTH - 1], mods[DEPTH - 1], final_norm_g, n_rows=t_lat, final=True,
                      **tiles, **mod_tiles)
    return out.reshape(n_batch, seq, d)
```

```python
import functools

import jax
import jax.numpy as jnp
from jax import lax
from jax.experimental import pallas as pl
from jax.experimental.pallas import tpu as pltpu

F32 = jnp.float32
BF16 = jnp.bfloat16

D_MODEL = 2048
DEPTH = 2
GRID_W = 64
NORM_EPS = 1e-6
RWKV_DIM = 1024
RWKV_HEAD = 64
RWKV_HEADS = 16
DECAY_LORA = 96
ICLR_LORA = 96
GATE_LORA = 256
RWKV_GN_EPS = 64e-5
GLA_HEADS = 4
GLA_KDIM = 512
GLA_VDIM = 1024
GLA_DK = 128
GLA_DV = 256
GLA_GATE_LORA = 16
GLA_TAU = 16.0
GLA_QKV = 2048
N_EXPERTS = 16
EXPERTS_PER_GROUP = 4
N_PAIRS = 24
D_EXPERT = 1408
RWKV_IN = 3520
GLA_SEQ_IN = 2064
GLA_GATE_OFF = RWKV_IN + GLA_SEQ_IN
BR_GATE_OFF = GLA_GATE_OFF + GLA_VDIM

LANES = 128
CHUNK = 64
MOD_ROWS = 16
LORA_PAD = 128
RW_XW = 3 * RWKV_DIM
RW_XA = RW_XW + LORA_PAD
RW_XG = RW_XA + LORA_PAD
RW_COLS = RW_XG + GATE_LORA
GLA_COLS = GLA_QKV + LANES
GATE_COLS = GLA_VDIM + 2 * D_MODEL
VMEM_LIMIT = 56 * 1024 * 1024


def _cparams(sem, vmem=VMEM_LIMIT):
    return pltpu.CompilerParams(dimension_semantics=sem, vmem_limit_bytes=vmem)


def _pick_tile(n, cands):
    for c in cands:
        if n % c == 0:
            return c
    raise ValueError(f"no tile for {n}")


def _mm(a, b):
    return jnp.dot(a.astype(BF16), b.astype(BF16), preferred_element_type=F32)


def _mm_nt(a, b):
    return lax.dot_general(a.astype(BF16), b.astype(BF16), (((1,), (1,)), ((), ())), preferred_element_type=F32)


def _mm_tn(a, b):
    return lax.dot_general(a.astype(BF16), b.astype(BF16), (((0,), (0,)), ((), ())), preferred_element_type=F32)


def _split2(a):
    hi = a.astype(BF16)
    lo = (a - hi.astype(F32)).astype(BF16)
    return hi, lo


def _split3(a):
    hi = a.astype(BF16)
    r1 = a - hi.astype(F32)
    mid = r1.astype(BF16)
    lo = (r1 - mid.astype(F32)).astype(BF16)
    return hi, mid, lo


def _mm_x3(a, b):
    ah, al = _split2(a)
    bh, bl = b if isinstance(b, tuple) else _split2(b)
    d = functools.partial(jnp.dot, preferred_element_type=F32)
    return d(ah, bh) + (d(al, bh) + d(ah, bl))


def _presplit(w):
    hi = w.astype(BF16)
    return jnp.stack([hi, (w - hi.astype(F32)).astype(BF16)], axis=-3)


def _mm_mask(m01, x):
    h, m, l = _split3(x)
    d = functools.partial(jnp.dot, preferred_element_type=F32)
    return d(m01, h) + (d(m01, m) + d(m01, l))


def _sigmoid(x):
    return 1.0 / (1.0 + jnp.exp(-x))


def _silu(x):
    return x * _sigmoid(x)


def _softplus(z):
    return jnp.maximum(z, 0.0) + jnp.log1p(jnp.exp(-jnp.abs(z)))


def _head_sum64(x):
    ri = lax.broadcasted_iota(jnp.int32, (256, 256), 0) // RWKV_HEAD
    ci = lax.broadcasted_iota(jnp.int32, (256, 256), 1) // RWKV_HEAD
    ones = (ri == ci).astype(BF16)
    hi, lo = _split2(x)
    d = functools.partial(jnp.dot, preferred_element_type=F32)
    outs = [d(hi[:, g * 256:(g + 1) * 256], ones) + d(lo[:, g * 256:(g + 1) * 256], ones)
            for g in range(x.shape[1] // 256)]
    return jnp.concatenate(outs, axis=1)


def _adaln_kernel(c_ref, w_ref, b_ref, o_ref):
    c = c_ref[...]
    o_ref[...] = _mm_x3(_silu(c), w_ref[...]) + b_ref[...]


def _adaln(c_all, w_ada, b_ada):
    nl, d, n = w_ada.shape
    tn = 1024
    return pl.pallas_call(
        _adaln_kernel,
        out_shape=jax.ShapeDtypeStruct((nl, MOD_ROWS, n), F32),
        grid=(nl, n // tn),
        in_specs=[pl.BlockSpec((MOD_ROWS, d), lambda l, j: (0, 0)),
                  pl.BlockSpec((None, d, tn), lambda l, j: (l, 0, j)),
                  pl.BlockSpec((None, 1, tn), lambda l, j: (l, 0, j))],
        out_specs=pl.BlockSpec((None, MOD_ROWS, tn), lambda l, j: (l, 0, j)),
        compiler_params=_cparams(("parallel", "parallel")),
        name="adaln",
    )(c_all, w_ada, b_ada.reshape(nl, 1, n))


def _rms(x, g):
    return x * lax.rsqrt(jnp.mean(x * x, axis=-1, keepdims=True) + NORM_EPS) * g


def _prenorm_kernel(x_ref, o0_ref, o1_ref, mprev_ref, mcur_ref, g_ref, *outs, final):
    x = x_ref[...] + mprev_ref[0, 5:6, :] * (o0_ref[...] + o1_ref[...])
    y = _rms(x, g_ref[...])
    if final:
        outs[0][...] = y
        return
    outs[0][...] = x
    outs[1][...] = (y * (1.0 + mcur_ref[0, 1:2, :]) + mcur_ref[0, 0:1, :]).astype(outs[1].dtype)


def _prenorm_first_kernel(xl_ref, xc_ref, mcur_ref, g_ref, xs_o, h_o, *, n_lat_tiles):
    x = jnp.where(pl.program_id(0) < n_lat_tiles, xl_ref[...], xc_ref[...])
    xs_o[...] = x
    h_o[...] = (_rms(x, g_ref[...]) * (1.0 + mcur_ref[0, 1:2, :]) + mcur_ref[0, 0:1, :]).astype(h_o.dtype)


def _mod_row_map(n_lat_tiles, tiles_per_batch, n_batch):
    return lambda i: (jnp.where(i < n_lat_tiles, i // tiles_per_batch, n_batch), 0, 0)


def _prenorm_first(x_lat, x_ctx, mods_cur, g, *, tm, n_lat_tiles, tiles_per_batch, n_batch):
    d = x_lat.shape[1]
    n_rows = x_lat.shape[0] + x_ctx.shape[0]
    row = pl.BlockSpec((tm, d), lambda i: (i, 0))
    return pl.pallas_call(
        functools.partial(_prenorm_first_kernel, n_lat_tiles=n_lat_tiles),
        out_shape=[jax.ShapeDtypeStruct((n_rows, d), F32), jax.ShapeDtypeStruct((n_rows, d), BF16)],
        grid=(n_rows // tm,),
        in_specs=[pl.BlockSpec((tm, d), lambda i: (jnp.minimum(i, n_lat_tiles - 1), 0)),
                  pl.BlockSpec((tm, d), lambda i: (jnp.maximum(i - n_lat_tiles, 0), 0)),
                  pl.BlockSpec((1, 6, d), _mod_row_map(n_lat_tiles, tiles_per_batch, n_batch)),
                  pl.BlockSpec((1, d), lambda i: (0, 0))],
        out_specs=[row, row],
        compiler_params=_cparams(("parallel",)),
        name="prenorm_first",
    )(x_lat, x_ctx, mods_cur, g.reshape(1, d))


def _prenorm(x, moe_out, mods_prev, mods_cur, g, *, tm, n_lat_tiles, tiles_per_batch, n_batch, n_rows, final=False):
    d = x.shape[1]
    row = pl.BlockSpec((tm, d), lambda i: (i, 0))
    mspec = pl.BlockSpec((1, 6, d), _mod_row_map(n_lat_tiles, tiles_per_batch, n_batch))
    slot1 = n_rows // tm
    out_shape = [jax.ShapeDtypeStruct((n_rows, d), F32)]
    if not final:
        out_shape.append(jax.ShapeDtypeStruct((n_rows, d), BF16))
    return pl.pallas_call(
        functools.partial(_prenorm_kernel, final=final),
        out_shape=out_shape,
        grid=(n_rows // tm,),
        in_specs=[row, row, pl.BlockSpec((tm, d), lambda i: (slot1 + i, 0)), mspec, mspec,
                  pl.BlockSpec((1, d), lambda i: (0, 0))],
        out_specs=[row] * len(out_shape),
        compiler_params=_cparams(("parallel",)),
        name="prenorm_final" if final else "prenorm",
    )(x, moe_out, moe_out, mods_prev, mods_cur, g.reshape(1, d))


def _matmul_kernel(a_ref, w_ref, o_ref):
    o_ref[...] = jnp.dot(a_ref[...], w_ref[...], preferred_element_type=F32).astype(o_ref.dtype)


def _matmul(a, w, out_dtype, *, tn, row0=0, n_rows=None):
    k = a.shape[1]
    n = w.shape[1]
    m = a.shape[0] if n_rows is None else n_rows
    tm = next(c for c in (1024, 512, 256, 128, 64) if m % c == 0 and row0 % c == 0)
    i0 = row0 // tm
    return pl.pallas_call(
        _matmul_kernel,
        out_shape=jax.ShapeDtypeStruct((m, n), out_dtype),
        grid=(m // tm, n // tn),
        in_specs=[pl.BlockSpec((tm, k), lambda i, j: (i + i0, 0)),
                  pl.BlockSpec((k, tn), lambda i, j: (0, j))],
        out_specs=pl.BlockSpec((tm, tn), lambda i, j: (i, j)),
        compiler_params=_cparams(("parallel", "parallel")),
        name="in_proj",
    )(a, w)


def _seq_flags(i, n_lat_tiles, tps_lat, tps_ctx):
    is_lat = i < n_lat_tiles
    j = jnp.where(is_lat, i % tps_lat, (i - n_lat_tiles) % tps_ctx)
    return j == 0, j == jnp.where(is_lat, tps_lat - 1, tps_ctx - 1)


def _neighbours(ref, prev_ref, next_ref, c0, c1, first, last):
    tm = ref.shape[0]
    p = ref[:, c0:c1]
    rows = lax.broadcasted_iota(jnp.int32, (tm, 1), 0)
    pr = jnp.where(first, 0.0, prev_ref[7:8, c0:c1])
    nx = jnp.where(last, 0.0, next_ref[0:1, c0:c1])
    prev = jnp.where(rows == 0, pr, pltpu.roll(p, 1, 0))
    nxt = jnp.where(rows == tm - 1, nx, pltpu.roll(p, tm - 1, 0))
    return p, prev, nxt


def _halo_specs(tm, cols, n_rows):
    r8 = tm // 8
    last8 = n_rows // 8 - 1
    return [pl.BlockSpec((tm, cols), lambda i: (i, 0)),
            pl.BlockSpec((8, cols), lambda i: (jnp.maximum(i * r8 - 1, 0), 0)),
            pl.BlockSpec((8, cols), lambda i: (jnp.minimum((i + 1) * r8, last8), 0))]


def _rwkv_prep_kernel(p_ref, pp_ref, pn_ref, mu_ref, w0_ref, w2_ref, a0_ref, a2_ref, g2_ref, kk_ref, ka_ref, rk_ref,
                      r_o, v_o, kap_o, lw_o, b_o, kd_o, bonus_o, gate_o, *, n_lat_tiles, tps_lat, tps_ctx):
    first, last = _seq_flags(pl.program_id(0), n_lat_tiles, tps_lat, tps_ctx)

    def shifted(c0, c1):
        p, prev, nxt = _neighbours(p_ref, pp_ref, pn_ref, c0, c1, first, last)
        mu0, mu1 = mu_ref[0:1, c0:c1], mu_ref[1:2, c0:c1]
        return (1.0 - mu0 - mu1) * p + mu0 * prev + mu1 * nxt

    pair = lambda ref, *idx: (ref[idx + (0,)], ref[idx + (1,)])

    r = shifted(0, RWKV_DIM)
    k = shifted(RWKV_DIM, 2 * RWKV_DIM)
    v = shifted(2 * RWKV_DIM, 3 * RWKV_DIM)
    kk = k * kk_ref[...]
    kap = kk / jnp.maximum(jnp.sqrt(_head_sum64(kk * kk)), 1e-12)
    r_o[...] = r.astype(r_o.dtype)
    v_o[...] = v.astype(v_o.dtype)
    kap_o[...] = kap.astype(kap_o.dtype)
    bonus_o[...] = _head_sum64(r * k * rk_ref[...]) * v
    txw = jnp.tanh(shifted(RW_XW, RW_XA))
    xa = shifted(RW_XA, RW_XG)
    for d in range(2):
        w_log = -_softplus(-(w0_ref[d:d + 1, :] + _mm_x3(txw, pair(w2_ref, d)))) - 0.5
        lw_o[d] = -jnp.exp(w_log)
        a = _sigmoid(a0_ref[d:d + 1, :] + _mm_x3(xa, pair(a2_ref, d)))
        kd_o[d] = (k * (1.0 + (a - 1.0) * ka_ref[...])).astype(kd_o.dtype)
        b_o[d] = (kap * a).astype(b_o.dtype)
    gate_o[...] = _mm_x3(_sigmoid(shifted(RW_XG, RW_COLS)), pair(g2_ref))


def _rwkv_prep(p_rw, mu, w0, w2, a0, a2, g2, k_k, k_a, r_k, *, tm, n_batch, n_lat_tiles, tps_lat, tps_ctx):
    n = p_rw.shape[0]
    gm = _group_major(_scan_groups(n_batch), n_batch, n_lat_tiles, tps_lat, tps_ctx)
    full = lambda a: pl.BlockSpec(a.shape, lambda i: (0,) * a.ndim)
    tok = pl.BlockSpec((tm, RWKV_DIM), lambda i: (i, 0))
    row = pl.BlockSpec((tm, RWKV_DIM), lambda i: (gm(i), 0))
    row2 = pl.BlockSpec((2, tm, RWKV_DIM), lambda i: (0, gm(i), 0))
    one = lambda dt: jax.ShapeDtypeStruct((n, RWKV_DIM), dt)
    two = lambda dt: jax.ShapeDtypeStruct((2, n, RWKV_DIM), dt)
    params = [mu, w0, w2, a0, a2, g2, k_k, k_a, r_k]
    return pl.pallas_call(
        functools.partial(_rwkv_prep_kernel, n_lat_tiles=n_lat_tiles, tps_lat=tps_lat, tps_ctx=tps_ctx),
        out_shape=[one(BF16), one(BF16), one(BF16), two(F32), two(BF16), two(BF16), one(F32), one(F32)],
        grid=(n // tm,),
        in_specs=_halo_specs(tm, RW_COLS, n) + [full(a) for a in params],
        out_specs=[row, row, row, row2, row2, row2, tok, tok],
        compiler_params=_cparams(("parallel",)),
        name="rwkv_prep",
    )(p_rw, p_rw, p_rw, *params)


def _order_masks(n, rev):
    ri = lax.broadcasted_iota(jnp.int32, (n, n), 0)
    ci = lax.broadcasted_iota(jnp.int32, (n, n), 1)
    if rev:
        return ci >= ri, ci > ri, ri == ci
    return ci <= ri, ci < ri, ri == ci


HEADS_PER_GROUP = 4
GROUP = HEADS_PER_GROUP * RWKV_HEAD


def _head_block_mask():
    ri = lax.broadcasted_iota(jnp.int32, (GROUP, GROUP), 0) // RWKV_HEAD
    ci = lax.broadcasted_iota(jnp.int32, (GROUP, GROUP), 1) // RWKV_HEAD
    return ri == ci


def _block_diag(y_g, bdm):
    return jnp.where(bdm, jnp.concatenate([y_g] * HEADS_PER_GROUP, axis=0), jnp.zeros((), y_g.dtype))


def _head_prod(x, y, bdm, nt=False):
    xb, yb = x.astype(BF16), y.astype(BF16)
    dims = (((1,), (1,)), ((), ())) if nt else (((1,), (0,)), ((), ()))
    outs = [lax.dot_general(xb[:, g:g + GROUP], _block_diag(yb[:, g:g + GROUP], bdm), dims,
                            preferred_element_type=F32) for g in range(0, x.shape[1], GROUP)]
    return jnp.concatenate(outs, axis=1)


def _head_prod_tn(a, b, bdm):
    ab, bb = a.astype(BF16), b.astype(BF16)
    outs = []
    for g in range(0, a.shape[1], GROUP):
        p = lax.dot_general(ab[:, g:g + GROUP], bb[:, g:g + GROUP], (((0,), (0,)), ((), ())),
                            preferred_element_type=F32)
        p = jnp.where(bdm, p, 0.0)
        outs.append((p[0:64] + p[64:128]) + (p[128:192] + p[192:256]))
    return jnp.concatenate(outs, axis=1)


def _rwkv_chunks(chains):
    row = lax.broadcasted_iota(jnp.int32, (CHUNK, RWKV_DIM), 0)
    col = lax.broadcasted_iota(jnp.int32, (CHUNK, RWKV_DIM), 1) % RWKV_HEAD
    eye_f = (col == row).astype(F32)
    bdm = _head_block_mask()
    prod = functools.partial(_head_prod, bdm=bdm)
    each = lambda f, *cols: [f(*a) for a in zip(*cols)]
    top, bottom = (lambda a: a[:CHUNK]), (lambda a: a[CHUNK:])
    stack = lambda a, b: jnp.concatenate([a, b], axis=0)

    incl, strict, kr, bt, kdt, vs, e_tot = [], [], [], [], [], [], []
    for r_ref, v_ref, kap_ref, lw_ref, b_ref, kd_ref, _, _, rev in chains:
        incl.append(col >= row if rev else col <= row)
        strict.append(col > row if rev else col < row)
        lw = lw_ref[...]
        gam = _mm_mask(_order_masks(CHUNK, rev)[0].astype(BF16), lw)
        e_tot.append(jnp.exp(jnp.sum(lw, axis=0, keepdims=True)))
        e_neg = jnp.exp(-gam)
        kr.append(stack(kap_ref[...] * jnp.exp(gam - lw), r_ref[...] * jnp.exp(gam)))
        bt.append(b_ref[...] * e_neg)
        kdt.append(kd_ref[...] * e_neg)
        vs.append(v_ref[...])
    g_b = each(lambda a, b: prod(a, b, nt=True), kr, bt)
    g_k = each(lambda a, b: prod(a, b, nt=True), kr, kdt)
    n_ab = each(lambda m, g: jnp.where(m, top(g), 0.0), strict, g_b)
    a_kv = each(lambda m, g: jnp.where(m, top(g), 0.0), strict, g_k)
    m_b = each(lambda m, g: jnp.where(m, bottom(g), 0.0), incl, g_b)
    m_kd = each(lambda m, g: jnp.where(m, bottom(g), 0.0), incl, g_k)
    t_inv = each(lambda n: eye_f - n, n_ab)
    pw = each(lambda n: prod(n, n), n_ab)
    for _ in range(4):
        both = each(lambda t, p: prod(stack(t, p), p), t_inv, pw)
        t_inv = each(lambda t, b: t + top(b), t_inv, both)
        pw = each(bottom, both)
    t_inv = each(lambda t, p: t + prod(t, p), t_inv, pw)
    av = each(lambda a, m, v: prod(stack(a, m), v), a_kv, m_kd, vs)
    s = [c[7][...] for c in chains]
    ks = each(lambda a, b: prod(a, b, nt=True), kr, s)
    u = each(lambda t, k, a: prod(t, top(k) + top(a)), t_inv, ks, av)
    mbu = each(prod, m_b, u)
    upd = each(lambda v, uu, kd, b: _head_prod_tn(stack(v, uu), stack(kd, -b), bdm), vs, u, kdt, bt)
    for c, k, a, m, s0, up, e in zip(chains, ks, av, mbu, s, upd, e_tot):
        c[6][...] = bottom(k) + bottom(a) - m
        c[7][...] = (s0 + up) * e


def _rwkv_scan_kernel(rf_ref, vf_ref, kapf_ref, lwf_ref, bf_ref, kdf_ref, rr_ref, vr_ref, kapr_ref, lwr_ref, br_ref,
                      kdr_ref, yf_ref, yr_ref, sf_ref, sr_ref):
    @pl.when(pl.program_id(1) == 0)
    def _():
        sf_ref[...] = jnp.zeros_like(sf_ref)
        sr_ref[...] = jnp.zeros_like(sr_ref)

    chains = []
    for p in range(rf_ref.shape[0]):
        chains.append((rf_ref.at[p], vf_ref.at[p], kapf_ref.at[p], lwf_ref.at[p], bf_ref.at[p], kdf_ref.at[p],
                       yf_ref.at[p], sf_ref.at[p], False))
        chains.append((rr_ref.at[p], vr_ref.at[p], kapr_ref.at[p], lwr_ref.at[p], br_ref.at[p], kdr_ref.at[p],
                       yr_ref.at[p], sr_ref.at[p], True))
    _rwkv_chunks(chains)


def _scan_block_map(rev, n_lat_blocks, nb_lat, nb_ctx):
    def f(b, s):
        c_ctx = (nb_ctx - 1 - s) if rev else s
        c_lat = (nb_lat - 1 - (s - nb_ctx)) if rev else (s - nb_ctx)
        return jnp.where(s < nb_ctx, n_lat_blocks + b * nb_ctx + c_ctx, b * nb_lat + c_lat)
    return f


def _scan_groups(n_batch):
    return 2 if n_batch % 2 == 0 else 1


def _group_major(n_groups, n_batch, n_lat_tiles, tps_lat, tps_ctx):
    per = n_batch // n_groups
    lat, ctx = per * tps_lat, per * tps_ctx

    def f(i):
        is_lat = i < n_lat_tiles
        b = jnp.where(is_lat, i // tps_lat, (i - n_lat_tiles) // tps_ctx)
        j = jnp.where(is_lat, i % tps_lat, (i - n_lat_tiles) % tps_ctx)
        return (b // per) * (lat + ctx) + jnp.where(is_lat, (b % per) * tps_lat + j, lat + (b % per) * tps_ctx + j)
    return f


def _rwkv_scan(r, v, kap, lw, bb, kd, *, n_batch, nc_lat, nc_ctx):
    n = r.shape[0]
    ng = _scan_groups(n_batch)
    per = n_batch // ng
    g3 = lambda a: a.reshape(ng, n // ng, RWKV_DIM)
    g4 = lambda a: a.reshape(2, ng, n // ng, RWKV_DIM)
    specs = []
    for d, rev in enumerate((False, True)):
        blk = _scan_block_map(rev, per * nc_lat, nc_lat, nc_ctx)
        row = pl.BlockSpec((ng, CHUNK, RWKV_DIM), lambda b, s, blk=blk: (0, blk(b, s), 0))
        rowd = pl.BlockSpec((None, ng, CHUNK, RWKV_DIM), lambda b, s, blk=blk, d=d: (d, 0, blk(b, s), 0))
        specs.append((row, rowd))
    (rowf, rowdf), (rowr, rowdr) = specs
    out = jax.ShapeDtypeStruct((ng, n // ng, RWKV_DIM), F32)
    state = pltpu.VMEM((ng, RWKV_HEAD, RWKV_DIM), F32)
    yf, yr = pl.pallas_call(
        _rwkv_scan_kernel,
        out_shape=[out, out],
        grid=(per, nc_ctx + nc_lat),
        in_specs=[rowf, rowf, rowf, rowdf, rowdf, rowdf, rowr, rowr, rowr, rowdr, rowdr, rowdr],
        out_specs=[rowf, rowr],
        scratch_shapes=[state, state],
        compiler_params=_cparams(("parallel", "arbitrary")),
        name="rwkv_scan",
    )(g3(r), g3(v), g3(kap), g4(lw), g4(bb), g4(kd), g3(r), g3(v), g3(kap), g4(lw), g4(bb), g4(kd))
    return yf.reshape(n, RWKV_DIM), yr.reshape(n, RWKV_DIM)


def _gla_blocks(dirs, n_rows, cw_ref):
    rows = lax.broadcasted_iota(jnp.int32, (n_rows, 1), 0)
    for d in dirs:
        for c0 in range(0, GLA_QKV, 512):
            c1 = c0 + 512
            p = d["load"](c0, c1)
            prev = jnp.where(rows == 0, d["prev_row"](c0, c1), pltpu.roll(p, 1, 0))
            nxt = jnp.where(rows == n_rows - 1, d["next_row"](c0, c1), pltpu.roll(p, n_rows - 1, 0))
            act = _silu(cw_ref[0:1, c0:c1] * prev + cw_ref[1:2, c0:c1] * p + cw_ref[2:3, c0:c1] * nxt)
            if c0 == 0:
                act = act * (GLA_DK ** -0.5)
            d["qkv"][0:n_rows, c0:c1] = act
        z = _mm_x3(d["load"](GLA_QKV, GLA_COLS), d["aw"][...]) + d["ab"][...]
        d["g"][0:n_rows, :] = -_softplus(-z) * (1.0 / GLA_TAU)

    each = lambda f, *cols: [f(*a) for a in zip(*cols)]
    incl = [_order_masks(CHUNK, d["rev"])[0] for d in dirs]
    incl_b = [m.astype(BF16) for m in incl]
    n_chunks = n_rows // CHUNK
    for step in range(n_chunks):
        rs = [slice(c * CHUNK, (c + 1) * CHUNK) for c in ((n_chunks - 1 - step) if d["rev"] else step for d in dirs)]
        g = each(lambda d, r: d["g"][r, :], dirs, rs)
        gc = each(_mm_mask, incl_b, g)
        gl = each(lambda a: jnp.sum(a, axis=0, keepdims=True), g)
        q = each(lambda d, r: d["qkv"][r, 0:GLA_KDIM], dirs, rs)
        k = each(lambda d, r: d["qkv"][r, GLA_KDIM:2 * GLA_KDIM], dirs, rs)
        qg = each(lambda a, c: a * jnp.exp(c), q, gc)
        kg = each(lambda a, c: a * jnp.exp(-c), k, gc)
        ke = each(lambda a, l, c: a * jnp.exp(l - c), k, gl, gc)
        e_l = each(jnp.exp, gl)
        for h in range(GLA_HEADS):
            ks = slice(h * GLA_DK, (h + 1) * GLA_DK)
            vs = slice(2 * GLA_KDIM + h * GLA_DV, 2 * GLA_KDIM + (h + 1) * GLA_DV)
            v_h = each(lambda d, r: d["qkv"][r, vs], dirs, rs)
            att = each(lambda m, a, b: jnp.where(m, _mm_nt(a[:, ks], b[:, ks]), 0.0), incl, qg, kg)
            st = [d["st"][h] for d in dirs]
            out = each(lambda a, v, qq, s0: _mm(a, v) + _mm_nt(qq[:, ks], s0), att, v_h, qg, st)
            new = each(lambda s0, e, v, kk: s0 * e[:, ks] + _mm_tn(v, kk[:, ks]), st, e_l, v_h, ke)
            for d, r, o_h, s1 in zip(dirs, rs, out, new):
                d["o"][r, h * GLA_DV:(h + 1) * GLA_DV] = o_h
                d["st"][h] = s1


def _gla_kernel(uc_ref, ulf_ref, ulfp_ref, ulfn_ref, ulr_ref, ulrp_ref, ulrn_ref, cw_ref, awf_ref, abf_ref, awr_ref,
                abr_ref, ocf_ref, ocr_ref, olf_ref, olr_ref, stf_ref, str_ref, qkvf_s, qkvr_s, gf_s, gr_s, of_s,
                or_s, *, n_wg):
    s = pl.program_id(1)
    grid_rows, wg_cols = ulf_ref.shape[0], ulf_ref.shape[1]
    zero_row = lambda c0, c1: jnp.zeros((1, c1 - c0), F32)
    fwd = dict(aw=awf_ref, ab=abf_ref, st=stf_ref, qkv=qkvf_s, g=gf_s, o=of_s, rev=False)
    bwd = dict(aw=awr_ref, ab=abr_ref, st=str_ref, qkv=qkvr_s, g=gr_s, o=or_s, rev=True)

    @pl.when(s == 0)
    def _():
        stf_ref[...] = jnp.zeros_like(stf_ref)
        str_ref[...] = jnp.zeros_like(str_ref)
        n = uc_ref.shape[0]
        ctx = dict(load=lambda c0, c1: uc_ref[:, c0:c1], prev_row=zero_row, next_row=zero_row)
        _gla_blocks([{**fwd, **ctx}, {**bwd, **ctx}], n, cw_ref)
        ocf_ref[...] = of_s[0:n, :]
        ocr_ref[...] = or_s[0:n, :]

    @pl.when(s > 0)
    def _():
        n = grid_rows * wg_cols

        def lat(u_ref, up_ref, un_ref, wg):
            hr = up_ref.shape[0]
            return dict(
                load=lambda c0, c1: jnp.concatenate([u_ref[:, w, c0:c1] for w in range(wg_cols)], axis=0),
                prev_row=lambda c0, c1: jnp.where(wg == 0, 0.0, up_ref[hr - 1:hr, 7, c0:c1]),
                next_row=lambda c0, c1: jnp.where(wg == n_wg - 1, 0.0, un_ref[0:1, 0, c0:c1]))

        _gla_blocks([{**fwd, **lat(ulf_ref, ulfp_ref, ulfn_ref, s - 1)},
                     {**bwd, **lat(ulr_ref, ulrp_ref, ulrn_ref, n_wg - s)}], n, cw_ref)
        for w in range(wg_cols):
            olf_ref[:, w, :] = of_s[w * grid_rows:(w + 1) * grid_rows, :]
            olr_ref[:, w, :] = or_s[w * grid_rows:(w + 1) * grid_rows, :]


def _gla_scan(u_ctx, u_lat, conv_w, alpha_w, alpha_b, *, n_batch):
    t_ctx, ch = u_ctx.shape
    lc = t_ctx // n_batch
    _, grid_rows, grid_w, _ = u_lat.shape
    wg_cols = min(grid_w, max(8, 256 // grid_rows))
    n_wg = grid_w // wg_cols
    assert (wg_cols * grid_rows) % CHUNK == 0 and lc % CHUNK == 0 and grid_w % wg_cols == 0 and wg_cols % 8 == 0
    hr = 8 if grid_rows % 8 == 0 else grid_rows
    n_rows = max(lc, wg_cols * grid_rows)

    def group(rev):
        return lambda s: jnp.clip((n_wg - s) if rev else (s - 1), 0, n_wg - 1)

    full = lambda a: pl.BlockSpec(a.shape, lambda b, s: (0,) * a.ndim)
    lat = lambda c, wg: pl.BlockSpec((None, grid_rows, wg_cols, c), lambda b, s: (b, 0, wg(s), 0))

    def lat_in(wg):
        return [lat(ch, wg),
                pl.BlockSpec((None, hr, 8, ch), lambda b, s: (
                    b, grid_rows // hr - 1, jnp.maximum(wg(s) * (wg_cols // 8) - 1, 0), 0)),
                pl.BlockSpec((None, hr, 8, ch), lambda b, s: (
                    b, 0, jnp.minimum((wg(s) + 1) * (wg_cols // 8), grid_w // 8 - 1), 0))]

    ctx_out = jax.ShapeDtypeStruct((t_ctx, GLA_VDIM), F32)
    lat_out = jax.ShapeDtypeStruct((n_batch, grid_rows, grid_w, GLA_VDIM), F32)
    ctx_spec = pl.BlockSpec((lc, GLA_VDIM), lambda b, s: (b, 0))
    per_dir = lambda shape: [pltpu.VMEM(shape, F32)] * 2
    params = [conv_w, alpha_w[0], alpha_b[0], alpha_w[1], alpha_b[1]]
    return pl.pallas_call(
        functools.partial(_gla_kernel, n_wg=n_wg),
        out_shape=[ctx_out, ctx_out, lat_out, lat_out],
        grid=(n_batch, 1 + n_wg),
        in_specs=[pl.BlockSpec((lc, ch), lambda b, s: (b, 0))] + lat_in(group(False)) + lat_in(group(True))
        + [full(a) for a in params],
        out_specs=[ctx_spec, ctx_spec, lat(GLA_VDIM, group(False)), lat(GLA_VDIM, group(True))],
        scratch_shapes=per_dir((GLA_HEADS, GLA_DV, GLA_DK)) + per_dir((n_rows, GLA_QKV))
        + per_dir((n_rows, GLA_KDIM)) + per_dir((n_rows, GLA_VDIM)),
        compiler_params=_cparams(("parallel", "arbitrary")),
        name="gla_scan",
    )(u_ctx, u_lat, u_lat, u_lat, u_lat, u_lat, u_lat, *params)


def _pair_tables():
    ri = lax.broadcasted_iota(jnp.int32, (LANES, LANES), 0)
    ci = lax.broadcasted_iota(jnp.int32, (LANES, LANES), 1)
    grp = ci // 6
    pr = ci % 6
    lo = jnp.where(pr < 3, 0, jnp.where(pr < 5, 1, 2))
    hi = jnp.where(pr < 3, pr + 1, jnp.where(pr < 5, pr - 1, 3))
    member = ((ri == grp * EXPERTS_PER_GROUP + lo) | (ri == grp * EXPERTS_PER_GROUP + hi)) & (ci < N_PAIRS)
    return member


def _route(h, rw_ref, rb_ref):
    lane = lax.broadcasted_iota(jnp.int32, (h.shape[0], LANES), 1)
    scores = _sigmoid(_mm_x3(h, (rw_ref[0], rw_ref[1])))
    sel = jnp.where(lane < N_EXPERTS, scores + rb_ref[...], 0.0)
    member = _pair_tables()
    s3 = _split3(sel)
    mb = member.astype(BF16)
    d = functools.partial(jnp.dot, preferred_element_type=F32)
    ps = d(s3[0], mb) + (d(s3[1], mb) + d(s3[2], mb))
    ps = jnp.where(lane < N_PAIRS, ps, -jnp.inf)
    best = jnp.max(ps, axis=-1, keepdims=True)
    bucket = jnp.min(jnp.where(ps == best, lane, LANES), axis=-1, keepdims=True)
    onehot = (lane == bucket).astype(BF16)
    chosen = lax.dot_general(onehot, mb, (((1,), (1,)), ((), ())), preferred_element_type=F32) > 0.5
    picked = jnp.where(chosen, scores, 0.0)
    gates = picked / jnp.sum(picked, axis=-1, keepdims=True)
    e_lo = jnp.min(jnp.where(chosen, lane, LANES), axis=-1, keepdims=True)
    e_hi = jnp.max(jnp.where(chosen, lane, -1), axis=-1, keepdims=True)
    w_lo = jnp.sum(jnp.where(lane == e_lo, gates, 0.0), axis=-1, keepdims=True)
    w_hi = jnp.sum(jnp.where(lane == e_hi, gates, 0.0), axis=-1, keepdims=True)
    out = jnp.where(lane == 0, e_lo.astype(F32), 0.0)
    out = jnp.where(lane == 1, e_hi.astype(F32), out)
    out = jnp.where(lane == 2, w_lo, out)
    return jnp.where(lane == 3, w_hi, out)


def _merge_kernel(x_ref, y0_ref, y1_ref, bonus_ref, gate_ref, ol0_ref, ol1_ref, oc0_ref, oc1_ref, pg_ref, mods_ref,
                  lng_ref, lnb_ref, gng_ref, wa_ref, wb_ref, wo_ref, nfg_ref, rw_ref, rb_ref,
                  xs_o, h_o, route_o, *, n_lat_tiles):
    y = y0_ref[...] + y1_ref[...]
    mean = _head_sum64(y) * (1.0 / RWKV_HEAD)
    yc = y - mean
    var = _head_sum64(yc * yc) * (1.0 / RWKV_HEAD)
    ya = (yc * lax.rsqrt(var + RWKV_GN_EPS) * lng_ref[...] + lnb_ref[...] + bonus_ref[...]) * gate_ref[...]
    o = jnp.where(pl.program_id(0) < n_lat_tiles, ol0_ref[...] + ol1_ref[...], oc0_ref[...] + oc1_ref[...])
    parts = []
    for h in range(GLA_HEADS):
        o_h = o[:, h * GLA_DV:(h + 1) * GLA_DV]
        parts.append(o_h * lax.rsqrt(jnp.mean(o_h * o_h, axis=-1, keepdims=True) + NORM_EPS) * gng_ref[...])
    yb = jnp.concatenate(parts, axis=1) * _silu(pg_ref[:, 0:GLA_VDIM].astype(F32))
    ga = _sigmoid(pg_ref[:, GLA_VDIM:GLA_VDIM + D_MODEL].astype(F32))
    gb = _sigmoid(pg_ref[:, GLA_VDIM + D_MODEL:GATE_COLS].astype(F32))
    z = ga * _mm(ya, wa_ref[...]) + gb * _mm(yb, wb_ref[...])
    xs = x_ref[...] + mods_ref[0, 2:3, :] * _mm(z, wo_ref[...])
    xs_o[...] = xs
    h = _rms(xs, nfg_ref[...]) * (1.0 + mods_ref[0, 4:5, :]) + mods_ref[0, 3:4, :]
    h_o[...] = h
    route_o[...] = _route(h, rw_ref, rb_ref)


def _merge(x, y0, y1, bonus, gate, o_lat, o_ctx, pg, mods, ln_g, ln_b, gn_g, wa, wb, wo, nf_g, rw, rb,
           *, layer, tm, n_rows, n_lat_tiles, tiles_per_batch, n_batch, tps_ctx):
    d = D_MODEL
    mrow = _mod_row_map(n_lat_tiles, tiles_per_batch, n_batch)
    gm = _group_major(_scan_groups(n_batch), n_batch, n_lat_tiles, tiles_per_batch, tps_ctx)
    row = lambda c: pl.BlockSpec((tm, c), lambda i: (i, 0))
    yrow = pl.BlockSpec((tm, RWKV_DIM), lambda i: (gm(i), 0))
    lat = pl.BlockSpec((tm, GLA_VDIM), lambda i: (jnp.minimum(i, n_lat_tiles - 1), 0))
    ctx = pl.BlockSpec((tm, GLA_VDIM), lambda i: (jnp.maximum(i - n_lat_tiles, 0), 0))
    full = lambda a: pl.BlockSpec(a.shape, lambda i: (0,) * a.ndim)
    of_layer = lambda a: pl.BlockSpec((None,) + a.shape[1:], lambda i: (layer, 0, 0))
    consts = [ln_g.reshape(1, -1), ln_b.reshape(1, -1), gn_g.reshape(1, -1), wa, wb, wo, nf_g.reshape(1, -1), rw, rb]
    const_specs = [full(a) for a in consts[:3]] + [of_layer(a) for a in consts[3:6]] + [full(a) for a in consts[6:]]
    return pl.pallas_call(
        functools.partial(_merge_kernel, n_lat_tiles=n_lat_tiles),
        out_shape=[jax.ShapeDtypeStruct((n_rows, d), F32), jax.ShapeDtypeStruct((n_rows, d), F32),
                   jax.ShapeDtypeStruct((n_rows, LANES), F32)],
        grid=(n_rows // tm,),
        in_specs=[row(d), yrow, yrow, row(RWKV_DIM), row(RWKV_DIM), lat, lat, ctx, ctx,
                  row(GATE_COLS), pl.BlockSpec((1, 6, d), mrow)] + const_specs,
        out_specs=[row(d), row(d), row(LANES)],
        compiler_params=_cparams(("parallel",)),
        name="merge",
    )(x, y0, y1, bonus, gate, o_lat[0], o_lat[1], o_ctx[0], o_ctx[1], pg, mods, *consts)


DMA_ISSUE_UNROLL = 8
MOE_UP_CHUNK = 256


def _moe_kernel(te_ref, nu_ref, src_ref, src_next_ref, dst_prev_ref, dst_ref, w_ref, h_hbm, wg_ref, wu_ref, wd_ref,
                o_hbm, xbuf, obuf, act_s, sem_in, sem_out, sem_fill, *, n_real_rows):
    del te_ref
    tm = xbuf.shape[1]
    i = pl.program_id(0)
    n_used = nu_ref[0]
    slot = i % 2

    other = 1 - slot

    def gather_row(idx_ref, s, r):
        return pltpu.make_async_copy(h_hbm.at[pl.ds(idx_ref[0, 0, r], 1)], xbuf.at[s, pl.ds(r, 1)], sem_in.at[s])

    def scatter_row(idx_ref, s, r):
        return pltpu.make_async_copy(obuf.at[s, pl.ds(r, 1)], o_hbm.at[pl.ds(idx_ref[0, 0, r], 1)], sem_out.at[s])

    def wait_gather(s):
        pltpu.make_async_copy(h_hbm.at[pl.ds(0, tm)], xbuf.at[s], sem_in.at[s]).wait()

    def wait_scatter(s):
        pltpu.make_async_copy(obuf.at[s], o_hbm.at[pl.ds(0, tm)], sem_out.at[s]).wait()

    def issue_all(make_row):
        def body(r, carry):
            make_row(r).start()
            return carry
        lax.fori_loop(0, tm, body, 0, unroll=DMA_ISSUE_UNROLL)

    @pl.when(i < n_used)
    def _():
        @pl.when(i == 0)
        def _():
            obuf[1] = jnp.zeros(obuf.shape[1:], obuf.dtype)
            fills = [pltpu.make_async_copy(obuf.at[1], o_hbm.at[pl.ds(n_real_rows + k * tm, tm)], sem_fill)
                     for k in range((o_hbm.shape[0] - n_real_rows) // tm)]
            for f in fills:
                f.start()
            for f in fills:
                f.wait()
            issue_all(lambda r: gather_row(src_ref, 0, r))

        @pl.when(i >= 1)
        def _():
            wait_scatter(slot)

        wait_gather(slot)
        bounds = list(range(0, D_EXPERT, MOE_UP_CHUNK)) + [D_EXPERT]
        n_chunks = len(bounds) - 1
        def up(c):
            xb = xbuf[slot].astype(BF16)
            cols = slice(bounds[c], bounds[c + 1])
            return (jnp.dot(xb, wg_ref[:, cols], preferred_element_type=F32),
                    jnp.dot(xb, wu_ref[:, cols], preferred_element_type=F32))

        g, u = up(0)
        for c in range(n_chunks):
            g_next, u_next = up(c + 1) if c + 1 < n_chunks else (None, None)
            act_s[:, bounds[c]:bounds[c + 1]] = (_silu(g) * u).astype(BF16)
            g, u = g_next, u_next
            for r in range(c * tm // n_chunks, (c + 1) * tm // n_chunks):
                gather_row(src_next_ref, other, r).start()
                scatter_row(dst_prev_ref, other, r).start()

        obuf[slot] = jnp.dot(act_s[...], wd_ref[...], preferred_element_type=F32) * w_ref[...][:, 0:1]

        @pl.when(i == n_used - 1)
        def _():
            issue_all(lambda r: scatter_row(dst_ref, slot, r))
            wait_scatter(other)
            wait_scatter(slot)
            wait_gather(other)


def _moe(h, route, wg, wu, wd, *, layer, tm):
    t, d = h.shape
    i32 = jnp.int32
    ids = jnp.concatenate([route[:, 0], route[:, 1]]).astype(i32)
    wts = jnp.concatenate([route[:, 2], route[:, 3]])
    n_tiles = (2 * t) // tm + N_EXPERTS
    n_spill = (N_EXPERTS + 1) * tm
    _, order, wts_s = lax.sort((ids, jnp.arange(2 * t, dtype=i32), wts), num_keys=1, is_stable=True)
    counts = jnp.sum((ids[:, None] == jnp.arange(N_EXPERTS, dtype=i32)[None, :]).astype(i32), axis=0)
    starts = jnp.cumsum(counts) - counts
    pcounts = ((counts + tm - 1) // tm) * tm
    pend = jnp.cumsum(pcounts)
    poff = pend - pcounts
    n_used = (pend[-1] // tm).astype(i32)
    tile = jnp.arange(n_tiles, dtype=i32)
    te = jnp.sum((tile[:, None] * tm >= pend[None, :]).astype(i32), axis=1)
    te = jnp.where(tile < n_used, te, jnp.take(te, jnp.maximum(n_used - 1, 0))).astype(i32)
    rank0 = tile * tm - jnp.take(poff, te)
    n_valid = jnp.where(tile < n_used, jnp.clip(jnp.take(counts, te) - rank0, 0, tm), 0)
    n_pos = n_tiles * tm
    shift = poff - starts
    e_tile = te[:, None]

    def spread(arr):
        ext = jnp.concatenate([jnp.zeros((n_pos,), arr.dtype), arr, jnp.zeros((n_pos,), arr.dtype)])
        out = jnp.zeros((n_tiles, tm), arr.dtype)
        for e in range(N_EXPERTS):
            seg = lax.dynamic_slice(ext, (n_pos - shift[e],), (n_pos,)).reshape(n_tiles, tm)
            out = jnp.where(e_tile == e, seg, out)
        return out

    a = spread(order)
    w_tile = spread(wts_s)
    valid = jnp.arange(tm, dtype=i32)[None, :] < n_valid[:, None]
    src = jnp.where(valid, a % t, 0).astype(i32)
    pad = (~valid) & (tile < n_used)[:, None]
    spill = 2 * t + tm + jnp.minimum(jnp.cumsum(pad.reshape(-1).astype(i32)) - 1, n_spill - tm - 1).reshape(n_tiles, tm)
    dst = jnp.where(valid, a, spill).astype(i32)
    dst = jnp.concatenate([2 * t + jnp.arange(tm, dtype=i32)[None, :], dst], axis=0)
    wrow = jnp.broadcast_to(jnp.where(valid, w_tile, 0.0).reshape(-1, 1), (n_tiles * tm, LANES))

    smem = functools.partial(pl.BlockSpec, (1, 1, tm), memory_space=pltpu.SMEM)
    src3, dst3 = src.reshape(n_tiles, 1, tm), dst.reshape(n_tiles + 1, 1, tm)
    return pl.pallas_call(
        functools.partial(_moe_kernel, n_real_rows=2 * t),
        out_shape=jax.ShapeDtypeStruct((2 * t + n_spill, d), F32),
        grid_spec=pltpu.PrefetchScalarGridSpec(
            num_scalar_prefetch=2,
            grid=(n_tiles,),
            in_specs=[smem(lambda i, *_: (i, 0, 0)),
                      smem(lambda i, *_: (jnp.minimum(i + 1, n_tiles - 1), 0, 0)),
                      smem(lambda i, *_: (i, 0, 0)),
                      smem(lambda i, *_: (i + 1, 0, 0)),
                      pl.BlockSpec((tm, LANES), lambda i, *_: (i, 0)),
                      pl.BlockSpec(memory_space=pl.ANY),
                      pl.BlockSpec((None, None, d, D_EXPERT), lambda i, te_r, *_: (layer, te_r[i], 0, 0)),
                      pl.BlockSpec((None, None, d, D_EXPERT), lambda i, te_r, *_: (layer, te_r[i], 0, 0)),
                      pl.BlockSpec((None, None, D_EXPERT, d), lambda i, te_r, *_: (layer, te_r[i], 0, 0))],
            out_specs=pl.BlockSpec(memory_space=pl.ANY),
            scratch_shapes=[pltpu.VMEM((2, tm, d), F32), pltpu.VMEM((2, tm, d), F32),
                            pltpu.VMEM((tm, D_EXPERT), BF16),
                            pltpu.SemaphoreType.DMA((2,)), pltpu.SemaphoreType.DMA((2,)), pltpu.SemaphoreType.DMA]),
        compiler_params=_cparams(("arbitrary",)),
        name="moe_ffn",
    )(te, n_used.reshape(1), src3, src3, dst3, dst3, wrow, h, wg, wu, wd)


def _pad_cols(a, n):
    return jnp.pad(a, [(0, 0)] * (a.ndim - 1) + [(0, n - a.shape[-1])])


def _pad_rows(a, n):
    return jnp.pad(a, [(0, 0)] * (a.ndim - 2) + [(0, n - a.shape[-2]), (0, 0)])


def _split_in_proj(w_in, mu):
    s = (RWKV_DIM, 2 * RWKV_DIM, 3 * RWKV_DIM, 3 * RWKV_DIM + DECAY_LORA, 3 * RWKV_DIM + DECAY_LORA + ICLR_LORA)

    def rw_layout(a):
        return jnp.concatenate([a[..., :s[2]], _pad_cols(a[..., s[2]:s[3]], LORA_PAD),
                                _pad_cols(a[..., s[3]:s[4]], LORA_PAD), a[..., s[4]:RWKV_IN]], axis=-1)

    w_in = w_in.astype(BF16)
    w_rw = rw_layout(w_in)
    w_gla = _pad_cols(w_in[:, RWKV_IN:GLA_GATE_OFF], GLA_COLS)
    w_gate = w_in[:, GLA_GATE_OFF:]
    return w_rw, w_gla, w_gate, rw_layout(mu)


def kernel(x, c, ctx, c_ctx, w_ada, b_ada, norm_mix_g, norm_ffn_g, w_in, rwkv_mu, rwkv_w0, rwkv_w2, rwkv_a0,
           rwkv_a2, rwkv_g2, rwkv_k_k, rwkv_k_a, rwkv_r_k, rwkv_ln_g, rwkv_ln_b, gla_conv, gla_alpha_w2,
           gla_alpha_b, gla_norm_g, w_branch_a, w_branch_b, w_out, router_w, router_b, exp_w_gate, exp_w_up,
           exp_w_down, final_norm_g):
    n_batch, seq, d = x.shape
    ctx_len = ctx.shape[1]
    assert d == D_MODEL and seq % GRID_W == 0 and n_batch < MOD_ROWS
    grid_rows = seq // GRID_W
    tm = min(256, ctx_len)
    assert seq % tm == 0 and ctx_len % tm == 0 and tm % CHUNK == 0
    t_lat, t_ctx = n_batch * seq, n_batch * ctx_len
    t_all = t_lat + t_ctx
    tiles = dict(tm=tm, n_lat_tiles=t_lat // tm)
    seq_tiles = dict(tps_lat=seq // tm, tps_ctx=ctx_len // tm)
    mod_tiles = dict(tiles_per_batch=seq // tm, n_batch=n_batch)

    c_all = jnp.zeros((MOD_ROWS, d), F32).at[:n_batch].set(c).at[n_batch].set(c_ctx)
    mods = _adaln(c_all, w_ada, b_ada).reshape(DEPTH, MOD_ROWS, 6, d)
    rw_pad = _presplit(_pad_cols(router_w, LANES))
    wa_all, wb_all, wo_all = (w.astype(BF16) for w in (w_branch_a, w_branch_b, w_out))
    wg_all, wu_all, wd_all = (w.astype(BF16) for w in (exp_w_gate, exp_w_up, exp_w_down))
    rb_pad = _pad_cols(router_b.reshape(1, -1), LANES)

    moe_out = None
    for l in range(DEPTH):
        last = l == DEPTH - 1
        if moe_out is None:
            xs, h = _prenorm_first(x.reshape(t_lat, d), ctx.reshape(t_ctx, d), mods[l], norm_mix_g[l],
                                   **tiles, **mod_tiles)
        else:
            xs, h = _prenorm(xs, moe_out, mods[l - 1], mods[l], norm_mix_g[l], n_rows=t_all, **tiles, **mod_tiles)
        w_rw, w_gla, w_gate, mu = _split_in_proj(w_in[l], rwkv_mu[l])
        p_rw = _matmul(h, w_rw, F32, tn=RW_COLS // 2)
        p_gla_lat = _matmul(h, w_gla, F32, tn=GLA_COLS, n_rows=t_lat)
        p_gla_ctx = _matmul(h, w_gla, F32, tn=GLA_COLS, row0=t_lat, n_rows=t_ctx)
        p_gate = _matmul(h, w_gate, BF16, tn=GATE_COLS // 5)

        r, v, kap, lw, bb, kd, bonus, gate = _rwkv_prep(
            p_rw, mu, rwkv_w0[l], _presplit(_pad_rows(rwkv_w2[l], LORA_PAD)), rwkv_a0[l],
            _presplit(_pad_rows(rwkv_a2[l], LORA_PAD)), _presplit(rwkv_g2[l]), rwkv_k_k[l].reshape(1, -1), rwkv_k_a[l].reshape(1, -1), rwkv_r_k[l].reshape(1, -1),
            n_batch=n_batch, **tiles, **seq_tiles)
        y0, y1 = _rwkv_scan(r, v, kap, lw, bb, kd, n_batch=n_batch, nc_lat=seq // CHUNK, nc_ctx=ctx_len // CHUNK)

        u_lat = p_gla_lat.reshape(n_batch, grid_rows, GRID_W, GLA_COLS)
        ocf, ocr, olf, olr = _gla_scan(p_gla_ctx, u_lat, gla_conv[l], _pad_rows(gla_alpha_w2[l], LANES),
                                       gla_alpha_b[l].reshape(2, 1, -1), n_batch=n_batch)
        o_ctx = [ocf, ocr]
        o_lat = [olf.reshape(t_lat, GLA_VDIM), olr.reshape(t_lat, GLA_VDIM)]

        n_rows = t_lat if last else t_all
        xs, hf, route = _merge(
            xs, y0, y1, bonus, gate, o_lat, o_ctx, p_gate, mods[l], rwkv_ln_g[l], rwkv_ln_b[l], gla_norm_g[l],
            wa_all, wb_all, wo_all, norm_ffn_g[l],
            rw_pad, rb_pad, layer=l, n_rows=n_rows, tps_ctx=ctx_len // tm, **tiles, **mod_tiles)
        moe_out = _moe(hf, route, wg_all, wu_all, wd_all, layer=l, tm=tm)

    (out,) = _prenorm(xs, moe_out, mods[DEPTH - 1], mods[DEPTH - 1], final_norm_g, n_rows=t_lat, final=True,
                      **tiles, **mod_tiles)
    return out.reshape(n_batch, seq, d)
```

```python
import functools

import jax
import jax.numpy as jnp
from jax import lax
from jax.experimental import pallas as pl
from jax.experimental.pallas import tpu as pltpu

F32 = jnp.float32
BF16 = jnp.bfloat16

D_MODEL = 2048
DEPTH = 2
GRID_W = 64
NORM_EPS = 1e-6
RWKV_DIM = 1024
RWKV_HEAD = 64
RWKV_HEADS = 16
DECAY_LORA = 96
ICLR_LORA = 96
GATE_LORA = 256
RWKV_GN_EPS = 64e-5
GLA_HEADS = 4
GLA_KDIM = 512
GLA_VDIM = 1024
GLA_DK = 128
GLA_DV = 256
GLA_GATE_LORA = 16
GLA_TAU = 16.0
GLA_QKV = 2048
N_EXPERTS = 16
EXPERTS_PER_GROUP = 4
N_PAIRS = 24
D_EXPERT = 1408
RWKV_IN = 3520
GLA_SEQ_IN = 2064
GLA_GATE_OFF = RWKV_IN + GLA_SEQ_IN
BR_GATE_OFF = GLA_GATE_OFF + GLA_VDIM

LANES = 128
CHUNK = 64
MOD_ROWS = 16
LORA_PAD = 128
RW_XW = 3 * RWKV_DIM
RW_XA = RW_XW + LORA_PAD
RW_XG = RW_XA + LORA_PAD
RW_COLS = RW_XG + GATE_LORA
GLA_COLS = GLA_QKV + LANES
GATE_COLS = GLA_VDIM + 2 * D_MODEL
VMEM_LIMIT = 56 * 1024 * 1024


def _cparams(sem, vmem=VMEM_LIMIT):
    return pltpu.CompilerParams(dimension_semantics=sem, vmem_limit_bytes=vmem)


def _pick_tile(n, cands):
    for c in cands:
        if n % c == 0:
            return c
    raise ValueError(f"no tile for {n}")


def _mm(a, b):
    return jnp.dot(a.astype(BF16), b.astype(BF16), preferred_element_type=F32)


def _mm_nt(a, b):
    return lax.dot_general(a.astype(BF16), b.astype(BF16), (((1,), (1,)), ((), ())), preferred_element_type=F32)


def _mm_tn(a, b):
    return lax.dot_general(a.astype(BF16), b.astype(BF16), (((0,), (0,)), ((), ())), preferred_element_type=F32)


def _split2(a):
    hi = a.astype(BF16)
    lo = (a - hi.astype(F32)).astype(BF16)
    return hi, lo


def _split3(a):
    hi = a.astype(BF16)
    r1 = a - hi.astype(F32)
    mid = r1.astype(BF16)
    lo = (r1 - mid.astype(F32)).astype(BF16)
    return hi, mid, lo


def _mm_x3(a, b):
    ah, al = _split2(a)
    bh, bl = b if isinstance(b, tuple) else _split2(b)
    d = functools.partial(jnp.dot, preferred_element_type=F32)
    return d(ah, bh) + (d(al, bh) + d(ah, bl))


def _presplit(w):
    hi = w.astype(BF16)
    return jnp.stack([hi, (w - hi.astype(F32)).astype(BF16)], axis=-3)


def _mm_mask(m01, x):
    h, m, l = _split3(x)
    d = functools.partial(jnp.dot, preferred_element_type=F32)
    return d(m01, h) + (d(m01, m) + d(m01, l))


def _sigmoid(x):
    return 1.0 / (1.0 + jnp.exp(-x))


def _silu(x):
    return x * _sigmoid(x)


def _softplus(z):
    return jnp.maximum(z, 0.0) + jnp.log1p(jnp.exp(-jnp.abs(z)))


def _head_sum64(x):
    ri = lax.broadcasted_iota(jnp.int32, (256, 256), 0) // RWKV_HEAD
    ci = lax.broadcasted_iota(jnp.int32, (256, 256), 1) // RWKV_HEAD
    ones = (ri == ci).astype(BF16)
    hi, lo = _split2(x)
    d = functools.partial(jnp.dot, preferred_element_type=F32)
    outs = [d(hi[:, g * 256:(g + 1) * 256], ones) + d(lo[:, g * 256:(g + 1) * 256], ones)
            for g in range(x.shape[1] // 256)]
    return jnp.concatenate(outs, axis=1)


def _adaln_kernel(c_ref, w_ref, b_ref, o_ref):
    c = c_ref[...]
    o_ref[...] = _mm_x3(_silu(c), w_ref[...]) + b_ref[...]


def _adaln(c_all, w_ada, b_ada):
    nl, d, n = w_ada.shape
    tn = 1024
    return pl.pallas_call(
        _adaln_kernel,
        out_shape=jax.ShapeDtypeStruct((nl, MOD_ROWS, n), F32),
        grid=(nl, n // tn),
        in_specs=[pl.BlockSpec((MOD_ROWS, d), lambda l, j: (0, 0)),
                  pl.BlockSpec((None, d, tn), lambda l, j: (l, 0, j)),
                  pl.BlockSpec((None, 1, tn), lambda l, j: (l, 0, j))],
        out_specs=pl.BlockSpec((None, MOD_ROWS, tn), lambda l, j: (l, 0, j)),
        compiler_params=_cparams(("parallel", "parallel")),
        name="adaln",
    )(c_all, w_ada, b_ada.reshape(nl, 1, n))


def _rms(x, g):
    return x * lax.rsqrt(jnp.mean(x * x, axis=-1, keepdims=True) + NORM_EPS) * g


def _prenorm_kernel(x_ref, o0_ref, o1_ref, mprev_ref, mcur_ref, g_ref, *outs, final):
    x = x_ref[...] + mprev_ref[0, 5:6, :] * (o0_ref[...] + o1_ref[...])
    y = _rms(x, g_ref[...])
    if final:
        outs[0][...] = y
        return
    outs[0][...] = x
    outs[1][...] = (y * (1.0 + mcur_ref[0, 1:2, :]) + mcur_ref[0, 0:1, :]).astype(outs[1].dtype)


def _prenorm_first_kernel(xl_ref, xc_ref, mcur_ref, g_ref, xs_o, h_o, *, n_lat_tiles):
    x = jnp.where(pl.program_id(0) < n_lat_tiles, xl_ref[...], xc_ref[...])
    xs_o[...] = x
    h_o[...] = (_rms(x, g_ref[...]) * (1.0 + mcur_ref[0, 1:2, :]) + mcur_ref[0, 0:1, :]).astype(h_o.dtype)


def _mod_row_map(n_lat_tiles, tiles_per_batch, n_batch):
    return lambda i: (jnp.where(i < n_lat_tiles, i // tiles_per_batch, n_batch), 0, 0)


def _prenorm_first(x_lat, x_ctx, mods_cur, g, *, tm, n_lat_tiles, tiles_per_batch, n_batch):
    d = x_lat.shape[1]
    n_rows = x_lat.shape[0] + x_ctx.shape[0]
    row = pl.BlockSpec((tm, d), lambda i: (i, 0))
    return pl.pallas_call(
        functools.partial(_prenorm_first_kernel, n_lat_tiles=n_lat_tiles),
        out_shape=[jax.ShapeDtypeStruct((n_rows, d), F32), jax.ShapeDtypeStruct((n_rows, d), BF16)],
        grid=(n_rows // tm,),
        in_specs=[pl.BlockSpec((tm, d), lambda i: (jnp.minimum(i, n_lat_tiles - 1), 0)),
                  pl.BlockSpec((tm, d), lambda i: (jnp.maximum(i - n_lat_tiles, 0), 0)),
                  pl.BlockSpec((1, 6, d), _mod_row_map(n_lat_tiles, tiles_per_batch, n_batch)),
                  pl.BlockSpec((1, d), lambda i: (0, 0))],
        out_specs=[row, row],
        compiler_params=_cparams(("parallel",)),
        name="prenorm_first",
    )(x_lat, x_ctx, mods_cur, g.reshape(1, d))


def _prenorm(x, moe_out, mods_prev, mods_cur, g, *, tm, n_lat_tiles, tiles_per_batch, n_batch, n_rows, final=False):
    d = x.shape[1]
    row = pl.BlockSpec((tm, d), lambda i: (i, 0))
    mspec = pl.BlockSpec((1, 6, d), _mod_row_map(n_lat_tiles, tiles_per_batch, n_batch))
    slot1 = n_rows // tm
    out_shape = [jax.ShapeDtypeStruct((n_rows, d), F32)]
    if not final:
        out_shape.append(jax.ShapeDtypeStruct((n_rows, d), BF16))
    return pl.pallas_call(
        functools.partial(_prenorm_kernel, final=final),
        out_shape=out_shape,
        grid=(n_rows // tm,),
        in_specs=[row, row, pl.BlockSpec((tm, d), lambda i: (slot1 + i, 0)), mspec, mspec,
                  pl.BlockSpec((1, d), lambda i: (0, 0))],
        out_specs=[row] * len(out_shape),
        compiler_params=_cparams(("parallel",)),
        name="prenorm_final" if final else "prenorm",
    )(x, moe_out, moe_out, mods_prev, mods_cur, g.reshape(1, d))


def _matmul_kernel(a_ref, w_ref, o_ref):
    o_ref[...] = jnp.dot(a_ref[...], w_ref[...], preferred_element_type=F32).astype(o_ref.dtype)


def _matmul(a, w, out_dtype, *, tn, row0=0, n_rows=None):
    k = a.shape[1]
    n = w.shape[1]
    m = a.shape[0] if n_rows is None else n_rows
    tm = next(c for c in (1024, 512, 256, 128, 64) if m % c == 0 and row0 % c == 0)
    i0 = row0 // tm
    return pl.pallas_call(
        _matmul_kernel,
        out_shape=jax.ShapeDtypeStruct((m, n), out_dtype),
        grid=(m // tm, n // tn),
        in_specs=[pl.BlockSpec((tm, k), lambda i, j: (i + i0, 0)),
                  pl.BlockSpec((k, tn), lambda i, j: (0, j))],
        out_specs=pl.BlockSpec((tm, tn), lambda i, j: (i, j)),
        compiler_params=_cparams(("parallel", "parallel")),
        name="in_proj",
    )(a, w)


def _seq_flags(i, n_lat_tiles, tps_lat, tps_ctx):
    is_lat = i < n_lat_tiles
    j = jnp.where(is_lat, i % tps_lat, (i - n_lat_tiles) % tps_ctx)
    return j == 0, j == jnp.where(is_lat, tps_lat - 1, tps_ctx - 1)


def _neighbours(ref, prev_ref, next_ref, c0, c1, first, last):
    tm = ref.shape[0]
    p = ref[:, c0:c1]
    rows = lax.broadcasted_iota(jnp.int32, (tm, 1), 0)
    pr = jnp.where(first, 0.0, prev_ref[7:8, c0:c1])
    nx = jnp.where(last, 0.0, next_ref[0:1, c0:c1])
    prev = jnp.where(rows == 0, pr, pltpu.roll(p, 1, 0))
    nxt = jnp.where(rows == tm - 1, nx, pltpu.roll(p, tm - 1, 0))
    return p, prev, nxt


def _halo_specs(tm, cols, n_rows):
    r8 = tm // 8
    last8 = n_rows // 8 - 1
    return [pl.BlockSpec((tm, cols), lambda i: (i, 0)),
            pl.BlockSpec((8, cols), lambda i: (jnp.maximum(i * r8 - 1, 0), 0)),
            pl.BlockSpec((8, cols), lambda i: (jnp.minimum((i + 1) * r8, last8), 0))]


def _rwkv_prep_kernel(p_ref, pp_ref, pn_ref, mu_ref, w0_ref, w2_ref, a0_ref, a2_ref, g2_ref, kk_ref, ka_ref, rk_ref,
                      r_o, v_o, kap_o, lw_o, b_o, kd_o, bonus_o, gate_o, *, n_lat_tiles, tps_lat, tps_ctx):
    first, last = _seq_flags(pl.program_id(0), n_lat_tiles, tps_lat, tps_ctx)

    def shifted(c0, c1):
        p, prev, nxt = _neighbours(p_ref, pp_ref, pn_ref, c0, c1, first, last)
        mu0, mu1 = mu_ref[0:1, c0:c1], mu_ref[1:2, c0:c1]
        return (1.0 - mu0 - mu1) * p + mu0 * prev + mu1 * nxt

    pair = lambda ref, *idx: (ref[idx + (0,)], ref[idx + (1,)])

    r = shifted(0, RWKV_DIM)
    k = shifted(RWKV_DIM, 2 * RWKV_DIM)
    v = shifted(2 * RWKV_DIM, 3 * RWKV_DIM)
    kk = k * kk_ref[...]
    kap = kk / jnp.maximum(jnp.sqrt(_head_sum64(kk * kk)), 1e-12)
    r_o[...] = r.astype(r_o.dtype)
    v_o[...] = v.astype(v_o.dtype)
    kap_o[...] = kap.astype(kap_o.dtype)
    bonus_o[...] = _head_sum64(r * k * rk_ref[...]) * v
    txw = jnp.tanh(shifted(RW_XW, RW_XA))
    xa = shifted(RW_XA, RW_XG)
    for d in range(2):
        w_log = -_softplus(-(w0_ref[d:d + 1, :] + _mm_x3(txw, pair(w2_ref, d)))) - 0.5
        lw_o[d] = -jnp.exp(w_log)
        a = _sigmoid(a0_ref[d:d + 1, :] + _mm_x3(xa, pair(a2_ref, d)))
        kd_o[d] = (k * (1.0 + (a - 1.0) * ka_ref[...])).astype(kd_o.dtype)
        b_o[d] = (kap * a).astype(b_o.dtype)
    gate_o[...] = _mm_x3(_sigmoid(shifted(RW_XG, RW_COLS)), pair(g2_ref))


def _rwkv_prep(p_rw, mu, w0, w2, a0, a2, g2, k_k, k_a, r_k, *, tm, n_batch, n_lat_tiles, tps_lat, tps_ctx):
    n = p_rw.shape[0]
    gm = _group_major(_scan_groups(n_batch), n_batch, n_lat_tiles, tps_lat, tps_ctx)
    full = lambda a: pl.BlockSpec(a.shape, lambda i: (0,) * a.ndim)
    tok = pl.BlockSpec((tm, RWKV_DIM), lambda i: (i, 0))
    row = pl.BlockSpec((tm, RWKV_DIM), lambda i: (gm(i), 0))
    row2 = pl.BlockSpec((2, tm, RWKV_DIM), lambda i: (0, gm(i), 0))
    one = lambda dt: jax.ShapeDtypeStruct((n, RWKV_DIM), dt)
    two = lambda dt: jax.ShapeDtypeStruct((2, n, RWKV_DIM), dt)
    params = [mu, w0, w2, a0, a2, g2, k_k, k_a, r_k]
    return pl.pallas_call(
        functools.partial(_rwkv_prep_kernel, n_lat_tiles=n_lat_tiles, tps_lat=tps_lat, tps_ctx=tps_ctx),
        out_shape=[one(BF16), one(BF16), one(BF16), two(F32), two(BF16), two(BF16), one(F32), one(F32)],
        grid=(n // tm,),
        in_specs=_halo_specs(tm, RW_COLS, n) + [full(a) for a in params],
        out_specs=[row, row, row, row2, row2, row2, tok, tok],
        compiler_params=_cparams(("parallel",)),
        name="rwkv_prep",
    )(p_rw, p_rw, p_rw, *params)


def _order_masks(n, rev):
    ri = lax.broadcasted_iota(jnp.int32, (n, n), 0)
    ci = lax.broadcasted_iota(jnp.int32, (n, n), 1)
    if rev:
        return ci >= ri, ci > ri, ri == ci
    return ci <= ri, ci < ri, ri == ci


HEADS_PER_GROUP = 4
GROUP = HEADS_PER_GROUP * RWKV_HEAD


def _head_block_mask():
    ri = lax.broadcasted_iota(jnp.int32, (GROUP, GROUP), 0) // RWKV_HEAD
    ci = lax.broadcasted_iota(jnp.int32, (GROUP, GROUP), 1) // RWKV_HEAD
    return ri == ci


def _block_diag(y_g, bdm):
    return jnp.where(bdm, jnp.concatenate([y_g] * HEADS_PER_GROUP, axis=0), jnp.zeros((), y_g.dtype))


def _head_prod(x, y, bdm, nt=False):
    xb, yb = x.astype(BF16), y.astype(BF16)
    dims = (((1,), (1,)), ((), ())) if nt else (((1,), (0,)), ((), ()))
    outs = [lax.dot_general(xb[:, g:g + GROUP], _block_diag(yb[:, g:g + GROUP], bdm), dims,
                            preferred_element_type=F32) for g in range(0, x.shape[1], GROUP)]
    return jnp.concatenate(outs, axis=1)


def _head_prod_tn(a, b, bdm):
    ab, bb = a.astype(BF16), b.astype(BF16)
    outs = []
    for g in range(0, a.shape[1], GROUP):
        p = lax.dot_general(ab[:, g:g + GROUP], bb[:, g:g + GROUP], (((0,), (0,)), ((), ())),
                            preferred_element_type=F32)
        p = jnp.where(bdm, p, 0.0)
        outs.append((p[0:64] + p[64:128]) + (p[128:192] + p[192:256]))
    return jnp.concatenate(outs, axis=1)


def _rwkv_chunks(chains):
    row = lax.broadcasted_iota(jnp.int32, (CHUNK, RWKV_DIM), 0)
    col = lax.broadcasted_iota(jnp.int32, (CHUNK, RWKV_DIM), 1) % RWKV_HEAD
    eye_f = (col == row).astype(F32)
    bdm = _head_block_mask()
    prod = functools.partial(_head_prod, bdm=bdm)
    each = lambda f, *cols: [f(*a) for a in zip(*cols)]
    top, bottom = (lambda a: a[:CHUNK]), (lambda a: a[CHUNK:])
    stack = lambda a, b: jnp.concatenate([a, b], axis=0)

    incl, strict, kr, bt, kdt, vs, e_tot = [], [], [], [], [], [], []
    for r_ref, v_ref, kap_ref, lw_ref, b_ref, kd_ref, _, _, rev in chains:
        incl.append(col >= row if rev else col <= row)
        strict.append(col > row if rev else col < row)
        lw = lw_ref[...]
        gam = _mm_mask(_order_masks(CHUNK, rev)[0].astype(BF16), lw)
        e_tot.append(jnp.exp(jnp.sum(lw, axis=0, keepdims=True)))
        e_neg = jnp.exp(-gam)
        kr.append(stack(kap_ref[...] * jnp.exp(gam - lw), r_ref[...] * jnp.exp(gam)))
        bt.append(b_ref[...] * e_neg)
        kdt.append(kd_ref[...] * e_neg)
        vs.append(v_ref[...])
    g_b = each(lambda a, b: prod(a, b, nt=True), kr, bt)
    g_k = each(lambda a, b: prod(a, b, nt=True), kr, kdt)
    n_ab = each(lambda m, g: jnp.where(m, top(g), 0.0), strict, g_b)
    a_kv = each(lambda m, g: jnp.where(m, top(g), 0.0), strict, g_k)
    m_b = each(lambda m, g: jnp.where(m, bottom(g), 0.0), incl, g_b)
    m_kd = each(lambda m, g: jnp.where(m, bottom(g), 0.0), incl, g_k)
    t_inv = each(lambda n: eye_f - n, n_ab)
    pw = each(lambda n: prod(n, n), n_ab)
    for _ in range(4):
        both = each(lambda t, p: prod(stack(t, p), p), t_inv, pw)
        t_inv = each(lambda t, b: t + top(b), t_inv, both)
        pw = each(bottom, both)
    t_inv = each(lambda t, p: t + prod(t, p), t_inv, pw)
    av = each(lambda a, m, v: prod(stack(a, m), v), a_kv, m_kd, vs)
    s = [c[7][...] for c in chains]
    ks = each(lambda a, b: prod(a, b, nt=True), kr, s)
    u = each(lambda t, k, a: prod(t, top(k) + top(a)), t_inv, ks, av)
    mbu = each(prod, m_b, u)
    upd = each(lambda v, uu, kd, b: _head_prod_tn(stack(v, uu), stack(kd, -b), bdm), vs, u, kdt, bt)
    for c, k, a, m, s0, up, e in zip(chains, ks, av, mbu, s, upd, e_tot):
        c[6][...] = bottom(k) + bottom(a) - m
        c[7][...] = (s0 + up) * e


def _rwkv_scan_kernel(rf_ref, vf_ref, kapf_ref, lwf_ref, bf_ref, kdf_ref, rr_ref, vr_ref, kapr_ref, lwr_ref, br_ref,
                      kdr_ref, yf_ref, yr_ref, sf_ref, sr_ref):
    @pl.when(pl.program_id(1) == 0)
    def _():
        sf_ref[...] = jnp.zeros_like(sf_ref)
        sr_ref[...] = jnp.zeros_like(sr_ref)

    chains = []
    for p in range(rf_ref.shape[0]):
        chains.append((rf_ref.at[p], vf_ref.at[p], kapf_ref.at[p], lwf_ref.at[p], bf_ref.at[p], kdf_ref.at[p],
                       yf_ref.at[p], sf_ref.at[p], False))
        chains.append((rr_ref.at[p], vr_ref.at[p], kapr_ref.at[p], lwr_ref.at[p], br_ref.at[p], kdr_ref.at[p],
                       yr_ref.at[p], sr_ref.at[p], True))
    _rwkv_chunks(chains)


def _scan_block_map(rev, n_lat_blocks, nb_lat, nb_ctx):
    def f(b, s):
        c_ctx = (nb_ctx - 1 - s) if rev else s
        c_lat = (nb_lat - 1 - (s - nb_ctx)) if rev else (s - nb_ctx)
        return jnp.where(s < nb_ctx, n_lat_blocks + b * nb_ctx + c_ctx, b * nb_lat + c_lat)
    return f


def _scan_groups(n_batch):
    return 4 if n_batch % 4 == 0 else (2 if n_batch % 2 == 0 else 1)


def _group_major(n_groups, n_batch, n_lat_tiles, tps_lat, tps_ctx):
    per = n_batch // n_groups
    lat, ctx = per * tps_lat, per * tps_ctx

    def f(i):
        is_lat = i < n_lat_tiles
        b = jnp.where(is_lat, i // tps_lat, (i - n_lat_tiles) // tps_ctx)
        j = jnp.where(is_lat, i % tps_lat, (i - n_lat_tiles) % tps_ctx)
        return (b // per) * (lat + ctx) + jnp.where(is_lat, (b % per) * tps_lat + j, lat + (b % per) * tps_ctx + j)
    return f


def _rwkv_scan(r, v, kap, lw, bb, kd, *, n_batch, nc_lat, nc_ctx):
    n = r.shape[0]
    ng = _scan_groups(n_batch)
    per = n_batch // ng
    g3 = lambda a: a.reshape(ng, n // ng, RWKV_DIM)
    g4 = lambda a: a.reshape(2, ng, n // ng, RWKV_DIM)
    specs = []
    for d, rev in enumerate((False, True)):
        blk = _scan_block_map(rev, per * nc_lat, nc_lat, nc_ctx)
        row = pl.BlockSpec((ng, CHUNK, RWKV_DIM), lambda b, s, blk=blk: (0, blk(b, s), 0))
        rowd = pl.BlockSpec((None, ng, CHUNK, RWKV_DIM), lambda b, s, blk=blk, d=d: (d, 0, blk(b, s), 0))
        specs.append((row, rowd))
    (rowf, rowdf), (rowr, rowdr) = specs
    out = jax.ShapeDtypeStruct((ng, n // ng, RWKV_DIM), F32)
    state = pltpu.VMEM((ng, RWKV_HEAD, RWKV_DIM), F32)
    yf, yr = pl.pallas_call(
        _rwkv_scan_kernel,
        out_shape=[out, out],
        grid=(per, nc_ctx + nc_lat),
        in_specs=[rowf, rowf, rowf, rowdf, rowdf, rowdf, rowr, rowr, rowr, rowdr, rowdr, rowdr],
        out_specs=[rowf, rowr],
        scratch_shapes=[state, state],
        compiler_params=_cparams(("parallel", "arbitrary")),
        name="rwkv_scan",
    )(g3(r), g3(v), g3(kap), g4(lw), g4(bb), g4(kd), g3(r), g3(v), g3(kap), g4(lw), g4(bb), g4(kd))
    return yf.reshape(n, RWKV_DIM), yr.reshape(n, RWKV_DIM)


def _gla_blocks(dirs, n_rows, cw_ref):
    rows = lax.broadcasted_iota(jnp.int32, (n_rows, 1), 0)
    for d in dirs:
        for c0 in range(0, GLA_QKV, 512):
            c1 = c0 + 512
            p = d["load"](c0, c1)
            prev = jnp.where(rows == 0, d["prev_row"](c0, c1), pltpu.roll(p, 1, 0))
            nxt = jnp.where(rows == n_rows - 1, d["next_row"](c0, c1), pltpu.roll(p, n_rows - 1, 0))
            act = _silu(cw_ref[0:1, c0:c1] * prev + cw_ref[1:2, c0:c1] * p + cw_ref[2:3, c0:c1] * nxt)
            if c0 == 0:
                act = act * (GLA_DK ** -0.5)
            d["qkv"][0:n_rows, c0:c1] = act
        z = _mm_x3(d["load"](GLA_QKV, GLA_COLS), d["aw"][...]) + d["ab"][...]
        d["g"][0:n_rows, :] = -_softplus(-z) * (1.0 / GLA_TAU)

    each = lambda f, *cols: [f(*a) for a in zip(*cols)]
    incl = [_order_masks(CHUNK, d["rev"])[0] for d in dirs]
    incl_b = [m.astype(BF16) for m in incl]
    n_chunks = n_rows // CHUNK
    for step in range(n_chunks):
        rs = [slice(c * CHUNK, (c + 1) * CHUNK) for c in ((n_chunks - 1 - step) if d["rev"] else step for d in dirs)]
        g = each(lambda d, r: d["g"][r, :], dirs, rs)
        gc = each(_mm_mask, incl_b, g)
        gl = each(lambda a: jnp.sum(a, axis=0, keepdims=True), g)
        q = each(lambda d, r: d["qkv"][r, 0:GLA_KDIM], dirs, rs)
        k = each(lambda d, r: d["qkv"][r, GLA_KDIM:2 * GLA_KDIM], dirs, rs)
        qg = each(lambda a, c: a * jnp.exp(c), q, gc)
        kg = each(lambda a, c: a * jnp.exp(-c), k, gc)
        ke = each(lambda a, l, c: a * jnp.exp(l - c), k, gl, gc)
        e_l = each(jnp.exp, gl)
        for h in range(GLA_HEADS):
            ks = slice(h * GLA_DK, (h + 1) * GLA_DK)
            vs = slice(2 * GLA_KDIM + h * GLA_DV, 2 * GLA_KDIM + (h + 1) * GLA_DV)
            v_h = each(lambda d, r: d["qkv"][r, vs], dirs, rs)
            att = each(lambda m, a, b: jnp.where(m, _mm_nt(a[:, ks], b[:, ks]), 0.0), incl, qg, kg)
            st = [d["st"][h] for d in dirs]
            out = each(lambda a, v, qq, s0: _mm(a, v) + _mm_nt(qq[:, ks], s0), att, v_h, qg, st)
            new = each(lambda s0, e, v, kk: s0 * e[:, ks] + _mm_tn(v, kk[:, ks]), st, e_l, v_h, ke)
            for d, r, o_h, s1 in zip(dirs, rs, out, new):
                d["o"][r, h * GLA_DV:(h + 1) * GLA_DV] = o_h
                d["st"][h] = s1


def _gla_kernel(uc_ref, ulf_ref, ulfp_ref, ulfn_ref, ulr_ref, ulrp_ref, ulrn_ref, cw_ref, awf_ref, abf_ref, awr_ref,
                abr_ref, ocf_ref, ocr_ref, olf_ref, olr_ref, stf_ref, str_ref, qkvf_s, qkvr_s, gf_s, gr_s, of_s,
                or_s, *, n_wg):
    s = pl.program_id(1)
    grid_rows, wg_cols = ulf_ref.shape[0], ulf_ref.shape[1]
    zero_row = lambda c0, c1: jnp.zeros((1, c1 - c0), F32)
    fwd = dict(aw=awf_ref, ab=abf_ref, st=stf_ref, qkv=qkvf_s, g=gf_s, o=of_s, rev=False)
    bwd = dict(aw=awr_ref, ab=abr_ref, st=str_ref, qkv=qkvr_s, g=gr_s, o=or_s, rev=True)

    @pl.when(s == 0)
    def _():
        stf_ref[...] = jnp.zeros_like(stf_ref)
        str_ref[...] = jnp.zeros_like(str_ref)
        n = uc_ref.shape[0]
        ctx = dict(load=lambda c0, c1: uc_ref[:, c0:c1], prev_row=zero_row, next_row=zero_row)
        _gla_blocks([{**fwd, **ctx}, {**bwd, **ctx}], n, cw_ref)
        ocf_ref[...] = of_s[0:n, :]
        ocr_ref[...] = or_s[0:n, :]

    @pl.when(s > 0)
    def _():
        n = grid_rows * wg_cols

        def lat(u_ref, up_ref, un_ref, wg):
            hr = up_ref.shape[0]
            return dict(
                load=lambda c0, c1: jnp.concatenate([u_ref[:, w, c0:c1] for w in range(wg_cols)], axis=0),
                prev_row=lambda c0, c1: jnp.where(wg == 0, 0.0, up_ref[hr - 1:hr, 7, c0:c1]),
                next_row=lambda c0, c1: jnp.where(wg == n_wg - 1, 0.0, un_ref[0:1, 0, c0:c1]))

        _gla_blocks([{**fwd, **lat(ulf_ref, ulfp_ref, ulfn_ref, s - 1)},
                     {**bwd, **lat(ulr_ref, ulrp_ref, ulrn_ref, n_wg - s)}], n, cw_ref)
        for w in range(wg_cols):
            olf_ref[:, w, :] = of_s[w * grid_rows:(w + 1) * grid_rows, :]
            olr_ref[:, w, :] = or_s[w * grid_rows:(w + 1) * grid_rows, :]


def _gla_scan(u_ctx, u_lat, conv_w, alpha_w, alpha_b, *, n_batch):
    t_ctx, ch = u_ctx.shape
    lc = t_ctx // n_batch
    _, grid_rows, grid_w, _ = u_lat.shape
    wg_cols = min(grid_w, max(8, 256 // grid_rows))
    n_wg = grid_w // wg_cols
    assert (wg_cols * grid_rows) % CHUNK == 0 and lc % CHUNK == 0 and grid_w % wg_cols == 0 and wg_cols % 8 == 0
    hr = 8 if grid_rows % 8 == 0 else grid_rows
    n_rows = max(lc, wg_cols * grid_rows)

    def group(rev):
        return lambda s: jnp.clip((n_wg - s) if rev else (s - 1), 0, n_wg - 1)

    full = lambda a: pl.BlockSpec(a.shape, lambda b, s: (0,) * a.ndim)
    lat = lambda c, wg: pl.BlockSpec((None, grid_rows, wg_cols, c), lambda b, s: (b, 0, wg(s), 0))

    def lat_in(wg):
        return [lat(ch, wg),
                pl.BlockSpec((None, hr, 8, ch), lambda b, s: (
                    b, grid_rows // hr - 1, jnp.maximum(wg(s) * (wg_cols // 8) - 1, 0), 0)),
                pl.BlockSpec((None, hr, 8, ch), lambda b, s: (
                    b, 0, jnp.minimum((wg(s) + 1) * (wg_cols // 8), grid_w // 8 - 1), 0))]

    ctx_out = jax.ShapeDtypeStruct((t_ctx, GLA_VDIM), F32)
    lat_out = jax.ShapeDtypeStruct((n_batch, grid_rows, grid_w, GLA_VDIM), F32)
    ctx_spec = pl.BlockSpec((lc, GLA_VDIM), lambda b, s: (b, 0))
    per_dir = lambda shape: [pltpu.VMEM(shape, F32)] * 2
    params = [conv_w, alpha_w[0], alpha_b[0], alpha_w[1], alpha_b[1]]
    return pl.pallas_call(
        functools.partial(_gla_kernel, n_wg=n_wg),
        out_shape=[ctx_out, ctx_out, lat_out, lat_out],
        grid=(n_batch, 1 + n_wg),
        in_specs=[pl.BlockSpec((lc, ch), lambda b, s: (b, 0))] + lat_in(group(False)) + lat_in(group(True))
        + [full(a) for a in params],
        out_specs=[ctx_spec, ctx_spec, lat(GLA_VDIM, group(False)), lat(GLA_VDIM, group(True))],
        scratch_shapes=per_dir((GLA_HEADS, GLA_DV, GLA_DK)) + per_dir((n_rows, GLA_QKV))
        + per_dir((n_rows, GLA_KDIM)) + per_dir((n_rows, GLA_VDIM)),
        compiler_params=_cparams(("parallel", "arbitrary")),
        name="gla_scan",
    )(u_ctx, u_lat, u_lat, u_lat, u_lat, u_lat, u_lat, *params)


def _pair_tables():
    ri = lax.broadcasted_iota(jnp.int32, (LANES, LANES), 0)
    ci = lax.broadcasted_iota(jnp.int32, (LANES, LANES), 1)
    grp = ci // 6
    pr = ci % 6
    lo = jnp.where(pr < 3, 0, jnp.where(pr < 5, 1, 2))
    hi = jnp.where(pr < 3, pr + 1, jnp.where(pr < 5, pr - 1, 3))
    member = ((ri == grp * EXPERTS_PER_GROUP + lo) | (ri == grp * EXPERTS_PER_GROUP + hi)) & (ci < N_PAIRS)
    return member


def _route(h, rw_ref, rb_ref):
    lane = lax.broadcasted_iota(jnp.int32, (h.shape[0], LANES), 1)
    scores = _sigmoid(_mm_x3(h, (rw_ref[0], rw_ref[1])))
    sel = jnp.where(lane < N_EXPERTS, scores + rb_ref[...], 0.0)
    member = _pair_tables()
    s3 = _split3(sel)
    mb = member.astype(BF16)
    d = functools.partial(jnp.dot, preferred_element_type=F32)
    ps = d(s3[0], mb) + (d(s3[1], mb) + d(s3[2], mb))
    ps = jnp.where(lane < N_PAIRS, ps, -jnp.inf)
    best = jnp.max(ps, axis=-1, keepdims=True)
    bucket = jnp.min(jnp.where(ps == best, lane, LANES), axis=-1, keepdims=True)
    onehot = (lane == bucket).astype(BF16)
    chosen = lax.dot_general(onehot, mb, (((1,), (1,)), ((), ())), preferred_element_type=F32) > 0.5
    picked = jnp.where(chosen, scores, 0.0)
    gates = picked / jnp.sum(picked, axis=-1, keepdims=True)
    e_lo = jnp.min(jnp.where(chosen, lane, LANES), axis=-1, keepdims=True)
    e_hi = jnp.max(jnp.where(chosen, lane, -1), axis=-1, keepdims=True)
    w_lo = jnp.sum(jnp.where(lane == e_lo, gates, 0.0), axis=-1, keepdims=True)
    w_hi = jnp.sum(jnp.where(lane == e_hi, gates, 0.0), axis=-1, keepdims=True)
    out = jnp.where(lane == 0, e_lo.astype(F32), 0.0)
    out = jnp.where(lane == 1, e_hi.astype(F32), out)
    out = jnp.where(lane == 2, w_lo, out)
    return jnp.where(lane == 3, w_hi, out)


def _merge_kernel(x_ref, y0_ref, y1_ref, bonus_ref, gate_ref, ol0_ref, ol1_ref, oc0_ref, oc1_ref, pg_ref, mods_ref,
                  lng_ref, lnb_ref, gng_ref, wa_ref, wb_ref, wo_ref, nfg_ref, rw_ref, rb_ref,
                  xs_o, h_o, route_o, *, n_lat_tiles):
    y = y0_ref[...] + y1_ref[...]
    mean = _head_sum64(y) * (1.0 / RWKV_HEAD)
    yc = y - mean
    var = _head_sum64(yc * yc) * (1.0 / RWKV_HEAD)
    ya = (yc * lax.rsqrt(var + RWKV_GN_EPS) * lng_ref[...] + lnb_ref[...] + bonus_ref[...]) * gate_ref[...]
    o = jnp.where(pl.program_id(0) < n_lat_tiles, ol0_ref[...] + ol1_ref[...], oc0_ref[...] + oc1_ref[...])
    parts = []
    for h in range(GLA_HEADS):
        o_h = o[:, h * GLA_DV:(h + 1) * GLA_DV]
        parts.append(o_h * lax.rsqrt(jnp.mean(o_h * o_h, axis=-1, keepdims=True) + NORM_EPS) * gng_ref[...])
    yb = jnp.concatenate(parts, axis=1) * _silu(pg_ref[:, 0:GLA_VDIM].astype(F32))
    ga = _sigmoid(pg_ref[:, GLA_VDIM:GLA_VDIM + D_MODEL].astype(F32))
    gb = _sigmoid(pg_ref[:, GLA_VDIM + D_MODEL:GATE_COLS].astype(F32))
    z = ga * _mm(ya, wa_ref[...]) + gb * _mm(yb, wb_ref[...])
    xs = x_ref[...] + mods_ref[0, 2:3, :] * _mm(z, wo_ref[...])
    xs_o[...] = xs
    h = _rms(xs, nfg_ref[...]) * (1.0 + mods_ref[0, 4:5, :]) + mods_ref[0, 3:4, :]
    h_o[...] = h
    route_o[...] = _route(h, rw_ref, rb_ref)


def _merge(x, y0, y1, bonus, gate, o_lat, o_ctx, pg, mods, ln_g, ln_b, gn_g, wa, wb, wo, nf_g, rw, rb,
           *, layer, tm, n_rows, n_lat_tiles, tiles_per_batch, n_batch, tps_ctx):
    d = D_MODEL
    mrow = _mod_row_map(n_lat_tiles, tiles_per_batch, n_batch)
    gm = _group_major(_scan_groups(n_batch), n_batch, n_lat_tiles, tiles_per_batch, tps_ctx)
    row = lambda c: pl.BlockSpec((tm, c), lambda i: (i, 0))
    yrow = pl.BlockSpec((tm, RWKV_DIM), lambda i: (gm(i), 0))
    lat = pl.BlockSpec((tm, GLA_VDIM), lambda i: (jnp.minimum(i, n_lat_tiles - 1), 0))
    ctx = pl.BlockSpec((tm, GLA_VDIM), lambda i: (jnp.maximum(i - n_lat_tiles, 0), 0))
    full = lambda a: pl.BlockSpec(a.shape, lambda i: (0,) * a.ndim)
    of_layer = lambda a: pl.BlockSpec((None,) + a.shape[1:], lambda i: (layer, 0, 0))
    consts = [ln_g.reshape(1, -1), ln_b.reshape(1, -1), gn_g.reshape(1, -1), wa, wb, wo, nf_g.reshape(1, -1), rw, rb]
    const_specs = [full(a) for a in consts[:3]] + [of_layer(a) for a in consts[3:6]] + [full(a) for a in consts[6:]]
    return pl.pallas_call(
        functools.partial(_merge_kernel, n_lat_tiles=n_lat_tiles),
        out_shape=[jax.ShapeDtypeStruct((n_rows, d), F32), jax.ShapeDtypeStruct((n_rows, d), F32),
                   jax.ShapeDtypeStruct((n_rows, LANES), F32)],
        grid=(n_rows // tm,),
        in_specs=[row(d), yrow, yrow, row(RWKV_DIM), row(RWKV_DIM), lat, lat, ctx, ctx,
                  row(GATE_COLS), pl.BlockSpec((1, 6, d), mrow)] + const_specs,
        out_specs=[row(d), row(d), row(LANES)],
        compiler_params=_cparams(("parallel",)),
        name="merge",
    )(x, y0, y1, bonus, gate, o_lat[0], o_lat[1], o_ctx[0], o_ctx[1], pg, mods, *consts)


DMA_ISSUE_UNROLL = 8
MOE_UP_CHUNK = 256


def _moe_kernel(te_ref, nu_ref, src_ref, src_next_ref, dst_prev_ref, dst_ref, w_ref, h_hbm, wg_ref, wu_ref, wd_ref,
                o_hbm, xbuf, obuf, act_s, sem_in, sem_out, sem_fill, *, n_real_rows):
    del te_ref
    tm = xbuf.shape[1]
    i = pl.program_id(0)
    n_used = nu_ref[0]
    slot = i % 2

    other = 1 - slot

    def gather_row(idx_ref, s, r):
        return pltpu.make_async_copy(h_hbm.at[pl.ds(idx_ref[0, 0, r], 1)], xbuf.at[s, pl.ds(r, 1)], sem_in.at[s])

    def scatter_row(idx_ref, s, r):
        return pltpu.make_async_copy(obuf.at[s, pl.ds(r, 1)], o_hbm.at[pl.ds(idx_ref[0, 0, r], 1)], sem_out.at[s])

    def wait_gather(s):
        pltpu.make_async_copy(h_hbm.at[pl.ds(0, tm)], xbuf.at[s], sem_in.at[s]).wait()

    def wait_scatter(s):
        pltpu.make_async_copy(obuf.at[s], o_hbm.at[pl.ds(0, tm)], sem_out.at[s]).wait()

    def issue_all(make_row):
        def body(r, carry):
            make_row(r).start()
            return carry
        lax.fori_loop(0, tm, body, 0, unroll=DMA_ISSUE_UNROLL)

    @pl.when(i < n_used)
    def _():
        @pl.when(i == 0)
        def _():
            obuf[1] = jnp.zeros(obuf.shape[1:], obuf.dtype)
            fills = [pltpu.make_async_copy(obuf.at[1], o_hbm.at[pl.ds(n_real_rows + k * tm, tm)], sem_fill)
                     for k in range((o_hbm.shape[0] - n_real_rows) // tm)]
            for f in fills:
                f.start()
            for f in fills:
                f.wait()
            issue_all(lambda r: gather_row(src_ref, 0, r))

        @pl.when(i >= 1)
        def _():
            wait_scatter(slot)

        wait_gather(slot)
        bounds = list(range(0, D_EXPERT, MOE_UP_CHUNK)) + [D_EXPERT]
        n_chunks = len(bounds) - 1
        def up(c):
            xb = xbuf[slot].astype(BF16)
            cols = slice(bounds[c], bounds[c + 1])
            return (jnp.dot(xb, wg_ref[:, cols], preferred_element_type=F32),
                    jnp.dot(xb, wu_ref[:, cols], preferred_element_type=F32))

        g, u = up(0)
        for c in range(n_chunks):
            g_next, u_next = up(c + 1) if c + 1 < n_chunks else (None, None)
            act_s[:, bounds[c]:bounds[c + 1]] = (_silu(g) * u).astype(BF16)
            g, u = g_next, u_next
            for r in range(c * tm // n_chunks, (c + 1) * tm // n_chunks):
                gather_row(src_next_ref, other, r).start()
                scatter_row(dst_prev_ref, other, r).start()

        obuf[slot] = jnp.dot(act_s[...], wd_ref[...], preferred_element_type=F32) * w_ref[...][:, 0:1]

        @pl.when(i == n_used - 1)
        def _():
            issue_all(lambda r: scatter_row(dst_ref, slot, r))
            wait_scatter(other)
            wait_scatter(slot)
            wait_gather(other)


def _moe(h, route, wg, wu, wd, *, layer, tm):
    t, d = h.shape
    i32 = jnp.int32
    ids = jnp.concatenate([route[:, 0], route[:, 1]]).astype(i32)
    wts = jnp.concatenate([route[:, 2], route[:, 3]])
    n_tiles = (2 * t) // tm + N_EXPERTS
    n_spill = (N_EXPERTS + 1) * tm
    _, order, wts_s = lax.sort((ids, jnp.arange(2 * t, dtype=i32), wts), num_keys=1, is_stable=True)
    counts = jnp.sum((ids[:, None] == jnp.arange(N_EXPERTS, dtype=i32)[None, :]).astype(i32), axis=0)
    starts = jnp.cumsum(counts) - counts
    pcounts = ((counts + tm - 1) // tm) * tm
    pend = jnp.cumsum(pcounts)
    poff = pend - pcounts
    n_used = (pend[-1] // tm).astype(i32)
    tile = jnp.arange(n_tiles, dtype=i32)
    te = jnp.sum((tile[:, None] * tm >= pend[None, :]).astype(i32), axis=1)
    te = jnp.where(tile < n_used, te, jnp.take(te, jnp.maximum(n_used - 1, 0))).astype(i32)
    rank0 = tile * tm - jnp.take(poff, te)
    n_valid = jnp.where(tile < n_used, jnp.clip(jnp.take(counts, te) - rank0, 0, tm), 0)
    n_pos = n_tiles * tm
    shift = poff - starts
    e_tile = te[:, None]

    def spread(arr):
        ext = jnp.concatenate([jnp.zeros((n_pos,), arr.dtype), arr, jnp.zeros((n_pos,), arr.dtype)])
        out = jnp.zeros((n_tiles, tm), arr.dtype)
        for e in range(N_EXPERTS):
            seg = lax.dynamic_slice(ext, (n_pos - shift[e],), (n_pos,)).reshape(n_tiles, tm)
            out = jnp.where(e_tile == e, seg, out)
        return out

    a = spread(order)
    w_tile = spread(wts_s)
    valid = jnp.arange(tm, dtype=i32)[None, :] < n_valid[:, None]
    src = jnp.where(valid, a % t, 0).astype(i32)
    pad = (~valid) & (tile < n_used)[:, None]
    spill = 2 * t + tm + jnp.minimum(jnp.cumsum(pad.reshape(-1).astype(i32)) - 1, n_spill - tm - 1).reshape(n_tiles, tm)
    dst = jnp.where(valid, a, spill).astype(i32)
    dst = jnp.concatenate([2 * t + jnp.arange(tm, dtype=i32)[None, :], dst], axis=0)
    wrow = jnp.broadcast_to(jnp.where(valid, w_tile, 0.0).reshape(-1, 1), (n_tiles * tm, LANES))

    smem = functools.partial(pl.BlockSpec, (1, 1, tm), memory_space=pltpu.SMEM)
    src3, dst3 = src.reshape(n_tiles, 1, tm), dst.reshape(n_tiles + 1, 1, tm)
    return pl.pallas_call(
        functools.partial(_moe_kernel, n_real_rows=2 * t),
        out_shape=jax.ShapeDtypeStruct((2 * t + n_spill, d), F32),
        grid_spec=pltpu.PrefetchScalarGridSpec(
            num_scalar_prefetch=2,
            grid=(n_tiles,),
            in_specs=[smem(lambda i, *_: (i, 0, 0)),
                      smem(lambda i, *_: (jnp.minimum(i + 1, n_tiles - 1), 0, 0)),
                      smem(lambda i, *_: (i, 0, 0)),
                      smem(lambda i, *_: (i + 1, 0, 0)),
                      pl.BlockSpec((tm, LANES), lambda i, *_: (i, 0)),
                      pl.BlockSpec(memory_space=pl.ANY),
                      pl.BlockSpec((None, None, d, D_EXPERT), lambda i, te_r, *_: (layer, te_r[i], 0, 0)),
                      pl.BlockSpec((None, None, d, D_EXPERT), lambda i, te_r, *_: (layer, te_r[i], 0, 0)),
                      pl.BlockSpec((None, None, D_EXPERT, d), lambda i, te_r, *_: (layer, te_r[i], 0, 0))],
            out_specs=pl.BlockSpec(memory_space=pl.ANY),
            scratch_shapes=[pltpu.VMEM((2, tm, d), F32), pltpu.VMEM((2, tm, d), F32),
                            pltpu.VMEM((tm, D_EXPERT), BF16),
                            pltpu.SemaphoreType.DMA((2,)), pltpu.SemaphoreType.DMA((2,)), pltpu.SemaphoreType.DMA]),
        compiler_params=_cparams(("arbitrary",)),
        name="moe_ffn",
    )(te, n_used.reshape(1), src3, src3, dst3, dst3, wrow, h, wg, wu, wd)


def _pad_cols(a, n):
    return jnp.pad(a, [(0, 0)] * (a.ndim - 1) + [(0, n - a.shape[-1])])


def _pad_rows(a, n):
    return jnp.pad(a, [(0, 0)] * (a.ndim - 2) + [(0, n - a.shape[-2]), (0, 0)])


def _split_in_proj(w_in, mu):
    s = (RWKV_DIM, 2 * RWKV_DIM, 3 * RWKV_DIM, 3 * RWKV_DIM + DECAY_LORA, 3 * RWKV_DIM + DECAY_LORA + ICLR_LORA)

    def rw_layout(a):
        return jnp.concatenate([a[..., :s[2]], _pad_cols(a[..., s[2]:s[3]], LORA_PAD),
                                _pad_cols(a[..., s[3]:s[4]], LORA_PAD), a[..., s[4]:RWKV_IN]], axis=-1)

    w_in = w_in.astype(BF16)
    w_rw = rw_layout(w_in)
    w_gla = _pad_cols(w_in[:, RWKV_IN:GLA_GATE_OFF], GLA_COLS)
    w_gate = w_in[:, GLA_GATE_OFF:]
    return w_rw, w_gla, w_gate, rw_layout(mu)


def kernel(x, c, ctx, c_ctx, w_ada, b_ada, norm_mix_g, norm_ffn_g, w_in, rwkv_mu, rwkv_w0, rwkv_w2, rwkv_a0,
           rwkv_a2, rwkv_g2, rwkv_k_k, rwkv_k_a, rwkv_r_k, rwkv_ln_g, rwkv_ln_b, gla_conv, gla_alpha_w2,
           gla_alpha_b, gla_norm_g, w_branch_a, w_branch_b, w_out, router_w, router_b, exp_w_gate, exp_w_up,
           exp_w_down, final_norm_g):
    n_batch, seq, d = x.shape
    ctx_len = ctx.shape[1]
    assert d == D_MODEL and seq % GRID_W == 0 and n_batch < MOD_ROWS
    grid_rows = seq // GRID_W
    tm = min(256, ctx_len)
    assert seq % tm == 0 and ctx_len % tm == 0 and tm % CHUNK == 0
    t_lat, t_ctx = n_batch * seq, n_batch * ctx_len
    t_all = t_lat + t_ctx
    tiles = dict(tm=tm, n_lat_tiles=t_lat // tm)
    seq_tiles = dict(tps_lat=seq // tm, tps_ctx=ctx_len // tm)
    mod_tiles = dict(tiles_per_batch=seq // tm, n_batch=n_batch)

    c_all = jnp.zeros((MOD_ROWS, d), F32).at[:n_batch].set(c).at[n_batch].set(c_ctx)
    mods = _adaln(c_all, w_ada, b_ada).reshape(DEPTH, MOD_ROWS, 6, d)
    rw_pad = _presplit(_pad_cols(router_w, LANES))
    wa_all, wb_all, wo_all = (w.astype(BF16) for w in (w_branch_a, w_branch_b, w_out))
    wg_all, wu_all, wd_all = (w.astype(BF16) for w in (exp_w_gate, exp_w_up, exp_w_down))
    rb_pad = _pad_cols(router_b.reshape(1, -1), LANES)

    moe_out = None
    for l in range(DEPTH):
        last = l == DEPTH - 1
        if moe_out is None:
            xs, h = _prenorm_first(x.reshape(t_lat, d), ctx.reshape(t_ctx, d), mods[l], norm_mix_g[l],
                                   **tiles, **mod_tiles)
        else:
            xs, h = _prenorm(xs, moe_out, mods[l - 1], mods[l], norm_mix_g[l], n_rows=t_all, **tiles, **mod_tiles)
        w_rw, w_gla, w_gate, mu = _split_in_proj(w_in[l], rwkv_mu[l])
        p_rw = _matmul(h, w_rw, F32, tn=RW_COLS // 2)
        p_gla_lat = _matmul(h, w_gla, F32, tn=GLA_COLS, n_rows=t_lat)
        p_gla_ctx = _matmul(h, w_gla, F32, tn=GLA_COLS, row0=t_lat, n_rows=t_ctx)
        p_gate = _matmul(h, w_gate, BF16, tn=GATE_COLS // 5)

        r, v, kap, lw, bb, kd, bonus, gate = _rwkv_prep(
            p_rw, mu, rwkv_w0[l], _presplit(_pad_rows(rwkv_w2[l], LORA_PAD)), rwkv_a0[l],
            _presplit(_pad_rows(rwkv_a2[l], LORA_PAD)), _presplit(rwkv_g2[l]), rwkv_k_k[l].reshape(1, -1), rwkv_k_a[l].reshape(1, -1), rwkv_r_k[l].reshape(1, -1),
            n_batch=n_batch, **tiles, **seq_tiles)
        y0, y1 = _rwkv_scan(r, v, kap, lw, bb, kd, n_batch=n_batch, nc_lat=seq // CHUNK, nc_ctx=ctx_len // CHUNK)

        u_lat = p_gla_lat.reshape(n_batch, grid_rows, GRID_W, GLA_COLS)
        ocf, ocr, olf, olr = _gla_scan(p_gla_ctx, u_lat, gla_conv[l], _pad_rows(gla_alpha_w2[l], LANES),
                                       gla_alpha_b[l].reshape(2, 1, -1), n_batch=n_batch)
        o_ctx = [ocf, ocr]
        o_lat = [olf.reshape(t_lat, GLA_VDIM), olr.reshape(t_lat, GLA_VDIM)]

        n_rows = t_lat if last else t_all
        xs, hf, route = _merge(
            xs, y0, y1, bonus, gate, o_lat, o_ctx, p_gate, mods[l], rwkv_ln_g[l], rwkv_ln_b[l], gla_norm_g[l],
            wa_all, wb_all, wo_all, norm_ffn_g[l],
            rw_pad, rb_pad, layer=l, n_rows=n_rows, tps_ctx=ctx_len // tm, **tiles, **mod_tiles)
        moe_out = _moe(hf, route, wg_all, wu_all, wd_all, layer=l, tm=tm)

    (out,) = _prenorm(xs, moe_out, mods[DEPTH - 1], mods[DEPTH - 1], final_norm_g, n_rows=t_lat, final=True,
                      **tiles, **mod_tiles)
    return out.reshape(n_batch, seq, d)
```
